```python
import math
import jax, jax.numpy as jnp
from jax import lax
import numpy as np

D_MODEL = 2048
BATCH = 4
SEQ = 8192
DEPTH = 1

CHUNK = 64
Q_BLOCK = 128
FOX_HEADS = 8
FOX_HEAD_DIM = 128
FOX_WIDTH = FOX_HEADS * FOX_HEAD_DIM
GDN_HEADS = 8
GDN_HEAD_DIM = 128
GDN_WIDTH = GDN_HEADS * GDN_HEAD_DIM
MIX_WIDTH = FOX_WIDTH + GDN_WIDTH
CONV_WIDTH = 4
EPS = 1e-6
IN_SIZES = (FOX_WIDTH, FOX_WIDTH, FOX_WIDTH, FOX_WIDTH, FOX_HEADS,
            GDN_WIDTH, GDN_WIDTH, GDN_WIDTH, GDN_WIDTH, GDN_HEADS, GDN_HEADS)
IN_WIDTH = 4 * FOX_WIDTH + FOX_HEADS + 4 * GDN_WIDTH + 2 * GDN_HEADS

kernel_name = "hybrid_fox_gdn_adaln_block"


def rmsnorm(x, g):
    xf = x.astype(jnp.float32)
    y = xf * lax.rsqrt(jnp.mean(xf * xf, axis=-1, keepdims=True) + EPS)
    return (y * g.astype(jnp.float32)).astype(x.dtype)


def l2norm(x):
    xf = x.astype(jnp.float32)
    return xf * lax.rsqrt(jnp.sum(xf * xf, axis=-1, keepdims=True) + EPS)


def to_heads(t, n_heads):
    b, s, w = t.shape
    return t.reshape(b, s, n_heads, w // n_heads).transpose(0, 2, 1, 3)


def from_heads(t):
    b, n, s, d = t.shape
    return t.transpose(0, 2, 1, 3).reshape(b, s, n * d)


def split_cols(p):
    idx = np.cumsum(np.array(IN_SIZES))[:-1].tolist()
    return jnp.split(p, idx, axis=-1)


def forgetting_attention(q, k, v, f_logit, b_f, qn_g, kn_g):
    b, s, _ = q.shape
    qh = rmsnorm(to_heads(q, FOX_HEADS), qn_g).astype(jnp.float32)
    kh = rmsnorm(to_heads(k, FOX_HEADS), kn_g).astype(jnp.float32)
    vh = to_heads(v, FOX_HEADS).astype(jnp.float32)
    log_f = jax.nn.log_sigmoid(f_logit.astype(jnp.float32) + b_f.astype(jnp.float32))
    F = jnp.cumsum(log_f, axis=1).transpose(0, 2, 1)
    nq = s // Q_BLOCK
    qb = qh.reshape(b, FOX_HEADS, nq, Q_BLOCK, FOX_HEAD_DIM).transpose(2, 0, 1, 3, 4)
    Fb = F.reshape(b, FOX_HEADS, nq, Q_BLOCK).transpose(2, 0, 1, 3)
    tb = jnp.arange(s, dtype=jnp.int32).reshape(nq, Q_BLOCK)
    s_pos = jnp.arange(s, dtype=jnp.int32)
    scale = FOX_HEAD_DIM ** -0.5

    def block(args):
        q_i, F_i, t_i = args
        logits = (jnp.einsum('bhqd,bhkd->bhqk', q_i, kh) * scale
                  + (F_i[..., :, None] - F[..., None, :]))
        mask = s_pos[None, :] <= t_i[:, None]
        logits = jnp.where(mask, logits, -jnp.inf)
        p = jax.nn.softmax(logits, axis=-1)
        return jnp.einsum('bhqk,bhkd->bhqd', p, vh)

    o = lax.map(block, (qb, Fb, tb))
    return o.transpose(1, 0, 3, 2, 4).reshape(b, s, FOX_WIDTH)


def causal_depthwise_conv(x, w):
    c = x.shape[-1]
    return lax.conv_general_dilated(
        x, w[:, None, :], window_strides=(1,), padding=[(CONV_WIDTH - 1, 0)],
        dimension_numbers=('NWC', 'WIO', 'NWC'), feature_group_count=c)


def chunk_gated_delta_rule(q, k, v, g, beta):
    b, h, s, dk = q.shape
    dv = v.shape[-1]
    n = s // CHUNK
    q = q.reshape(b, h, n, CHUNK, dk)
    k = k.reshape(b, h, n, CHUNK, dk)
    v = v.reshape(b, h, n, CHUNK, dv)
    g = g.reshape(b, h, n, CHUNK)
    beta = beta.reshape(b, h, n, CHUNK)
    gc = jnp.cumsum(g, axis=-1)
    idx = jnp.arange(CHUNK)
    lower = idx[:, None] >= idx[None, :]
    strict = idx[:, None] > idx[None, :]
    decay = jnp.exp(jnp.where(lower, gc[..., :, None] - gc[..., None, :], -jnp.inf))
    kb = k * beta[..., None]
    vb = v * beta[..., None]
    M = jnp.where(strict, jnp.einsum('bhncd,bhnsd->bhncs', kb, k) * decay, 0.0)
    A = M + jnp.eye(CHUNK, dtype=jnp.float32)
    rhs = jnp.concatenate([vb, kb * jnp.exp(gc)[..., None]], axis=-1)
    sol = lax.linalg.triangular_solve(A, rhs, left_side=True, lower=True, unit_diagonal=True)
    u, w = sol[..., :dv], sol[..., dv:]
    attn = jnp.where(lower, jnp.einsum('bhncd,bhnsd->bhncs', q, k) * decay, 0.0)

    def step(state, xs):
        q_i, k_i, u_i, w_i, gc_i, attn_i = xs
        v_new = u_i - jnp.einsum('bhcd,bhde->bhce', w_i, state)
        o_i = (jnp.einsum('bhcd,bhde->bhce', q_i * jnp.exp(gc_i)[..., None], state)
               + jnp.einsum('bhcs,bhse->bhce', attn_i, v_new))
        g_last = gc_i[..., -1]
        k_dec = k_i * jnp.exp(g_last[..., None] - gc_i)[..., None]
        new_state = state * jnp.exp(g_last)[..., None, None] + jnp.einsum('bhcd,bhce->bhde', k_dec, v_new)
        return new_state, o_i

    mv = lambda t: jnp.moveaxis(t, 2, 0)
    state0 = jnp.zeros((b, h, dk, dv), jnp.float32)
    _, o = lax.scan(step, state0, (mv(q), mv(k), mv(u), mv(w), mv(gc), mv(attn)))
    return o.transpose(1, 2, 0, 3, 4).reshape(b, h, s, dv)


def gated_deltanet(q, k, v, a, bt, conv_w, A_log, dt_bias, norm_g):
    qkv = jnp.concatenate([q, k, v], axis=-1).astype(jnp.float32)
    qkv = jax.nn.silu(causal_depthwise_conv(qkv, conv_w.astype(jnp.float32)))
    q, k, v = jnp.split(qkv, 3, axis=-1)
    qh = l2norm(to_heads(q, GDN_HEADS)) * (GDN_HEAD_DIM ** -0.5)
    kh = l2norm(to_heads(k, GDN_HEADS))
    vh = to_heads(v, GDN_HEADS)
    beta = jax.nn.sigmoid(bt.astype(jnp.float32)).transpose(0, 2, 1)
    g = (-jnp.exp(A_log.astype(jnp.float32))
         * jax.nn.softplus(a.astype(jnp.float32) + dt_bias.astype(jnp.float32))).transpose(0, 2, 1)
    o = chunk_gated_delta_rule(qh, kh, vh, g, beta)
    o = rmsnorm(o, norm_g)
    return from_heads(o)


def setup_inputs(seed: int = 0) -> dict:
    key = jax.random.key(seed)
    ks = jax.random.split(key, 16)
    f32 = jnp.float32
    x = jax.random.normal(ks[0], (BATCH, SEQ, D_MODEL), f32)
    c = jax.random.normal(ks[1], (BATCH, D_MODEL), f32)
    norm_g = 1.0 + 0.02 * jax.random.normal(ks[2], (DEPTH, D_MODEL), f32)
    w_ada = 0.5 * D_MODEL ** -0.5 * jax.random.normal(ks[3], (DEPTH, D_MODEL, 3 * D_MODEL), f32)
    b_ada = 0.01 * jax.random.normal(ks[4], (DEPTH, 3 * D_MODEL), f32)
    w_in = D_MODEL ** -0.5 * jax.random.normal(ks[5], (DEPTH, D_MODEL, IN_WIDTH), f32)
    b_fgate = jax.random.uniform(ks[6], (DEPTH, FOX_HEADS), f32, 1.0, 4.0)
    fox_qn_g = 1.0 + 0.02 * jax.random.normal(ks[7], (DEPTH, FOX_HEAD_DIM), f32)
    fox_kn_g = 1.0 + 0.02 * jax.random.normal(ks[8], (DEPTH, FOX_HEAD_DIM), f32)
    gdn_conv_w = CONV_WIDTH ** -0.5 * jax.random.normal(ks[9], (DEPTH, CONV_WIDTH, 3 * GDN_WIDTH), f32)
    gdn_A_log = jnp.log(jax.random.uniform(ks[10], (DEPTH, GDN_HEADS), f32, 1.0, 16.0))
    dt = jnp.exp(jax.random.uniform(ks[11], (DEPTH, GDN_HEADS), f32, math.log(1e-3), math.log(1e-1)))
    gdn_dt_bias = dt + jnp.log(-jnp.expm1(-dt))
    gdn_norm_g = 1.0 + 0.02 * jax.random.normal(ks[12], (DEPTH, GDN_HEAD_DIM), f32)
    w_out = MIX_WIDTH ** -0.5 * jax.random.normal(ks[13], (DEPTH, MIX_WIDTH, D_MODEL), f32)
    final_g = 1.0 + 0.02 * jax.random.normal(ks[14], (D_MODEL,), f32)
    return {"x": x, "c": c, "norm_g": norm_g, "w_ada": w_ada, "b_ada": b_ada,
            "w_in": w_in, "b_fgate": b_fgate, "fox_qn_g": fox_qn_g, "fox_kn_g": fox_kn_g,
            "gdn_conv_w": gdn_conv_w, "gdn_A_log": gdn_A_log, "gdn_dt_bias": gdn_dt_bias,
            "gdn_norm_g": gdn_norm_g, "w_out": w_out, "final_g": final_g}


def reference(x, c, norm_g, w_ada, b_ada, w_in, b_fgate, fox_qn_g, fox_kn_g,
              gdn_conv_w, gdn_A_log, gdn_dt_bias, gdn_norm_g, w_out, final_g):
    c_act = jax.nn.silu(c)
    for l in range(DEPTH):
        mod = c_act @ w_ada[l] + b_ada[l]
        shift, scale, gate = jnp.split(mod, 3, axis=-1)
        h = rmsnorm(x, norm_g[l]) * (1.0 + scale[:, None, :]) + shift[:, None, :]
        p = h @ w_in[l]
        fq, fk, fv, fz, ff, gq, gk, gv, gz, ga, gb = split_cols(p)
        fox_o = forgetting_attention(fq, fk, fv, ff, b_fgate[l], fox_qn_g[l], fox_kn_g[l])
        fox_o = fox_o * jax.nn.silu(fz.astype(jnp.float32))
        gdn_o = gated_deltanet(gq, gk, gv, ga, gb, gdn_conv_w[l], gdn_A_log[l],
                               gdn_dt_bias[l], gdn_norm_g[l])
        gdn_o = gdn_o * jax.nn.silu(gz.astype(jnp.float32))
        mixed = jnp.concatenate([fox_o, gdn_o], axis=-1).astype(x.dtype)
        x = x + gate[:, None, :] * (mixed @ w_out[l])
    return rmsnorm(x, final_g)
```

```python
import functools

import jax
import jax.numpy as jnp
from jax import lax
from jax.experimental import pallas as pl
from jax.experimental.pallas import tpu as pltpu

F32 = jnp.float32
BF16 = jnp.bfloat16
HIGHEST = lax.Precision.HIGHEST

HEADS = 8
HEAD_DIM = 128
WIDTH = HEADS * HEAD_DIM
CHUNK = 64
CONV_WIDTH = 4
EPS = 1e-6
GATE_COLS = 128
COL_F, COL_A, COL_B, COL_A2, COL_A3 = 0, 8, 16, 24, 32
EXP_UNDERFLOW = 104.0
VMEM_LIMIT = 52 * 1024 * 1024

NT_DIMS = (((1,), (1,)), ((), ()))
TN_DIMS = (((0,), (0,)), ((), ()))


def _dot(a, b, precision=None):
    return jnp.dot(a, b, preferred_element_type=F32, precision=precision)


def _dot_nt(a, b, precision=None):
    return lax.dot_general(a, b, NT_DIMS, preferred_element_type=F32, precision=precision)


def _dot_tn(a, b):
    return lax.dot_general(a, b, TN_DIMS, preferred_element_type=F32)


def _silu(x):
    return x * jax.nn.sigmoid(x)


def _params(*sem):
    return pltpu.CompilerParams(dimension_semantics=sem, vmem_limit_bytes=VMEM_LIMIT)


def _ada_kernel(c_ref, w_ref, b_ref, o_ref):
    o_ref[...] = _dot(_silu(c_ref[...]), w_ref[...], HIGHEST) + b_ref[...]


def _ada(c, w, b):
    bsz, d = c.shape
    n = w.shape[1]
    rows = 8
    tn = 512 if n % 512 == 0 else 128
    cp =jnp.pad(c, ((0, rows - bsz), (0, 0)))
    out = pl.pallas_call(
        _ada_kernel,
        grid=(n // tn,),
        in_specs=[pl.BlockSpec((rows, d), lambda j: (0, 0)),
                  pl.BlockSpec((d, tn), lambda j: (0, j)),
                  pl.BlockSpec((1, tn), lambda j: (0, j))],
        out_specs=pl.BlockSpec((rows, tn), lambda j: (0, j)),
        out_shape=jax.ShapeDtypeStruct((rows, n), F32),
        compiler_params=_params("arbitrary"),
        name="ada_mod",
    )(cp, w, b.reshape(1, n))
    return out[:bsz]


def _proj_kernel(x_ref, sh_ref, sc_ref, g_ref, w_ref, ws_ref, p_ref, ps_ref, h_ref, *, tm, rows):
    @pl.when(pl.program_id(2) == 0)
    def _():
        gmul = g_ref[...] * (1.0 + sc_ref[0])
        shift = sh_ref[0]

        def body(r, carry):
            sl = pl.ds(pl.multiple_of(r * rows, rows), rows)
            xs = x_ref[0, sl, :]
            ms = jnp.mean(xs * xs, axis=-1, keepdims=True)
            h_ref[sl, :] = (xs * lax.rsqrt(ms + EPS) * gmul + shift).astype(BF16)
            return carry

        lax.fori_loop(0, tm // rows, body, 0)
        ps_ref[0] = _dot(h_ref[...], ws_ref[...])

    p_ref[0] = _dot(h_ref[...], w_ref[...]).astype(BF16)


def _proj(x, shift, scale, g, w_main, w_small):
    bsz, s, d = x.shape
    n = w_main.shape[1]
    tm = min(1024, s)
    tn = 1024
    kern = functools.partial(_proj_kernel, tm=tm, rows=min(128, tm))
    return pl.pallas_call(
        kern,
        grid=(bsz, s // tm, n // tn),
        in_specs=[pl.BlockSpec((1, tm, d), lambda b, i, j: (b, i, 0)),
                  pl.BlockSpec((1, 1, d), lambda b, i, j: (b, 0, 0)),
                  pl.BlockSpec((1, 1, d), lambda b, i, j: (b, 0, 0)),
                  pl.BlockSpec((1, d), lambda b, i, j: (0, 0)),
                  pl.BlockSpec((d, tn), lambda b, i, j: (0, j)),
                  pl.BlockSpec((d, GATE_COLS), lambda b, i, j: (0, 0))],
        out_specs=[pl.BlockSpec((1, tm, tn), lambda b, i, j: (b, i, j)),
                   pl.BlockSpec((1, tm, GATE_COLS), lambda b, i, j: (b, i, 0))],
        out_shape=[jax.ShapeDtypeStruct((bsz, s, n), BF16),
                   jax.ShapeDtypeStruct((bsz, s, GATE_COLS), F32)],
        scratch_shapes=[pltpu.VMEM((tm, d), BF16)],
        compiler_params=_params("arbitrary", "arbitrary", "arbitrary"),
        name="norm_in_proj",
    )(x, shift, scale, g, w_main, w_small)


def _gates_kernel(ps_ref, add_ref, alog_ref, ft_ref, ga_ref, gb_ref, sm_ref, carry_ref, *, tb):
    @pl.when(pl.program_id(1) == 0)
    def _():
        carry_ref[...] = jnp.zeros_like(carry_ref)

    x = ps_ref[0] + add_ref[...]
    col = lax.broadcasted_iota(jnp.int32, (tb, GATE_COLS), 1)
    is_f = col < COL_A
    is_a1 = (col >= COL_A) & (col < COL_B)
    is_b = (col >= COL_B) & (col < COL_A2)
    is_a2 = (col >= COL_A2) & (col < COL_A3)
    is_a3 = (col >= COL_A3) & (col < COL_A3 + HEADS)
    log_f = jax.nn.log_sigmoid(x)
    g = -jnp.exp(alog_ref[...]) * jax.nn.softplus(x)
    beta = jax.nn.sigmoid(x)
    vals = jnp.where(is_f, log_f, jnp.where(is_a1 | is_a2 | is_a3, g, 0.0))

    r = lax.broadcasted_iota(jnp.int32, (tb, tb), 0)
    c = lax.broadcasted_iota(jnp.int32, (tb, tb), 1)
    same_chunk = (r // CHUNK) == (c // CHUNK)
    tri_all = (r >= c).astype(F32)
    tri_chunk = ((r >= c) & same_chunk).astype(F32)
    sum_chunk = same_chunk.astype(F32)

    carry = carry_ref[0:1, :]
    cs_all = _dot(tri_all, vals, HIGHEST) + carry
    cs = _dot(tri_chunk, vals, HIGHEST)
    tot = _dot(sum_chunk, vals, HIGHEST)
    carry_ref[...] = jnp.broadcast_to(carry + jnp.sum(vals, axis=0, keepdims=True), carry_ref.shape)

    ft_ref[0] = jnp.transpose(cs_all)[0:HEADS, :]
    ga_ref[0] = jnp.where(is_a1, cs, jnp.where(is_a2, 1.0, 0.0))
    gb_ref[0] = jnp.where(is_a1, 1.0, jnp.where(is_a2, -cs, 0.0))
    sm_ref[0] = jnp.where(is_a1, jnp.exp(cs),
                          jnp.where(is_b, beta,
                                    jnp.where(is_a2, jnp.exp(tot - cs),
                                              jnp.where(is_a3, jnp.exp(tot), 0.0))))


def _gates(ps, add_row, alog_row):
    bsz, s, _ = ps.shape
    tb = min(512, s)
    small = pl.BlockSpec((1, tb, GATE_COLS), lambda b, i: (b, i, 0))
    row = pl.BlockSpec((1, GATE_COLS), lambda b, i: (0, 0))
    small_shape = jax.ShapeDtypeStruct((bsz, s, GATE_COLS), F32)
    return pl.pallas_call(
        functools.partial(_gates_kernel, tb=tb),
        grid=(bsz, s // tb),
        in_specs=[small, row, row],
        out_specs=[pl.BlockSpec((1, HEADS, tb), lambda b, i: (b, 0, i)), small, small, small],
        out_shape=[jax.ShapeDtypeStruct((bsz, HEADS, s), F32), small_shape, small_shape, small_shape],
        scratch_shapes=[pltpu.VMEM((8, GATE_COLS), F32)],
        compiler_params=_params("arbitrary", "arbitrary"),
        name="gates",
    )(ps, add_row, alog_row)


def _foxprep_kernel(q_ref, k_ref, qg_ref, kg_ref, qo_ref, ko_ref):
    def norm(src, gain, dst):
        for h in range(HEADS):
            sl = slice(h * HEAD_DIM, (h + 1) * HEAD_DIM)
            t = src[0, :, sl].astype(F32)
            ms = jnp.mean(t * t, axis=-1, keepdims=True)
            dst[0, :, sl] = (t * lax.rsqrt(ms + EPS) * gain).astype(BF16)

    norm(q_ref, qg_ref[...] * (HEAD_DIM ** -0.5), qo_ref)
    norm(k_ref, kg_ref[...], ko_ref)


def _foxprep(p, qg, kg):
    bsz, s, _ = p.shape
    tb = min(512, s)
    blk = lambda j: pl.BlockSpec((1, tb, WIDTH), lambda b, i: (b, i, j))
    row = pl.BlockSpec((1, HEAD_DIM), lambda b, i: (0, 0))
    shape = jax.ShapeDtypeStruct((bsz, s, WIDTH), BF16)
    return pl.pallas_call(
        _foxprep_kernel,
        grid=(bsz, s // tb),
        in_specs=[blk(0), blk(1), row, row],
        out_specs=[blk(0), blk(0)],
        out_shape=[shape, shape],
        compiler_params=_params("arbitrary", "arbitrary"),
        name="fox_qk_norm",
    )(p, p, qg, kg)


def _fox_kernel(lo_ref, q_ref, k_ref, v_ref, f_ref, o_ref, m_ref, l_ref, acc_ref, *, tq):
    b = pl.program_id(0)
    h = pl.program_id(1)
    i = pl.program_id(2)
    lo = lo_ref[b * HEADS + h, i]
    q = q_ref[0]
    m_ref[...] = jnp.full_like(m_ref, -jnp.inf)
    l_ref[...] = jnp.zeros_like(l_ref)
    acc_ref[...] = jnp.zeros_like(acc_ref)

    def step(j, masked):
        rows = pl.ds(pl.multiple_of(j * tq, tq), tq)
        k = k_ref[0, rows, :]
        v = v_ref[0, rows, :]
        s = _dot_nt(q, k) - f_ref[0, pl.ds(j, 1), :]
        if masked:
            r = lax.broadcasted_iota(jnp.int32, (tq, tq), 0)
            c = lax.broadcasted_iota(jnp.int32, (tq, tq), 1)
            s = jnp.where(r >= c, s, -jnp.inf)
        m_prev = m_ref[...]
        m_new = jnp.maximum(m_prev, jnp.max(s, axis=-1, keepdims=True))
        p = jnp.exp(s - m_new)
        alpha = jnp.exp(m_prev - m_new)
        l_ref[...] = alpha * l_ref[...] + jnp.sum(p, axis=-1, keepdims=True)
        acc_ref[...] = alpha * acc_ref[...] + _dot(p.astype(BF16), v)
        m_ref[...] = m_new

    def body(j, carry):
        step(j, False)
        return carry

    lax.fori_loop(lo, i, body, 0)
    step(i, True)
    o_ref[0] = (acc_ref[...] / l_ref[...]).astype(BF16)


def _fox(lo, qn, kn, p, ft):
    bsz, s, _ = qn.shape
    tq = min(512, s)
    nq = s // tq
    f3 = ft.reshape(bsz * HEADS, nq, tq)
    grid_spec = pltpu.PrefetchScalarGridSpec(
        num_scalar_prefetch=1,
        grid=(bsz, HEADS, nq),
        in_specs=[pl.BlockSpec((1, tq, HEAD_DIM), lambda b, h, i, lo_r: (b, i, h)),
                  pl.BlockSpec((1, s, HEAD_DIM), lambda b, h, i, lo_r: (b, 0, h)),
                  pl.BlockSpec((1, s, HEAD_DIM), lambda b, h, i, lo_r: (b, 0, 2 * HEADS + h)),
                  pl.BlockSpec((1, nq, tq), lambda b, h, i, lo_r: (b * HEADS + h, 0, 0))],
        out_specs=pl.BlockSpec((1, tq, HEAD_DIM), lambda b, h, i, lo_r: (b, i, h)),
        scratch_shapes=[pltpu.VMEM((tq, 1), F32), pltpu.VMEM((tq, 1), F32),
                        pltpu.VMEM((tq, HEAD_DIM), F32)],
    )
    return pl.pallas_call(
        functools.partial(_fox_kernel, tq=tq),
        grid_spec=grid_spec,
        out_shape=jax.ShapeDtypeStruct((bsz, s, WIDTH), BF16),
        compiler_params=_params("arbitrary", "arbitrary", "arbitrary"),
        name="fox_attention",
    )(lo, qn, kn, p, f3)


def _fox_block_start(ft, qg, kg, tq):
    bsz, _, s = ft.shape
    f_first = ft[:, :, 0::tq]
    f_last = ft[:, :, tq - 1::tq]
    qk_bound = 1.02 * (HEAD_DIM ** 0.5) * jnp.max(jnp.abs(qg)) * jnp.max(jnp.abs(kg))
    thresh = EXP_UNDERFLOW + 2.0 * qk_bound
    skip = f_last[:, :, None, :] > f_first[:, :, :, None] + thresh
    return jnp.sum(skip, axis=-1).astype(jnp.int32).reshape(bsz * HEADS, s // tq)


def _gdnprep_kernel(q_ref, k_ref, v_ref, qh_ref, kh_ref, vh_ref, cw_ref, sm_ref, ex_ref, sel_ref,
                    qo_ref, ko_ref, kbo_ref, vbo_ref, kbeo_ref, qeo_ref, kdo_ref, eglo_ref, ext_ref, *, tb):
    first = pl.program_id(1) == 0

    def conv_silu(src, halo, which):
        ext_ref[0:8, :] = jnp.where(first, 0.0, halo[0].astype(F32))
        ext_ref[8:tb + 8, :] = src[0].astype(F32)
        acc = jnp.zeros((tb, WIDTH), F32)
        for t in range(CONV_WIDTH):
            w_row = cw_ref[t:t + 1, which * WIDTH:(which + 1) * WIDTH]
            acc = acc + ext_ref[pl.ds(8 - (CONV_WIDTH - 1) + t, tb), :] * w_row
        return _silu(acc)

    def l2norm(t):
        parts = []
        for h in range(HEADS):
            th = t[:, h * HEAD_DIM:(h + 1) * HEAD_DIM]
            parts.append(th * lax.rsqrt(jnp.sum(th * th, axis=-1, keepdims=True) + EPS))
        return jnp.concatenate(parts, axis=-1)

    sm = sm_ref[0]
    beta_x = _dot(sm, ex_ref[0], HIGHEST)
    egc_x = _dot(sm, ex_ref[1], HIGHEST)
    edec_x = _dot(sm, ex_ref[2], HIGHEST)
    eglo_ref[0] = _dot(_dot(sel_ref[...], sm, HIGHEST), ex_ref[3], HIGHEST)

    k = l2norm(conv_silu(k_ref, kh_ref, 1))
    kb = k * beta_x
    ko_ref[0] = k.astype(BF16)
    kbo_ref[0] = kb.astype(BF16)
    kbeo_ref[0] = (kb * egc_x).astype(BF16)
    kdo_ref[0] = (k * edec_x).astype(BF16)
    q = l2norm(conv_silu(q_ref, qh_ref, 0)) * (HEAD_DIM ** -0.5)
    qo_ref[0] = q.astype(BF16)
    qeo_ref[0] = (q * egc_x).astype(BF16)
    v = conv_silu(v_ref, vh_ref, 2)
    vbo_ref[0] = (v * beta_x).astype(BF16)


def _gdnprep(p, conv_w, sm, expand, sel):
    bsz, s, _ = p.shape
    tb = min(512, s)
    nchunk = tb // CHUNK
    blk = lambda j: pl.BlockSpec((1, tb, WIDTH), lambda b, i: (b, i, j))
    halo = lambda j: pl.BlockSpec((1, 8, WIDTH), lambda b, i: (b, jnp.maximum(i * (tb // 8) - 1, 0), j))
    shape = jax.ShapeDtypeStruct((bsz, s, WIDTH), BF16)
    return pl.pallas_call(
        functools.partial(_gdnprep_kernel, tb=tb),
        grid=(bsz, s // tb),
        in_specs=[blk(4), blk(5), blk(6), halo(4), halo(5), halo(6),
                  pl.BlockSpec((CONV_WIDTH, 3 * WIDTH), lambda b, i: (0, 0)),
                  pl.BlockSpec((1, tb, GATE_COLS), lambda b, i: (b, i, 0)),
                  pl.BlockSpec((4, GATE_COLS, WIDTH), lambda b, i: (0, 0, 0)),
                  pl.BlockSpec((nchunk, tb), lambda b, i: (0, 0))],
        out_specs=[blk(0)] * 7 + [pl.BlockSpec((1, nchunk, WIDTH), lambda b, i: (b, i, 0))],
        out_shape=[shape] * 7 + [jax.ShapeDtypeStruct((bsz, s // CHUNK, WIDTH), F32)],
        scratch_shapes=[pltpu.VMEM((tb + 8, WIDTH), F32)],
        compiler_params=_params("arbitrary", "arbitrary"),
        name="gdn_prep",
    )(p, p, p, p, p, p, conv_w, sm, expand, sel)


def _gdn_kernel(q_ref, k_ref, kb_ref, vb_ref, kbe_ref, qe_ref, kd_ref, ga_ref, gb_ref, egl_ref,
                o_ref, s_ref, *, nchunk):
    @pl.when(pl.program_id(1) == 0)
    def _():
        s_ref[...] = jnp.zeros_like(s_ref)

    r = lax.broadcasted_iota(jnp.int32, (CHUNK, CHUNK), 0)
    c = lax.broadcasted_iota(jnp.int32, (CHUNK, CHUNK), 1)
    lower = r >= c
    strict = r > c
    eye = (r == c).astype(F32)
    diag8 = (r // 8) == (c // 8)
    levels = [((r // (2 * w)) == (c // (2 * w))) & ((r // w) != (c // w)) for w in (8, 16, 32)]
    gcol = lax.broadcasted_iota(jnp.int32, (CHUNK, GATE_COLS), 1)

    def chunk_body(ci, carry):
        rows = pl.ds(pl.multiple_of(ci * CHUNK, CHUNK), CHUNK)
        ga = ga_ref[0, rows, :]
        gb = gb_ref[0, rows, :]
        egl = egl_ref[0, pl.ds(ci, 1), :]
        for h in range(HEADS):
            hs = slice(h * HEAD_DIM, (h + 1) * HEAD_DIM)
            gbh = jnp.where((gcol == COL_A + h) | (gcol == COL_A2 + h), gb, 0.0)
            dlog = _dot_nt(ga, gbh, HIGHEST)
            decay = jnp.exp(jnp.where(lower, dlog, -jnp.inf))
            k = k_ref[0, rows, hs]
            m = jnp.where(strict, _dot_nt(kb_ref[0, rows, hs], k) * decay, 0.0)
            attn = _dot_nt(q_ref[0, rows, hs], k) * decay

            md = jnp.where(diag8, m, 0.0).astype(BF16)
            m2 = _dot(md, md)
            m4 = _dot(m2.astype(BF16), m2.astype(BF16))
            inv = eye - md.astype(F32)
            inv = inv + _dot(inv.astype(BF16), m2.astype(BF16))
            inv = inv + _dot(inv.astype(BF16), m4.astype(BF16))
            for lvl in levels:
                off = jnp.where(lvl, m, 0.0).astype(BF16)
                invb = inv.astype(BF16)
                inv = inv - _dot(_dot(invb, off).astype(BF16), invb)

            rhs = jnp.concatenate([vb_ref[0, rows, hs], kbe_ref[0, rows, hs]], axis=-1)
            uw = _dot(inv.astype(BF16), rhs)
            u = uw[:, :HEAD_DIM]
            w = uw[:, HEAD_DIM:]
            state = s_ref[h]
            ws = _dot(jnp.concatenate([w.astype(BF16), qe_ref[0, rows, hs]], axis=0), state.astype(BF16))
            v_new = u - ws[:CHUNK]
            v_new_b = v_new.astype(BF16)
            o_ref[0, rows, hs] = (ws[CHUNK:] + _dot(attn.astype(BF16), v_new_b)).astype(BF16)
            s_ref[h] = state * egl[:, hs] + _dot_tn(kd_ref[0, rows, hs], v_new_b)
        return carry

    lax.fori_loop(0, nchunk, chunk_body, 0)


def _gdn(q, k, kb, vb, kbe, qe, kd, ga, gb, egl):
    bsz, s, _ = q.shape
    tb = min(512, s)
    nchunk = tb // CHUNK
    blk = pl.BlockSpec((1, tb, WIDTH), lambda b, i: (b, i, 0))
    small = pl.BlockSpec((1, tb, GATE_COLS), lambda b, i: (b, i, 0))
    return pl.pallas_call(
        functools.partial(_gdn_kernel, nchunk=nchunk),
        grid=(bsz, s // tb),
        in_specs=[blk] * 7 + [small, small, pl.BlockSpec((1, nchunk, WIDTH), lambda b, i: (b, i, 0))],
        out_specs=blk,
        out_shape=jax.ShapeDtypeStruct((bsz, s, WIDTH), BF16),
        scratch_shapes=[pltpu.VMEM((HEADS, HEAD_DIM, HEAD_DIM), F32)],
        compiler_params=_params("arbitrary", "arbitrary"),
        name="gdn_delta_rule",
    )(q, k, kb, vb, kbe, qe, kd, ga, gb, egl)


def _out_kernel(x_ref, fo_ref, fz_ref, go_ref, gz_ref, gate_ref, gng_ref, w_ref, fg_ref, o_ref, *, final_norm):
    a = fo_ref[0].astype(F32) * _silu(fz_ref[0].astype(F32))
    parts = []
    for h in range(HEADS):
        sl = slice(h * HEAD_DIM, (h + 1) * HEAD_DIM)
        t = go_ref[0, :, sl].astype(F32)
        ms = jnp.mean(t * t, axis=-1, keepdims=True)
        parts.append(t * lax.rsqrt(ms + EPS) * gng_ref[...])
    g = jnp.concatenate(parts, axis=-1) * _silu(gz_ref[0].astype(F32))
    y = _dot(a.astype(BF16), w_ref[0:WIDTH, :]) + _dot(g.astype(BF16), w_ref[WIDTH:2 * WIDTH, :])
    xn = x_ref[0] + gate_ref[0] * y
    if final_norm:
        ms = jnp.mean(xn * xn, axis=-1, keepdims=True)
        xn = xn * lax.rsqrt(ms + EPS) * fg_ref[...]
    o_ref[0] = xn


def _out(x, fox_o, p, gdn_o, gate, gng, w_out, final_g, final_norm):
    bsz, s, d = x.shape
    tm = min(512, s)
    blk = lambda j: pl.BlockSpec((1, tm, WIDTH), lambda b, i: (b, i, j))
    return pl.pallas_call(
        functools.partial(_out_kernel, final_norm=final_norm),
        grid=(bsz, s // tm),
        in_specs=[pl.BlockSpec((1, tm, d), lambda b, i: (b, i, 0)),
                  blk(0), blk(3), blk(0), blk(7),
                  pl.BlockSpec((1, 1, d), lambda b, i: (b, 0, 0)),
                  pl.BlockSpec((1, HEAD_DIM), lambda b, i: (0, 0)),
                  pl.BlockSpec((2 * WIDTH, d), lambda b, i: (0, 0)),
                  pl.BlockSpec((1, d), lambda b, i: (0, 0))],
        out_specs=pl.BlockSpec((1, tm, d), lambda b, i: (b, i, 0)),
        out_shape=jax.ShapeDtypeStruct((bsz, s, d), F32),
        compiler_params=_params("arbitrary", "arbitrary"),
        name="gate_out_proj",
    )(x, fox_o, p, gdn_o, p, gate, gng, w_out, final_g)


def _expand_matrices():
    lane_head = jnp.arange(WIDTH) // HEAD_DIM
    rows = jnp.arange(GATE_COLS)[:, None]
    mats = [(rows == grp + lane_head[None, :]).astype(F32) for grp in (COL_B, COL_A, COL_A2, COL_A3)]
    return jnp.stack(mats)


def _chunk_last_selector(tb):
    nchunk = tb // CHUNK
    return (jnp.arange(tb)[None, :] == (jnp.arange(nchunk)[:, None] * CHUNK + CHUNK - 1)).astype(F32)


def _split_w_in(w):
    fw = WIDTH
    o_ff = 4 * fw
    o_g = o_ff + HEADS
    o_ga = o_g + 4 * fw
    o_gb = o_ga + HEADS
    w_main = jnp.concatenate([w[:, :o_ff], w[:, o_g:o_ga]], axis=1).astype(BF16)
    ff, ga, gb = w[:, o_ff:o_g], w[:, o_ga:o_gb], w[:, o_gb:o_gb + HEADS]
    pad = jnp.zeros((w.shape[0], GATE_COLS - 5 * HEADS), w.dtype)
    w_small = jnp.concatenate([ff, ga, gb, ga, ga, pad], axis=1).astype(BF16)
    return w_main, w_small


def _gate_rows(b_f, dt_bias, a_log):
    z = jnp.zeros((HEADS,), F32)
    pad = jnp.zeros((GATE_COLS - 5 * HEADS,), F32)
    add_row = jnp.concatenate([b_f, dt_bias, z, dt_bias, dt_bias, pad]).reshape(1, GATE_COLS)
    alog_row = jnp.concatenate([z, a_log, z, a_log, a_log, pad]).reshape(1, GATE_COLS)
    return add_row.astype(F32), alog_row.astype(F32)


def kernel(x, c, norm_g, w_ada, b_ada, w_in, b_fgate, fox_qn_g, fox_kn_g, gdn_conv_w, gdn_A_log,
           gdn_dt_bias, gdn_norm_g, w_out, final_g):
    bsz, s, d = x.shape
    depth = w_in.shape[0]
    expand = _expand_matrices()
    sel = _chunk_last_selector(min(512, s))
    tq = min(512, s)
    for l in range(depth):
        mod = _ada(c, w_ada[l], b_ada[l])
        shift, scale, gate = (mod[:, k * d:(k + 1) * d].reshape(bsz, 1, d) for k in range(3))
        w_main, w_small = _split_w_in(w_in[l])
        p, ps = _proj(x, shift, scale, norm_g[l].reshape(1, d), w_main, w_small)

        add_row, alog_row = _gate_rows(b_fgate[l], gdn_dt_bias[l], gdn_A_log[l])
        ft, ga, gb, sm = _gates(ps, add_row, alog_row)

        qg = fox_qn_g[l].reshape(1, HEAD_DIM)
        kg = fox_kn_g[l].reshape(1, HEAD_DIM)
        qn, kn = _foxprep(p, qg, kg)
        lo = _fox_block_start(ft, qg, kg, tq)
        fox_o = _fox(lo, qn, kn, p, ft)

        gq, gk, gkb, gvb, gkbe, gqe, gkd, egl = _gdnprep(p, gdn_conv_w[l], sm, expand, sel)
        gdn_o = _gdn(gq, gk, gkb, gvb, gkbe, gqe, gkd, ga, gb, egl)

        x = _out(x, fox_o, p, gdn_o, gate, gdn_norm_g[l].reshape(1, HEAD_DIM), w_out[l].astype(BF16),
                 final_g.reshape(1, d), final_norm=(l == depth - 1))
    return x
```

```python
import functools

import jax
import jax.numpy as jnp
from jax import lax
from jax.experimental import pallas as pl
from jax.experimental.pallas import tpu as pltpu

F32 = jnp.float32
BF16 = jnp.bfloat16
HIGHEST = lax.Precision.HIGHEST

HEADS = 8
HEAD_DIM = 128
WIDTH = HEADS * HEAD_DIM
CHUNK = 64
CONV_WIDTH = 4
EPS = 1e-6
GATE_COLS = 128
COL_F, COL_A, COL_B, COL_A2, COL_A3, COL_F2, COL_F3, COL_A4 = 0, 8, 16, 24, 32, 40, 48, 56
GATE_GROUPS = "fabaaffa"
LOG2E = 1.4426950408889634
SOLVE_UNROLL = 2
EXP_UNDERFLOW = 104.0
VMEM_LIMIT = 52 * 1024 * 1024

NT_DIMS = (((1,), (1,)), ((), ()))
TN_DIMS = (((0,), (0,)), ((), ()))


def _dot(a, b, precision=None):
    return jnp.dot(a, b, preferred_element_type=F32, precision=precision)


def _dot_nt(a, b, precision=None):
    return lax.dot_general(a, b, NT_DIMS, preferred_element_type=F32, precision=precision)


def _dot_tn(a, b):
    return lax.dot_general(a, b, TN_DIMS, preferred_element_type=F32)


def _silu(x):
    return x * jax.nn.sigmoid(x)


def _params(*sem):
    return pltpu.CompilerParams(dimension_semantics=sem, vmem_limit_bytes=VMEM_LIMIT)


def _ada_kernel(c_ref, w_ref, b_ref, o_ref):
    o_ref[...] = _dot(_silu(c_ref[...]), w_ref[...], HIGHEST) + b_ref[...]


def _ada(c, w, b):
    bsz, d = c.shape
    n = w.shape[1]
    rows = 8
    tn = 512 if n % 512 == 0 else 128
    cp =jnp.pad(c, ((0, rows - bsz), (0, 0)))
    out = pl.pallas_call(
        _ada_kernel,
        grid=(n // tn,),
        in_specs=[pl.BlockSpec((rows, d), lambda j: (0, 0)),
                  pl.BlockSpec((d, tn), lambda j: (0, j)),
                  pl.BlockSpec((1, tn), lambda j: (0, j))],
        out_specs=pl.BlockSpec((rows, tn), lambda j: (0, j)),
        out_shape=jax.ShapeDtypeStruct((rows, n), F32),
        compiler_params=_params("arbitrary"),
        name="ada_mod",
    )(cp, w, b.reshape(1, n))
    return out[:bsz]


def _proj_kernel(x_ref, sh_ref, sc_ref, g_ref, w_ref, ws_ref, p_ref, ps_ref, h_ref, *, tm, rows):
    @pl.when(pl.program_id(2) == 0)
    def _():
        gmul = g_ref[...] * (1.0 + sc_ref[0])
        shift = sh_ref[0]

        def body(r, carry):
            sl = pl.ds(pl.multiple_of(r * rows, rows), rows)
            xs = x_ref[0, sl, :]
            ms = jnp.mean(xs * xs, axis=-1, keepdims=True)
            h_ref[sl, :] = (xs * lax.rsqrt(ms + EPS) * gmul + shift).astype(BF16)
            return carry

        lax.fori_loop(0, tm // rows, body, 0)
        ps_ref[0] = _dot(h_ref[...], ws_ref[...])

    p_ref[0] = _dot(h_ref[...], w_ref[...]).astype(BF16)


def _proj(x, shift, scale, g, w_main, w_small):
    bsz, s, d = x.shape
    n = w_main.shape[1]
    tm = min(1024, s)
    tn = 1024
    kern = functools.partial(_proj_kernel, tm=tm, rows=min(128, tm))
    return pl.pallas_call(
        kern,
        grid=(bsz, s // tm, n // tn),
        in_specs=[pl.BlockSpec((1, tm, d), lambda b, i, j: (b, i, 0)),
                  pl.BlockSpec((1, 1, d), lambda b, i, j: (b, 0, 0)),
                  pl.BlockSpec((1, 1, d), lambda b, i, j: (b, 0, 0)),
                  pl.BlockSpec((1, d), lambda b, i, j: (0, 0)),
                  pl.BlockSpec((d, tn), lambda b, i, j: (0, j)),
                  pl.BlockSpec((d, GATE_COLS), lambda b, i, j: (0, 0))],
        out_specs=[pl.BlockSpec((1, tm, tn), lambda b, i, j: (b, i, j)),
                   pl.BlockSpec((1, tm, GATE_COLS), lambda b, i, j: (b, i, 0))],
        out_shape=[jax.ShapeDtypeStruct((bsz, s, n), BF16),
                   jax.ShapeDtypeStruct((bsz, s, GATE_COLS), F32)],
        scratch_shapes=[pltpu.VMEM((tm, d), BF16)],
        compiler_params=_params("arbitrary", "arbitrary", "arbitrary"),
        name="norm_in_proj",
    )(x, shift, scale, g, w_main, w_small)


def _gates_kernel(ps_ref, add_ref, alog_ref, ft_ref, fx_ref, ga_ref, gb_ref, sm_ref, carry_ref, *, tb):
    @pl.when(pl.program_id(1) == 0)
    def _():
        carry_ref[...] = jnp.zeros_like(carry_ref)

    x = ps_ref[0] + add_ref[...]
    col = lax.broadcasted_iota(jnp.int32, (tb, GATE_COLS), 1)
    grp = lambda start: (col >= start) & (col < start + HEADS)
    is_f = grp(COL_F) | grp(COL_F2) | grp(COL_F3)
    is_a = grp(COL_A) | grp(COL_A2) | grp(COL_A3) | grp(COL_A4)
    log_f = jax.nn.log_sigmoid(x)
    g = -jnp.exp(alog_ref[...]) * jax.nn.softplus(x)
    beta = jax.nn.sigmoid(x)
    vals = jnp.where(is_f, log_f, jnp.where(is_a, g, 0.0))

    r = lax.broadcasted_iota(jnp.int32, (tb, tb), 0)
    c = lax.broadcasted_iota(jnp.int32, (tb, tb), 1)
    same_chunk = (r // CHUNK) == (c // CHUNK)
    tri_all = (r >= c).astype(F32)
    tri_chunk = ((r >= c) & same_chunk).astype(F32)
    sum_chunk = same_chunk.astype(F32)

    carry = carry_ref[0:1, :]
    cs_all = _dot(tri_all, vals, HIGHEST) + carry
    cs = _dot(tri_chunk, vals, HIGHEST)
    tot = _dot(sum_chunk, vals, HIGHEST)
    carry_ref[...] = jnp.broadcast_to(carry + jnp.sum(vals, axis=0, keepdims=True), carry_ref.shape)

    def split3(t):
        hi = t.astype(BF16).astype(F32)
        mid = (t - hi).astype(BF16).astype(F32)
        return hi, mid, t - hi - mid

    ft_ref[0] = jnp.transpose(cs_all)[0:HEADS, :]
    f_hi, f_mid, f_lo = split3(cs_all * (-LOG2E))
    fx_ref[0] = jnp.where(grp(COL_F), f_hi, jnp.where(grp(COL_F2), f_mid,
                                                      jnp.where(grp(COL_F3), f_lo, 0.0))).astype(BF16)
    c_hi, c_lo, _ = split3(cs)
    ones = grp(COL_A3) | grp(COL_A4)
    ga_ref[0] = jnp.where(grp(COL_A), c_hi, jnp.where(grp(COL_A2), c_lo, jnp.where(ones, 1.0, 0.0))).astype(BF16)
    gb_ref[0] = jnp.where(grp(COL_A) | grp(COL_A2), 1.0,
                          jnp.where(grp(COL_A3), -c_hi, jnp.where(grp(COL_A4), -c_lo, 0.0))).astype(BF16)
    sm_ref[0] = jnp.where(grp(COL_A), jnp.exp(cs),
                          jnp.where(grp(COL_B), beta,
                                    jnp.where(grp(COL_A2), jnp.exp(tot - cs),
                                              jnp.where(grp(COL_A3), jnp.exp(tot), 0.0))))


def _gates(ps, add_row, alog_row):
    bsz, s, _ = ps.shape
    tb = min(512, s)
    small = pl.BlockSpec((1, tb, GATE_COLS), lambda b, i: (b, i, 0))
    row = pl.BlockSpec((1, GATE_COLS), lambda b, i: (0, 0))
    small_shape = jax.ShapeDtypeStruct((bsz, s, GATE_COLS), F32)
    bf16_shape = jax.ShapeDtypeStruct((bsz, s, GATE_COLS), BF16)
    return pl.pallas_call(
        functools.partial(_gates_kernel, tb=tb),
        grid=(bsz, s // tb),
        in_specs=[small, row, row],
        out_specs=[pl.BlockSpec((1, HEADS, tb), lambda b, i: (b, 0, i)), small, small, small, small],
        out_shape=[jax.ShapeDtypeStruct((bsz, HEADS, s), F32), bf16_shape, bf16_shape, bf16_shape, small_shape],
        scratch_shapes=[pltpu.VMEM((8, GATE_COLS), F32)],
        compiler_params=_params("arbitrary", "arbitrary"),
        name="gates",
    )(ps, add_row, alog_row)


def _foxprep_kernel(q_ref, k_ref, qg_ref, kg_ref, qo_ref, ko_ref):
    def norm(src, gain, dst):
        for h in range(HEADS):
            sl = slice(h * HEAD_DIM, (h + 1) * HEAD_DIM)
            t = src[0, :, sl].astype(F32)
            ms = jnp.mean(t * t, axis=-1, keepdims=True)
            dst[0, :, sl] = (t * lax.rsqrt(ms + EPS) * gain).astype(BF16)

    norm(q_ref, qg_ref[...] * (HEAD_DIM ** -0.5 * LOG2E), qo_ref)
    norm(k_ref, kg_ref[...], ko_ref)


def _foxprep(p, qg, kg):
    bsz, s, _ = p.shape
    tb = min(512, s)
    blk = lambda j: pl.BlockSpec((1, tb, WIDTH), lambda b, i: (b, i, j))
    row = pl.BlockSpec((1, HEAD_DIM), lambda b, i: (0, 0))
    shape = jax.ShapeDtypeStruct((bsz, s, WIDTH), BF16)
    return pl.pallas_call(
        _foxprep_kernel,
        grid=(bsz, s // tb),
        in_specs=[blk(0), blk(1), row, row],
        out_specs=[blk(0), blk(0)],
        out_shape=[shape, shape],
        compiler_params=_params("arbitrary", "arbitrary"),
        name="fox_qk_norm",
    )(p, p, qg, kg)


def _fox_kernel(lo_ref, q_ref, k_ref, v_ref, fx_ref, o_ref, acc_ref, *, tq):
    b = pl.program_id(0)
    h = pl.program_id(1)
    i = pl.program_id(2)
    lo = lo_ref[b * HEADS + h, i]
    lane = lax.broadcasted_iota(jnp.int32, (tq, GATE_COLS), 1)
    pick = (lane == COL_F + h) | (lane == COL_F2 + h) | (lane == COL_F3 + h)
    qa = jnp.concatenate([q_ref[0], jnp.where(pick, 1.0, 0.0).astype(BF16)], axis=1)
    acc_ref[...] = jnp.zeros_like(acc_ref)

    def key_rows(j):
        return pl.ds(pl.multiple_of(j * tq, tq), tq)

    def scores(j):
        ka = jnp.concatenate([k_ref[0, key_rows(j), :], fx_ref[0, key_rows(j), :]], axis=1)
        return _dot_nt(ka, qa)

    def update(js, sts, m, l):
        m_new = m
        for st in sts:
            m_new = jnp.maximum(m_new, jnp.max(st, axis=0, keepdims=True))
        alpha = jnp.exp2(m - m_new)
        l_new = alpha * l
        pv = None
        for j, st in zip(js, sts):
            p = jnp.exp2(st - m_new)
            l_new = l_new + jnp.sum(p, axis=0, keepdims=True)
            t = _dot_tn(v_ref[0, key_rows(j), :], p.astype(BF16))
            pv = t if pv is None else pv + t
        acc_ref[...] = alpha * acc_ref[...] + pv
        return m_new, l_new

    def causal(st):
        r = lax.broadcasted_iota(jnp.int32, (tq, tq), 0)
        c = lax.broadcasted_iota(jnp.int32, (tq, tq), 1)
        return jnp.where(r <= c, st, -jnp.inf)

    def pair_body(t, carry):
        j = lo + 2 * t
        return update((j, j + 1), (scores(j), scores(j + 1)), *carry)

    n = i - lo
    init = (jnp.full((1, tq), -jnp.inf, F32), jnp.zeros((1, tq), F32))
    m, l = lax.fori_loop(0, n // 2, pair_body, init)
    m, l = lax.cond(n % 2 == 1,
                    lambda m, l: update((i - 1, i), (scores(i - 1), causal(scores(i))), m, l),
                    lambda m, l: update((i,), (causal(scores(i)),), m, l),
                    m, l)
    o_ref[0] = jnp.transpose(acc_ref[...] / l).astype(BF16)


def _fox(lo, qn, kn, p, fx):
    bsz, s, _ = qn.shape
    tq = min(512, s)
    nq = s // tq
    grid_spec = pltpu.PrefetchScalarGridSpec(
        num_scalar_prefetch=1,
        grid=(bsz, HEADS, nq),
        in_specs=[pl.BlockSpec((1, tq, HEAD_DIM), lambda b, h, i, lo_r: (b, i, h)),
                  pl.BlockSpec((1, s, HEAD_DIM), lambda b, h, i, lo_r: (b, 0, h)),
                  pl.BlockSpec((1, s, HEAD_DIM), lambda b, h, i, lo_r: (b, 0, 2 * HEADS + h)),
                  pl.BlockSpec((1, s, GATE_COLS), lambda b, h, i, lo_r: (b, 0, 0))],
        out_specs=pl.BlockSpec((1, tq, HEAD_DIM), lambda b, h, i, lo_r: (b, i, h)),
        scratch_shapes=[pltpu.VMEM((HEAD_DIM, tq), F32)],
    )
    return pl.pallas_call(
        functools.partial(_fox_kernel, tq=tq),
        grid_spec=grid_spec,
        out_shape=jax.ShapeDtypeStruct((bsz, s, WIDTH), BF16),
        compiler_params=_params("arbitrary", "arbitrary", "arbitrary"),
        name="fox_attention",
    )(lo, qn, kn, p, fx)


def _fox_block_start(ft, qg, kg, tq):
    bsz, _, s = ft.shape
    f_first = ft[:, :, 0::tq]
    f_last = ft[:, :, tq - 1::tq]
    qk_bound = 1.02 * (HEAD_DIM ** 0.5) * jnp.max(jnp.abs(qg)) * jnp.max(jnp.abs(kg))
    thresh = EXP_UNDERFLOW + 2.0 * qk_bound
    skip = f_last[:, :, None, :] > f_first[:, :, :, None] + thresh
    return jnp.sum(skip, axis=-1).astype(jnp.int32).reshape(bsz * HEADS, s // tq)


def _gdnprep_kernel(q_ref, k_ref, v_ref, qh_ref, kh_ref, vh_ref, cw_ref, sm_ref, ex_ref, sel_ref,
                    qo_ref, ko_ref, kbo_ref, vbo_ref, kbeo_ref, qeo_ref, kdo_ref, eglo_ref, ext_ref, *, tb):
    first = pl.program_id(1) == 0

    def conv_silu(src, halo, which):
        ext_ref[0:8, :] = jnp.where(first, 0.0, halo[0].astype(F32))
        ext_ref[8:tb + 8, :] = src[0].astype(F32)
        acc = jnp.zeros((tb, WIDTH), F32)
        for t in range(CONV_WIDTH):
            w_row = cw_ref[t:t + 1, which * WIDTH:(which + 1) * WIDTH]
            acc = acc + ext_ref[pl.ds(8 - (CONV_WIDTH - 1) + t, tb), :] * w_row
        return _silu(acc)

    def l2norm(t):
        parts = []
        for h in range(HEADS):
            th = t[:, h * HEAD_DIM:(h + 1) * HEAD_DIM]
            parts.append(th * lax.rsqrt(jnp.sum(th * th, axis=-1, keepdims=True) + EPS))
        return jnp.concatenate(parts, axis=-1)

    sm = sm_ref[0]
    beta_x = _dot(sm, ex_ref[0], HIGHEST)
    egc_x = _dot(sm, ex_ref[1], HIGHEST)
    edec_x = _dot(sm, ex_ref[2], HIGHEST)
    eglo_ref[0] = _dot(_dot(sel_ref[...], sm, HIGHEST), ex_ref[3], HIGHEST)

    k = l2norm(conv_silu(k_ref, kh_ref, 1))
    kb = k * beta_x
    ko_ref[0] = k.astype(BF16)
    kbo_ref[0] = kb.astype(BF16)
    kbeo_ref[0] = (kb * egc_x).astype(BF16)
    kdo_ref[0] = (k * edec_x).astype(BF16)
    q = l2norm(conv_silu(q_ref, qh_ref, 0)) * (HEAD_DIM ** -0.5)
    qo_ref[0] = q.astype(BF16)
    qeo_ref[0] = (q * egc_x).astype(BF16)
    v = conv_silu(v_ref, vh_ref, 2)
    vbo_ref[0] = (v * beta_x).astype(BF16)


def _gdnprep(p, conv_w, sm, expand, sel):
    bsz, s, _ = p.shape
    tb = min(512, s)
    nchunk = tb // CHUNK
    blk = lambda j: pl.BlockSpec((1, tb, WIDTH), lambda b, i: (b, i, j))
    halo = lambda j: pl.BlockSpec((1, 8, WIDTH), lambda b, i: (b, jnp.maximum(i * (tb // 8) - 1, 0), j))
    shape = jax.ShapeDtypeStruct((bsz, s, WIDTH), BF16)
    return pl.pallas_call(
        functools.partial(_gdnprep_kernel, tb=tb),
        grid=(bsz, s // tb),
        in_specs=[blk(4), blk(5), blk(6), halo(4), halo(5), halo(6),
                  pl.BlockSpec((CONV_WIDTH, 3 * WIDTH), lambda b, i: (0, 0)),
                  pl.BlockSpec((1, tb, GATE_COLS), lambda b, i: (b, i, 0)),
                  pl.BlockSpec((4, GATE_COLS, WIDTH), lambda b, i: (0, 0, 0)),
                  pl.BlockSpec((nchunk, tb), lambda b, i: (0, 0))],
        out_specs=[blk(0)] * 7 + [pl.BlockSpec((1, nchunk, WIDTH), lambda b, i: (b, i, 0))],
        out_shape=[shape] * 7 + [jax.ShapeDtypeStruct((bsz, s // CHUNK, WIDTH), F32)],
        scratch_shapes=[pltpu.VMEM((tb + 8, WIDTH), F32)],
        compiler_params=_params("arbitrary", "arbitrary"),
        name="gdn_prep",
    )(p, p, p, p, p, p, conv_w, sm, expand, sel)


def _gdn_kernel(q_ref, k_ref, kb_ref, vb_ref, kbe_ref, qe_ref, kd_ref, ga_ref, gb_ref, egl_ref,
                o_ref, s_ref, u_ref, w_ref, attn_ref, *, nchunk):
    @pl.when(pl.program_id(1) == 0)
    def _():
        s_ref[...] = jnp.zeros_like(s_ref)

    r = lax.broadcasted_iota(jnp.int32, (CHUNK, CHUNK), 0)
    c = lax.broadcasted_iota(jnp.int32, (CHUNK, CHUNK), 1)
    lower = r >= c
    strict = r > c
    eye = (r == c).astype(F32)
    diag8 = (r // 8) == (c // 8)
    levels = [((r // (2 * w)) == (c // (2 * w))) & ((r // w) != (c // w)) for w in (8, 16, 32)]
    gcol = lax.broadcasted_iota(jnp.int32, (CHUNK, GATE_COLS), 1)
    heads = range(HEADS)
    hsl = [slice(h * HEAD_DIM, (h + 1) * HEAD_DIM) for h in heads]
    head_cols = [(gcol == COL_A + h) | (gcol == COL_A2 + h) | (gcol == COL_A3 + h) | (gcol == COL_A4 + h)
                 for h in heads]

    def solve_chunks(ci, carry):
        probs = [(t, h) for t in range(SOLVE_UNROLL) for h in heads]
        rows = [pl.ds(pl.multiple_of((ci * SOLVE_UNROLL + t) * CHUNK, CHUNK), CHUNK) for t in range(SOLVE_UNROLL)]
        ga = [ga_ref[0, rows[t], :] for t in range(SOLVE_UNROLL)]
        gb = [gb_ref[0, rows[t], :] for t in range(SOLVE_UNROLL)]
        gbh = [jnp.where(head_cols[h], gb[t], jnp.zeros((), BF16)) for t, h in probs]
        dlog = [_dot_nt(ga[t], gbh[p]) for p, (t, h) in enumerate(probs)]
        kk = [_dot_nt(kb_ref[0, rows[t], hsl[h]], k_ref[0, rows[t], hsl[h]]) for t, h in probs]
        qk = [_dot_nt(q_ref[0, rows[t], hsl[h]], k_ref[0, rows[t], hsl[h]]) for t, h in probs]
        n = range(len(probs))
        decay = [jnp.exp(jnp.where(lower, dlog[p], -jnp.inf)) for p in n]
        m = [jnp.where(strict, kk[p] * decay[p], 0.0) for p in n]
        for p, (t, h) in enumerate(probs):
            attn_ref[rows[t], h * HEAD_DIM:h * HEAD_DIM + CHUNK] = (qk[p] * decay[p]).astype(BF16)

        md = [jnp.where(diag8, m[p], 0.0).astype(BF16) for p in n]
        m2 = [_dot(md[p], md[p]).astype(BF16) for p in n]
        inv = [eye - md[p].astype(F32) for p in n]
        m4 = [_dot(m2[p], m2[p]).astype(BF16) for p in n]
        inv = [inv[p] + _dot(inv[p].astype(BF16), m2[p]) for p in n]
        inv = [inv[p] + _dot(inv[p].astype(BF16), m4[p]) for p in n]
        for lvl in levels:
            off = [jnp.where(lvl, m[p], 0.0).astype(BF16) for p in n]
            invb = [inv[p].astype(BF16) for p in n]
            x = [_dot(invb[p], off[p]).astype(BF16) for p in n]
            inv = [inv[p] - _dot(x[p], invb[p]) for p in n]

        uw = [_dot(inv[p].astype(BF16),
                   jnp.concatenate([vb_ref[0, rows[t], hsl[h]], kbe_ref[0, rows[t], hsl[h]]], axis=-1))
              for p, (t, h) in enumerate(probs)]
        for p, (t, h) in enumerate(probs):
            u_ref[rows[t], hsl[h]] = uw[p][:, :HEAD_DIM]
            w_ref[rows[t], hsl[h]] = uw[p][:, HEAD_DIM:].astype(BF16)
        return carry

    def scan_chunk(ci, carry):
        rows = pl.ds(pl.multiple_of(ci * CHUNK, CHUNK), CHUNK)
        egl = egl_ref[0, pl.ds(ci, 1), :]
        state = [s_ref[h] for h in heads]
        ws = [_dot(jnp.concatenate([w_ref[rows, hsl[h]], qe_ref[0, rows, hsl[h]]], axis=0),
                   state[h].astype(BF16)) for h in heads]
        v_new = [(u_ref[rows, hsl[h]] - ws[h][:CHUNK]).astype(BF16) for h in heads]
        for h in heads:
            intra = _dot(attn_ref[rows, h * HEAD_DIM:h * HEAD_DIM + CHUNK], v_new[h])
            o_ref[0, rows, hsl[h]] = (ws[h][CHUNK:] + intra).astype(BF16)
        for h in heads:
            s_ref[h] = state[h] * egl[:, hsl[h]] + _dot_tn(kd_ref[0, rows, hsl[h]], v_new[h])
        return carry

    lax.fori_loop(0, nchunk // SOLVE_UNROLL, solve_chunks, 0)
    lax.fori_loop(0, nchunk, scan_chunk, 0)


def _gdn(q, k, kb, vb, kbe, qe, kd, ga, gb, egl):
    bsz, s, _ = q.shape
    tb = min(512, s)
    nchunk = tb // CHUNK
    blk = pl.BlockSpec((1, tb, WIDTH), lambda b, i: (b, i, 0))
    small = pl.BlockSpec((1, tb, GATE_COLS), lambda b, i: (b, i, 0))
    return pl.pallas_call(
        functools.partial(_gdn_kernel, nchunk=nchunk),
        grid=(bsz, s // tb),
        in_specs=[blk] * 7 + [small, small, pl.BlockSpec((1, nchunk, WIDTH), lambda b, i: (b, i, 0))],
        out_specs=blk,
        out_shape=jax.ShapeDtypeStruct((bsz, s, WIDTH), BF16),
        scratch_shapes=[pltpu.VMEM((HEADS, HEAD_DIM, HEAD_DIM), F32),
                        pltpu.VMEM((tb, WIDTH), F32), pltpu.VMEM((tb, WIDTH), BF16),
                        pltpu.VMEM((tb, WIDTH), BF16)],
        compiler_params=_params("arbitrary", "arbitrary"),
        name="gdn_delta_rule",
    )(q, k, kb, vb, kbe, qe, kd, ga, gb, egl)


def _out_kernel(x_ref, fo_ref, fz_ref, go_ref, gz_ref, gate_ref, gng_ref, w_ref, fg_ref, o_ref, *, final_norm):
    a = fo_ref[0].astype(F32) * _silu(fz_ref[0].astype(F32))
    parts = []
    for h in range(HEADS):
        sl = slice(h * HEAD_DIM, (h + 1) * HEAD_DIM)
        t = go_ref[0, :, sl].astype(F32)
        ms = jnp.mean(t * t, axis=-1, keepdims=True)
        parts.append(t * lax.rsqrt(ms + EPS) * gng_ref[...])
    g = jnp.concatenate(parts, axis=-1) * _silu(gz_ref[0].astype(F32))
    y = _dot(a.astype(BF16), w_ref[0:WIDTH, :]) + _dot(g.astype(BF16), w_ref[WIDTH:2 * WIDTH, :])
    xn = x_ref[0] + gate_ref[0] * y
    if final_norm:
        ms = jnp.mean(xn * xn, axis=-1, keepdims=True)
        xn = xn * lax.rsqrt(ms + EPS) * fg_ref[...]
    o_ref[0] = xn


def _out(x, fox_o, p, gdn_o, gate, gng, w_out, final_g, final_norm):
    bsz, s, d = x.shape
    tm = min(512, s)
    blk = lambda j: pl.BlockSpec((1, tm, WIDTH), lambda b, i: (b, i, j))
    return pl.pallas_call(
        functools.partial(_out_kernel, final_norm=final_norm),
        grid=(bsz, s // tm),
        in_specs=[pl.BlockSpec((1, tm, d), lambda b, i: (b, i, 0)),
                  blk(0), blk(3), blk(0), blk(7),
                  pl.BlockSpec((1, 1, d), lambda b, i: (b, 0, 0)),
                  pl.BlockSpec((1, HEAD_DIM), lambda b, i: (0, 0)),
                  pl.BlockSpec((2 * WIDTH, d), lambda b, i: (0, 0)),
                  pl.BlockSpec((1, d), lambda b, i: (0, 0))],
        out_specs=pl.BlockSpec((1, tm, d), lambda b, i: (b, i, 0)),
        out_shape=jax.ShapeDtypeStruct((bsz, s, d), F32),
        compiler_params=_params("arbitrary", "arbitrary"),
        name="gate_out_proj",
    )(x, fox_o, p, gdn_o, p, gate, gng, w_out, final_g)


def _expand_matrices():
    lane_head = jnp.arange(WIDTH) // HEAD_DIM
    rows = jnp.arange(GATE_COLS)[:, None]
    mats = [(rows == grp + lane_head[None, :]).astype(F32) for grp in (COL_B, COL_A, COL_A2, COL_A3)]
    return jnp.stack(mats)


def _chunk_last_selector(tb):
    nchunk = tb // CHUNK
    return (jnp.arange(tb)[None, :] == (jnp.arange(nchunk)[:, None] * CHUNK + CHUNK - 1)).astype(F32)


def _split_w_in(w):
    fw = WIDTH
    o_ff = 4 * fw
    o_g = o_ff + HEADS
    o_ga = o_g + 4 * fw
    o_gb = o_ga + HEADS
    w_main = jnp.concatenate([w[:, :o_ff], w[:, o_g:o_ga]], axis=1).astype(BF16)
    cols = {"f": w[:, o_ff:o_g], "a": w[:, o_ga:o_gb], "b": w[:, o_gb:o_gb + HEADS]}
    pad = jnp.zeros((w.shape[0], GATE_COLS - len(GATE_GROUPS) * HEADS), w.dtype)
    w_small = jnp.concatenate([cols[t] for t in GATE_GROUPS] + [pad], axis=1).astype(BF16)
    return w_main, w_small


def _gate_rows(b_f, dt_bias, a_log):
    z = jnp.zeros((HEADS,), F32)
    pad = jnp.zeros((GATE_COLS - len(GATE_GROUPS) * HEADS,), F32)
    add = {"f": b_f, "a": dt_bias, "b": z}
    alog = {"f": z, "a": a_log, "b": z}
    add_row = jnp.concatenate([add[t] for t in GATE_GROUPS] + [pad]).reshape(1, GATE_COLS)
    alog_row = jnp.concatenate([alog[t] for t in GATE_GROUPS] + [pad]).reshape(1, GATE_COLS)
    return add_row.astype(F32), alog_row.astype(F32)


def kernel(x, c, norm_g, w_ada, b_ada, w_in, b_fgate, fox_qn_g, fox_kn_g, gdn_conv_w, gdn_A_log,
           gdn_dt_bias, gdn_norm_g, w_out, final_g):
    bsz, s, d = x.shape
    depth = w_in.shape[0]
    expand = _expand_matrices()
    sel = _chunk_last_selector(min(512, s))
    tq = min(512, s)
    for l in range(depth):
        mod = _ada(c, w_ada[l], b_ada[l])
        shift, scale, gate = (mod[:, k * d:(k + 1) * d].reshape(bsz, 1, d) for k in range(3))
        w_main, w_small = _split_w_in(w_in[l])
        p, ps = _proj(x, shift, scale, norm_g[l].reshape(1, d), w_main, w_small)

        add_row, alog_row = _gate_rows(b_fgate[l], gdn_dt_bias[l], gdn_A_log[l])
        ft, fx, ga, gb, sm = _gates(ps, add_row, alog_row)

        qg = fox_qn_g[l].reshape(1, HEAD_DIM)
        kg = fox_kn_g[l].reshape(1, HEAD_DIM)
        qn, kn = _foxprep(p, qg, kg)
        lo = _fox_block_start(ft, qg, kg, tq)
        fox_o = _fox(lo, qn, kn, p, fx)

        gq, gk, gkb, gvb, gkbe, gqe, gkd, egl = _gdnprep(p, gdn_conv_w[l], sm, expand, sel)
        gdn_o = _gdn(gq, gk, gkb, gvb, gkbe, gqe, gkd, ga, gb, egl)

        x = _out(x, fox_o, p, gdn_o, gate, gdn_norm_g[l].reshape(1, HEAD_DIM), w_out[l].astype(BF16),
                 final_g.reshape(1, d), final_norm=(l == depth - 1))
    return x
```

```python
import functools

import jax
import jax.numpy as jnp
from jax import lax
from jax.experimental import pallas as pl
from jax.experimental.pallas import tpu as pltpu

F32 = jnp.float32
BF16 = jnp.bfloat16
HIGHEST = lax.Precision.HIGHEST

HEADS = 8
HEAD_DIM = 128
WIDTH = HEADS * HEAD_DIM
CHUNK = 64
CONV_WIDTH = 4
EPS = 1e-6
GATE_COLS = 128
COL_F, COL_A, COL_B, COL_A2, COL_A3, COL_F2, COL_F3, COL_A4 = 0, 8, 16, 24, 32, 40, 48, 56
GATE_GROUPS = "fabaaffa"
LOG2E = 1.4426950408889634
SOLVE_UNROLL = 2
EXP_UNDERFLOW = 104.0
VMEM_LIMIT = 52 * 1024 * 1024

NT_DIMS = (((1,), (1,)), ((), ()))
TN_DIMS = (((0,), (0,)), ((), ()))


def _dot(a, b, precision=None):
    return jnp.dot(a, b, preferred_element_type=F32, precision=precision)


def _dot_nt(a, b, precision=None):
    return lax.dot_general(a, b, NT_DIMS, preferred_element_type=F32, precision=precision)


def _dot_tn(a, b):
    return lax.dot_general(a, b, TN_DIMS, preferred_element_type=F32)


def _silu(x):
    return x * jax.nn.sigmoid(x)


def _params(*sem):
    return pltpu.CompilerParams(dimension_semantics=sem, vmem_limit_bytes=VMEM_LIMIT)


def _ada_kernel(c_ref, w_ref, b_ref, o_ref):
    o_ref[...] = _dot(_silu(c_ref[...]), w_ref[...], HIGHEST) + b_ref[...]


def _ada(c, w, b):
    bsz, d = c.shape
    n = w.shape[1]
    rows = 8
    tn = 512 if n % 512 == 0 else 128
    cp =jnp.pad(c, ((0, rows - bsz), (0, 0)))
    out = pl.pallas_call(
        _ada_kernel,
        grid=(n // tn,),
        in_specs=[pl.BlockSpec((rows, d), lambda j: (0, 0)),
                  pl.BlockSpec((d, tn), lambda j: (0, j)),
                  pl.BlockSpec((1, tn), lambda j: (0, j))],
        out_specs=pl.BlockSpec((rows, tn), lambda j: (0, j)),
        out_shape=jax.ShapeDtypeStruct((rows, n), F32),
        compiler_params=_params("arbitrary"),
        name="ada_mod",
    )(cp, w, b.reshape(1, n))
    return out[:bsz]


def _proj_kernel(x_ref, sh_ref, sc_ref, g_ref, w_ref, ws_ref, p_ref, ps_ref, h_ref, *, tm, rows):
    @pl.when(pl.program_id(2) == 0)
    def _():
        gmul = g_ref[...] * (1.0 + sc_ref[0])
        shift = sh_ref[0]

        def body(r, carry):
            sl = pl.ds(pl.multiple_of(r * rows, rows), rows)
            xs = x_ref[0, sl, :]
            ms = jnp.mean(xs * xs, axis=-1, keepdims=True)
            h_ref[sl, :] = (xs * lax.rsqrt(ms + EPS) * gmul + shift).astype(BF16)
            return carry

        lax.fori_loop(0, tm // rows, body, 0)
        ps_ref[0] = _dot(h_ref[...], ws_ref[...])

    p_ref[0] = _dot(h_ref[...], w_ref[...]).astype(BF16)


def _proj(x, shift, scale, g, w_main, w_small):
    bsz, s, d = x.shape
    n = w_main.shape[1]
    tm = min(1024, s)
    tn = 1024
    kern = functools.partial(_proj_kernel, tm=tm, rows=min(128, tm))
    return pl.pallas_call(
        kern,
        grid=(bsz, s // tm, n // tn),
        in_specs=[pl.BlockSpec((1, tm, d), lambda b, i, j: (b, i, 0)),
                  pl.BlockSpec((1, 1, d), lambda b, i, j: (b, 0, 0)),
                  pl.BlockSpec((1, 1, d), lambda b, i, j: (b, 0, 0)),
                  pl.BlockSpec((1, d), lambda b, i, j: (0, 0)),
                  pl.BlockSpec((d, tn), lambda b, i, j: (0, j)),
                  pl.BlockSpec((d, GATE_COLS), lambda b, i, j: (0, 0))],
        out_specs=[pl.BlockSpec((1, tm, tn), lambda b, i, j: (b, i, j)),
                   pl.BlockSpec((1, tm, GATE_COLS), lambda b, i, j: (b, i, 0))],
        out_shape=[jax.ShapeDtypeStruct((bsz, s, n), BF16),
                   jax.ShapeDtypeStruct((bsz, s, GATE_COLS), F32)],
        scratch_shapes=[pltpu.VMEM((tm, d), BF16)],
        compiler_params=_params("arbitrary", "arbitrary", "arbitrary"),
        name="norm_in_proj",
    )(x, shift, scale, g, w_main, w_small)


def _gates_kernel(ps_ref, add_ref, alog_ref, ft_ref, fx_ref, ga_ref, gb_ref, sm_ref, carry_ref, *, tb):
    @pl.when(pl.program_id(1) == 0)
    def _():
        carry_ref[...] = jnp.zeros_like(carry_ref)

    x = ps_ref[0] + add_ref[...]
    col = lax.broadcasted_iota(jnp.int32, (tb, GATE_COLS), 1)
    grp = lambda start: (col >= start) & (col < start + HEADS)
    is_f = grp(COL_F) | grp(COL_F2) | grp(COL_F3)
    is_a = grp(COL_A) | grp(COL_A2) | grp(COL_A3) | grp(COL_A4)
    log_f = jax.nn.log_sigmoid(x)
    g = -jnp.exp(alog_ref[...]) * jax.nn.softplus(x)
    beta = jax.nn.sigmoid(x)
    vals = jnp.where(is_f, log_f, jnp.where(is_a, g, 0.0))

    r = lax.broadcasted_iota(jnp.int32, (tb, tb), 0)
    c = lax.broadcasted_iota(jnp.int32, (tb, tb), 1)
    same_chunk = (r // CHUNK) == (c // CHUNK)
    sums = jnp.concatenate([(r >= c).astype(F32),
                            ((r >= c) & same_chunk).astype(F32),
                            same_chunk.astype(F32)], axis=0).astype(BF16)

    def split3(t):
        hi = t.astype(BF16).astype(F32)
        mid = (t - hi).astype(BF16).astype(F32)
        return hi, mid, t - hi - mid

    pieces = _dot(sums, jnp.concatenate(split3(vals), axis=1).astype(BF16))
    summed = pieces[:, :GATE_COLS] + pieces[:, GATE_COLS:2 * GATE_COLS] + pieces[:, 2 * GATE_COLS:]
    carry = carry_ref[0:1, :]
    cs_all = summed[0:tb] + carry
    cs = summed[tb:2 * tb]
    tot = summed[2 * tb:]
    carry_ref[...] = jnp.broadcast_to(carry + jnp.sum(vals, axis=0, keepdims=True), carry_ref.shape)

    ft_ref[0] = jnp.transpose(cs_all)[0:HEADS, :]
    f_hi, f_mid, f_lo = split3(cs_all * (-LOG2E))
    fx_ref[0] = jnp.where(grp(COL_F), f_hi, jnp.where(grp(COL_F2), f_mid,
                                                      jnp.where(grp(COL_F3), f_lo, 0.0))).astype(BF16)
    c_hi, c_lo, _ = split3(cs)
    ones = grp(COL_A3) | grp(COL_A4)
    ga_ref[0] = jnp.where(grp(COL_A), c_hi, jnp.where(grp(COL_A2), c_lo, jnp.where(ones, 1.0, 0.0))).astype(BF16)
    gb_ref[0] = jnp.where(grp(COL_A) | grp(COL_A2), 1.0,
                          jnp.where(grp(COL_A3), -c_hi, jnp.where(grp(COL_A4), -c_lo, 0.0))).astype(BF16)
    sm_ref[0] = jnp.where(grp(COL_A), jnp.exp(cs),
                          jnp.where(grp(COL_B), beta,
                                    jnp.where(grp(COL_A2), jnp.exp(tot - cs),
                                              jnp.where(grp(COL_A3), jnp.exp(tot), 0.0))))


def _gates(ps, add_row, alog_row):
    bsz, s, _ = ps.shape
    tb = min(256, s)
    small = pl.BlockSpec((1, tb, GATE_COLS), lambda b, i: (b, i, 0))
    row = pl.BlockSpec((1, GATE_COLS), lambda b, i: (0, 0))
    small_shape = jax.ShapeDtypeStruct((bsz, s, GATE_COLS), F32)
    bf16_shape = jax.ShapeDtypeStruct((bsz, s, GATE_COLS), BF16)
    return pl.pallas_call(
        functools.partial(_gates_kernel, tb=tb),
        grid=(bsz, s // tb),
        in_specs=[small, row, row],
        out_specs=[pl.BlockSpec((1, HEADS, tb), lambda b, i: (b, 0, i)), small, small, small, small],
        out_shape=[jax.ShapeDtypeStruct((bsz, HEADS, s), F32), bf16_shape, bf16_shape, bf16_shape, small_shape],
        scratch_shapes=[pltpu.VMEM((8, GATE_COLS), F32)],
        compiler_params=_params("arbitrary", "arbitrary"),
        name="gates",
    )(ps, add_row, alog_row)


def _foxprep_kernel(q_ref, k_ref, qg_ref, kg_ref, qo_ref, ko_ref):
    def norm(src, gain, dst):
        for h in range(HEADS):
            sl = slice(h * HEAD_DIM, (h + 1) * HEAD_DIM)
            t = src[0, :, sl].astype(F32)
            ms = jnp.mean(t * t, axis=-1, keepdims=True)
            dst[0, :, sl] = (t * lax.rsqrt(ms + EPS) * gain).astype(BF16)

    norm(q_ref, qg_ref[...] * (HEAD_DIM ** -0.5 * LOG2E), qo_ref)
    norm(k_ref, kg_ref[...], ko_ref)


def _foxprep(p, qg, kg):
    bsz, s, _ = p.shape
    tb = min(512, s)
    blk = lambda j: pl.BlockSpec((1, tb, WIDTH), lambda b, i: (b, i, j))
    row = pl.BlockSpec((1, HEAD_DIM), lambda b, i: (0, 0))
    shape = jax.ShapeDtypeStruct((bsz, s, WIDTH), BF16)
    return pl.pallas_call(
        _foxprep_kernel,
        grid=(bsz, s // tb),
        in_specs=[blk(0), blk(1), row, row],
        out_specs=[blk(0), blk(0)],
        out_shape=[shape, shape],
        compiler_params=_params("arbitrary", "arbitrary"),
        name="fox_qk_norm",
    )(p, p, qg, kg)


def _fox_kernel(lo_ref, q_ref, k_ref, v_ref, fx_ref, o_ref, acc_ref, st_ref, *, tq):
    b = pl.program_id(0)
    h = pl.program_id(1)
    i = pl.program_id(2)
    lo = lo_ref[b * HEADS + h, i]
    lane = lax.broadcasted_iota(jnp.int32, (tq, GATE_COLS), 1)
    pick = (lane == COL_F + h) | (lane == COL_F2 + h) | (lane == COL_F3 + h)
    qa = jnp.concatenate([q_ref[0], jnp.where(pick, 1.0, 0.0).astype(BF16)], axis=1)
    acc_ref[...] = jnp.zeros_like(acc_ref)

    def key_rows(j):
        return pl.ds(pl.multiple_of(j * tq, tq), tq)

    def scores(j):
        ka = jnp.concatenate([k_ref[0, key_rows(j), :], fx_ref[0, key_rows(j), :]], axis=1)
        return _dot_nt(ka, qa)

    def causal(st):
        r = lax.broadcasted_iota(jnp.int32, (tq, tq), 0)
        c = lax.broadcasted_iota(jnp.int32, (tq, tq), 1)
        return jnp.where(r <= c, st, -jnp.inf)

    def step(cur_j, cur_slot, carry, nxt_j=None, nxt_causal=False):
        if nxt_j is not None:
            nxt = scores(nxt_j)
            st_ref[1 - cur_slot] = causal(nxt) if nxt_causal else nxt
        m, l = carry
        st = st_ref[cur_slot]
        m_new = jnp.maximum(m, jnp.max(st, axis=0, keepdims=True))
        alpha = jnp.exp2(m - m_new)
        p = jnp.exp2(st - m_new)
        l_new = alpha * l + jnp.sum(p, axis=0, keepdims=True)
        pv = _dot_tn(v_ref[0, key_rows(cur_j), :], p.astype(BF16))
        acc_ref[...] = alpha * acc_ref[...] + pv
        return m_new, l_new

    n = i - lo

    @pl.when(n == 0)
    def _():
        st_ref[0] = causal(scores(i))

    @pl.when(n > 0)
    def _():
        st_ref[0] = scores(lo)

    def two_interior(t, carry):
        j = lo + 2 * t
        carry = step(j, 0, carry, nxt_j=j + 1)
        return step(j + 1, 1, carry, nxt_j=j + 2)

    init = (jnp.full((1, tq), -jnp.inf, F32), jnp.zeros((1, tq), F32))
    pairs = jnp.maximum(n - 1, 0) // 2
    carry = lax.fori_loop(0, pairs, two_interior, init)

    def finish_diag_only(carry):
        return step(i, 0, carry)

    def finish_one_interior(carry):
        carry = step(i - 1, 0, carry, nxt_j=i, nxt_causal=True)
        return step(i, 1, carry)

    def finish_two_interior(carry):
        carry = step(i - 2, 0, carry, nxt_j=i - 1)
        carry = step(i - 1, 1, carry, nxt_j=i, nxt_causal=True)
        return step(i, 0, carry)

    left = n - 2 * pairs
    m, l = lax.switch(left, (finish_diag_only, finish_one_interior, finish_two_interior), carry)
    o_ref[0] = jnp.transpose(acc_ref[...] / l).astype(BF16)


def _fox(lo, qn, kn, p, fx):
    bsz, s, _ = qn.shape
    tq = min(512, s)
    nq = s // tq
    grid_spec = pltpu.PrefetchScalarGridSpec(
        num_scalar_prefetch=1,
        grid=(bsz, HEADS, nq),
        in_specs=[pl.BlockSpec((1, tq, HEAD_DIM), lambda b, h, i, lo_r: (b, i, h)),
                  pl.BlockSpec((1, s, HEAD_DIM), lambda b, h, i, lo_r: (b, 0, h)),
                  pl.BlockSpec((1, s, HEAD_DIM), lambda b, h, i, lo_r: (b, 0, 2 * HEADS + h)),
                  pl.BlockSpec((1, s, GATE_COLS), lambda b, h, i, lo_r: (b, 0, 0))],
        out_specs=pl.BlockSpec((1, tq, HEAD_DIM), lambda b, h, i, lo_r: (b, i, h)),
        scratch_shapes=[pltpu.VMEM((HEAD_DIM, tq), F32), pltpu.VMEM((2, tq, tq), F32)],
    )
    return pl.pallas_call(
        functools.partial(_fox_kernel, tq=tq),
        grid_spec=grid_spec,
        out_shape=jax.ShapeDtypeStruct((bsz, s, WIDTH), BF16),
        compiler_params=_params("arbitrary", "arbitrary", "arbitrary"),
        name="fox_attention",
    )(lo, qn, kn, p, fx)


def _fox_block_start(ft, qg, kg, tq):
    bsz, _, s = ft.shape
    f_first = ft[:, :, 0::tq]
    f_last = ft[:, :, tq - 1::tq]
    qk_bound = 1.02 * (HEAD_DIM ** 0.5) * jnp.max(jnp.abs(qg)) * jnp.max(jnp.abs(kg))
    thresh = EXP_UNDERFLOW + 2.0 * qk_bound
    skip = f_last[:, :, None, :] > f_first[:, :, :, None] + thresh
    return jnp.sum(skip, axis=-1).astype(jnp.int32).reshape(bsz * HEADS, s // tq)


def _gdnprep_kernel(q_ref, k_ref, v_ref, qh_ref, kh_ref, vh_ref, cw_ref, sm_ref, ex_ref, sel_ref,
                    qo_ref, ko_ref, kbo_ref, vbo_ref, kbeo_ref, qeo_ref, kdo_ref, eglo_ref, *, tb):
    first = pl.program_id(1) == 0

    def conv_silu(src, halo, which):
        ext = jnp.concatenate([jnp.where(first, 0.0, halo[0].astype(F32)), src[0].astype(F32)], axis=0)
        acc = jnp.zeros((tb, WIDTH), F32)
        for t in range(CONV_WIDTH):
            w_row = cw_ref[t:t + 1, which * WIDTH:(which + 1) * WIDTH]
            back = CONV_WIDTH - 1 - t
            shifted = pltpu.roll(ext, back, axis=0) if back else ext
            acc = acc + shifted[8:, :] * w_row
        return _silu(acc)

    def spread(group):
        e = ex_ref[group].astype(BF16)
        return _dot(sm2, jnp.concatenate([e, e], axis=0))

    def l2norm(t):
        parts = []
        for h in range(HEADS):
            th = t[:, h * HEAD_DIM:(h + 1) * HEAD_DIM]
            parts.append(th * lax.rsqrt(jnp.sum(th * th, axis=-1, keepdims=True) + EPS))
        return jnp.concatenate(parts, axis=-1)

    sm = sm_ref[0]
    sm_hi = sm.astype(BF16)
    sm2 = jnp.concatenate([sm_hi, (sm - sm_hi.astype(F32)).astype(BF16)], axis=1)
    beta_x = spread(0)
    egc_x = spread(1)
    edec_x = spread(2)
    eglo_ref[0] = _dot(_dot(sel_ref[...], sm, HIGHEST), ex_ref[3], HIGHEST)

    k = l2norm(conv_silu(k_ref, kh_ref, 1))
    kb = k * beta_x
    ko_ref[0] = k.astype(BF16)
    kbo_ref[0] = kb.astype(BF16)
    kbeo_ref[0] = (kb * egc_x).astype(BF16)
    kdo_ref[0] = (k * edec_x).astype(BF16)
    q = l2norm(conv_silu(q_ref, qh_ref, 0)) * (HEAD_DIM ** -0.5)
    qo_ref[0] = q.astype(BF16)
    qeo_ref[0] = (q * egc_x).astype(BF16)
    v = conv_silu(v_ref, vh_ref, 2)
    vbo_ref[0] = (v * beta_x).astype(BF16)


def _gdnprep(p, conv_w, sm, expand, sel):
    bsz, s, _ = p.shape
    tb = min(512, s)
    nchunk = tb // CHUNK
    blk = lambda j: pl.BlockSpec((1, tb, WIDTH), lambda b, i: (b, i, j))
    halo = lambda j: pl.BlockSpec((1, 8, WIDTH), lambda b, i: (b, jnp.maximum(i * (tb // 8) - 1, 0), j))
    shape = jax.ShapeDtypeStruct((bsz, s, WIDTH), BF16)
    return pl.pallas_call(
        functools.partial(_gdnprep_kernel, tb=tb),
        grid=(bsz, s // tb),
        in_specs=[blk(4), blk(5), blk(6), halo(4), halo(5), halo(6),
                  pl.BlockSpec((CONV_WIDTH, 3 * WIDTH), lambda b, i: (0, 0)),
                  pl.BlockSpec((1, tb, GATE_COLS), lambda b, i: (b, i, 0)),
                  pl.BlockSpec((4, GATE_COLS, WIDTH), lambda b, i: (0, 0, 0)),
                  pl.BlockSpec((nchunk, tb), lambda b, i: (0, 0))],
        out_specs=[blk(0)] * 7 + [pl.BlockSpec((1, nchunk, WIDTH), lambda b, i: (b, i, 0))],
        out_shape=[shape] * 7 + [jax.ShapeDtypeStruct((bsz, s // CHUNK, WIDTH), F32)],
        compiler_params=_params("arbitrary", "arbitrary"),
        name="gdn_prep",
    )(p, p, p, p, p, p, conv_w, sm, expand, sel)


def _gdn_kernel(q_ref, k_ref, kb_ref, vb_ref, kbe_ref, qe_ref, kd_ref, ga_ref, gb_ref, egl_ref,
                o_ref, s_ref, u_ref, w_ref, attn_ref, *, nchunk):
    @pl.when(pl.program_id(1) == 0)
    def _():
        s_ref[...] = jnp.zeros_like(s_ref)

    r = lax.broadcasted_iota(jnp.int32, (CHUNK, CHUNK), 0)
    c = lax.broadcasted_iota(jnp.int32, (CHUNK, CHUNK), 1)
    lower = r >= c
    strict = r > c
    eye = (r == c).astype(F32)
    diag8 = (r // 8) == (c // 8)
    levels = [((r // (2 * w)) == (c // (2 * w))) & ((r // w) != (c // w)) for w in (8, 16, 32)]
    gcol = lax.broadcasted_iota(jnp.int32, (CHUNK, GATE_COLS), 1)
    heads = range(HEADS)
    hsl = [slice(h * HEAD_DIM, (h + 1) * HEAD_DIM) for h in heads]
    head_cols = [(gcol == COL_A + h) | (gcol == COL_A2 + h) | (gcol == COL_A3 + h) | (gcol == COL_A4 + h)
                 for h in heads]

    def solve_chunks(ci, carry):
        probs = [(t, h) for t in range(SOLVE_UNROLL) for h in heads]
        rows = [pl.ds(pl.multiple_of((ci * SOLVE_UNROLL + t) * CHUNK, CHUNK), CHUNK) for t in range(SOLVE_UNROLL)]
        ga = [ga_ref[0, rows[t], :] for t in range(SOLVE_UNROLL)]
        gb = [gb_ref[0, rows[t], :] for t in range(SOLVE_UNROLL)]
        gbh = [jnp.where(head_cols[h], gb[t], jnp.zeros((), BF16)) for t, h in probs]
        dlog = [_dot_nt(ga[t], gbh[p]) for p, (t, h) in enumerate(probs)]
        kk = [_dot_nt(kb_ref[0, rows[t], hsl[h]], k_ref[0, rows[t], hsl[h]]) for t, h in probs]
        qk = [_dot_nt(q_ref[0, rows[t], hsl[h]], k_ref[0, rows[t], hsl[h]]) for t, h in probs]
        n = range(len(probs))
        decay = [jnp.exp(jnp.where(lower, dlog[p], -jnp.inf)) for p in n]
        m = [jnp.where(strict, kk[p] * decay[p], 0.0) for p in n]
        for p, (t, h) in enumerate(probs):
            attn_ref[rows[t], h * HEAD_DIM:h * HEAD_DIM + CHUNK] = (qk[p] * decay[p]).astype(BF16)

        md = [jnp.where(diag8, m[p], 0.0).astype(BF16) for p in n]
        m2 = [_dot(md[p], md[p]).astype(BF16) for p in n]
        inv = [eye - md[p].astype(F32) for p in n]
        m4 = [_dot(m2[p], m2[p]).astype(BF16) for p in n]
        inv = [inv[p] + _dot(inv[p].astype(BF16), m2[p]) for p in n]
        inv = [inv[p] + _dot(inv[p].astype(BF16), m4[p]) for p in n]
        for lvl in levels:
            off = [jnp.where(lvl, m[p], 0.0).astype(BF16) for p in n]
            invb = [inv[p].astype(BF16) for p in n]
            x = [_dot(invb[p], off[p]).astype(BF16) for p in n]
            inv = [inv[p] - _dot(x[p], invb[p]) for p in n]

        uw = [_dot(inv[p].astype(BF16),
                   jnp.concatenate([vb_ref[0, rows[t], hsl[h]], kbe_ref[0, rows[t], hsl[h]]], axis=-1))
              for p, (t, h) in enumerate(probs)]
        for p, (t, h) in enumerate(probs):
            u_ref[rows[t], hsl[h]] = uw[p][:, :HEAD_DIM]
            w_ref[rows[t], hsl[h]] = uw[p][:, HEAD_DIM:].astype(BF16)
        return carry

    def scan_chunk(ci, carry):
        rows = pl.ds(pl.multiple_of(ci * CHUNK, CHUNK), CHUNK)
        egl = egl_ref[0, pl.ds(ci, 1), :]
        state = [s_ref[h] for h in heads]
        ws = [_dot(jnp.concatenate([w_ref[rows, hsl[h]], qe_ref[0, rows, hsl[h]]], axis=0),
                   state[h].astype(BF16)) for h in heads]
        v_new = [(u_ref[rows, hsl[h]] - ws[h][:CHUNK]).astype(BF16) for h in heads]
        for h in heads:
            intra = _dot(attn_ref[rows, h * HEAD_DIM:h * HEAD_DIM + CHUNK], v_new[h])
            o_ref[0, rows, hsl[h]] = (ws[h][CHUNK:] + intra).astype(BF16)
        for h in heads:
            s_ref[h] = state[h] * egl[:, hsl[h]] + _dot_tn(kd_ref[0, rows, hsl[h]], v_new[h])
        return carry

    lax.fori_loop(0, nchunk // SOLVE_UNROLL, solve_chunks, 0)
    lax.fori_loop(0, nchunk, scan_chunk, 0)


def _gdn(q, k, kb, vb, kbe, qe, kd, ga, gb, egl):
    bsz, s, _ = q.shape
    tb = min(512, s)
    nchunk = tb // CHUNK
    blk = pl.BlockSpec((1, tb, WIDTH), lambda b, i: (b, i, 0))
    small = pl.BlockSpec((1, tb, GATE_COLS), lambda b, i: (b, i, 0))
    return pl.pallas_call(
        functools.partial(_gdn_kernel, nchunk=nchunk),
        grid=(bsz, s // tb),
        in_specs=[blk] * 7 + [small, small, pl.BlockSpec((1, nchunk, WIDTH), lambda b, i: (b, i, 0))],
        out_specs=blk,
        out_shape=jax.ShapeDtypeStruct((bsz, s, WIDTH), BF16),
        scratch_shapes=[pltpu.VMEM((HEADS, HEAD_DIM, HEAD_DIM), F32),
                        pltpu.VMEM((tb, WIDTH), F32), pltpu.VMEM((tb, WIDTH), BF16),
                        pltpu.VMEM((tb, WIDTH), BF16)],
        compiler_params=_params("arbitrary", "arbitrary"),
        name="gdn_delta_rule",
    )(q, k, kb, vb, kbe, qe, kd, ga, gb, egl)


def _out_kernel(x_ref, fo_ref, fz_ref, go_ref, gz_ref, gate_ref, gng_ref, w_ref, fg_ref, o_ref, *, final_norm):
    a = fo_ref[0].astype(F32) * _silu(fz_ref[0].astype(F32))
    parts = []
    for h in range(HEADS):
        sl = slice(h * HEAD_DIM, (h + 1) * HEAD_DIM)
        t = go_ref[0, :, sl].astype(F32)
        ms = jnp.mean(t * t, axis=-1, keepdims=True)
        parts.append(t * lax.rsqrt(ms + EPS) * gng_ref[...])
    g = jnp.concatenate(parts, axis=-1) * _silu(gz_ref[0].astype(F32))
    y = _dot(a.astype(BF16), w_ref[0:WIDTH, :]) + _dot(g.astype(BF16), w_ref[WIDTH:2 * WIDTH, :])
    xn = x_ref[0] + gate_ref[0] * y
    if final_norm:
        ms = jnp.mean(xn * xn, axis=-1, keepdims=True)
        xn = xn * lax.rsqrt(ms + EPS) * fg_ref[...]
    o_ref[0] = xn


def _out(x, fox_o, p, gdn_o, gate, gng, w_out, final_g, final_norm):
    bsz, s, d = x.shape
    tm = min(512, s)
    blk = lambda j: pl.BlockSpec((1, tm, WIDTH), lambda b, i: (b, i, j))
    return pl.pallas_call(
        functools.partial(_out_kernel, final_norm=final_norm),
        grid=(bsz, s // tm),
        in_specs=[pl.BlockSpec((1, tm, d), lambda b, i: (b, i, 0)),
                  blk(0), blk(3), blk(0), blk(7),
                  pl.BlockSpec((1, 1, d), lambda b, i: (b, 0, 0)),
                  pl.BlockSpec((1, HEAD_DIM), lambda b, i: (0, 0)),
                  pl.BlockSpec((2 * WIDTH, d), lambda b, i: (0, 0)),
                  pl.BlockSpec((1, d), lambda b, i: (0, 0))],
        out_specs=pl.BlockSpec((1, tm, d), lambda b, i: (b, i, 0)),
        out_shape=jax.ShapeDtypeStruct((bsz, s, d), F32),
        compiler_params=_params("arbitrary", "arbitrary"),
        name="gate_out_proj",
    )(x, fox_o, p, gdn_o, p, gate, gng, w_out, final_g)


def _expand_matrices():
    lane_head = jnp.arange(WIDTH) // HEAD_DIM
    rows = jnp.arange(GATE_COLS)[:, None]
    mats = [(rows == grp + lane_head[None, :]).astype(F32) for grp in (COL_B, COL_A, COL_A2, COL_A3)]
    return jnp.stack(mats)


def _chunk_last_selector(tb):
    nchunk = tb // CHUNK
    return (jnp.arange(tb)[None, :] == (jnp.arange(nchunk)[:, None] * CHUNK + CHUNK - 1)).astype(F32)


def _split_w_in(w):
    fw = WIDTH
    o_ff = 4 * fw
    o_g = o_ff + HEADS
    o_ga = o_g + 4 * fw
    o_gb = o_ga + HEADS
    w_main = jnp.concatenate([w[:, :o_ff], w[:, o_g:o_ga]], axis=1).astype(BF16)
    cols = {"f": w[:, o_ff:o_g], "a": w[:, o_ga:o_gb], "b": w[:, o_gb:o_gb + HEADS]}
    pad = jnp.zeros((w.shape[0], GATE_COLS - len(GATE_GROUPS) * HEADS), w.dtype)
    w_small = jnp.concatenate([cols[t] for t in GATE_GROUPS] + [pad], axis=1).astype(BF16)
    return w_main, w_small


def _gate_rows(b_f, dt_bias, a_log):
    z = jnp.zeros((HEADS,), F32)
    pad = jnp.zeros((GATE_COLS - len(GATE_GROUPS) * HEADS,), F32)
    add = {"f": b_f, "a": dt_bias, "b": z}
    alog = {"f": z, "a": a_log, "b": z}
    add_row = jnp.concatenate([add[t] for t in GATE_GROUPS] + [pad]).reshape(1, GATE_COLS)
    alog_row = jnp.concatenate([alog[t] for t in GATE_GROUPS] + [pad]).reshape(1, GATE_COLS)
    return add_row.astype(F32), alog_row.astype(F32)


def kernel(x, c, norm_g, w_ada, b_ada, w_in, b_fgate, fox_qn_g, fox_kn_g, gdn_conv_w, gdn_A_log,
           gdn_dt_bias, gdn_norm_g, w_out, final_g):
    bsz, s, d = x.shape
    depth = w_in.shape[0]
    expand = _expand_matrices()
    sel = _chunk_last_selector(min(512, s))
    tq = min(512, s)
    for l in range(depth):
        mod = _ada(c, w_ada[l], b_ada[l])
        shift, scale, gate = (mod[:, k * d:(k + 1) * d].reshape(bsz, 1, d) for k in range(3))
        w_main, w_small = _split_w_in(w_in[l])
        p, ps = _proj(x, shift, scale, norm_g[l].reshape(1, d), w_main, w_small)

        add_row, alog_row = _gate_rows(b_fgate[l], gdn_dt_bias[l], gdn_A_log[l])
        ft, fx, ga, gb, sm = _gates(ps, add_row, alog_row)

        qg = fox_qn_g[l].reshape(1, HEAD_DIM)
        kg = fox_kn_g[l].reshape(1, HEAD_DIM)
        qn, kn = _foxprep(p, qg, kg)
        lo = _fox_block_start(ft, qg, kg, tq)
        fox_o = _fox(lo, qn, kn, p, fx)

        gq, gk, gkb, gvb, gkbe, gqe, gkd, egl = _gdnprep(p, gdn_conv_w[l], sm, expand, sel)
        gdn_o = _gdn(gq, gk, gkb, gvb, gkbe, gqe, gkd, ga, gb, egl)

        x = _out(x, fox_o, p, gdn_o, gate, gdn_norm_g[l].reshape(1, HEAD_DIM), w_out[l].astype(BF16),
                 final_g.reshape(1, d), final_norm=(l == depth - 1))
    return x
```

```python
import functools

import jax
import jax.numpy as jnp
from jax import lax
from jax.experimental import pallas as pl
from jax.experimental.pallas import tpu as pltpu

F32 = jnp.float32
BF16 = jnp.bfloat16
HIGHEST = lax.Precision.HIGHEST

HEADS = 8
HEAD_DIM = 128
WIDTH = HEADS * HEAD_DIM
CHUNK = 64
CONV_WIDTH = 4
EPS = 1e-6
GATE_COLS = 128
COL_F, COL_A, COL_B, COL_A2, COL_A3, COL_F2, COL_F3, COL_A4 = 0, 8, 16, 24, 32, 40, 48, 56
GATE_GROUPS = "fabaaffa"
LOG2E = 1.4426950408889634
SOLVE_UNROLL = 4
EXP_UNDERFLOW = 104.0
VMEM_LIMIT = 52 * 1024 * 1024

NT_DIMS = (((1,), (1,)), ((), ()))
TN_DIMS = (((0,), (0,)), ((), ()))


def _dot(a, b, precision=None):
    return jnp.dot(a, b, preferred_element_type=F32, precision=precision)


def _dot_nt(a, b, precision=None):
    return lax.dot_general(a, b, NT_DIMS, preferred_element_type=F32, precision=precision)


def _dot_tn(a, b):
    return lax.dot_general(a, b, TN_DIMS, preferred_element_type=F32)


def _silu(x):
    return x * jax.nn.sigmoid(x)


def _params(*sem):
    return pltpu.CompilerParams(dimension_semantics=sem, vmem_limit_bytes=VMEM_LIMIT)


def _ada_kernel(c_ref, w_ref, b_ref, o_ref):
    o_ref[...] = _dot(_silu(c_ref[...]), w_ref[...], HIGHEST) + b_ref[...]


def _ada(c, w, b):
    bsz, d = c.shape
    n = w.shape[1]
    rows = 8
    tn = 512 if n % 512 == 0 else 128
    cp =jnp.pad(c, ((0, rows - bsz), (0, 0)))
    out = pl.pallas_call(
        _ada_kernel,
        grid=(n // tn,),
        in_specs=[pl.BlockSpec((rows, d), lambda j: (0, 0)),
                  pl.BlockSpec((d, tn), lambda j: (0, j)),
                  pl.BlockSpec((1, tn), lambda j: (0, j))],
        out_specs=pl.BlockSpec((rows, tn), lambda j: (0, j)),
        out_shape=jax.ShapeDtypeStruct((rows, n), F32),
        compiler_params=_params("arbitrary"),
        name="ada_mod",
    )(cp, w, b.reshape(1, n))
    return out[:bsz]


def _proj_kernel(x_ref, sh_ref, sc_ref, g_ref, w_ref, ws_ref, p_ref, ps_ref, h_ref, *, tm, rows):
    @pl.when(pl.program_id(2) == 0)
    def _():
        gmul = g_ref[...] * (1.0 + sc_ref[0])
        shift = sh_ref[0]

        def body(r, carry):
            sl = pl.ds(pl.multiple_of(r * rows, rows), rows)
            xs = x_ref[0, sl, :]
            ms = jnp.mean(xs * xs, axis=-1, keepdims=True)
            h_ref[sl, :] = (xs * lax.rsqrt(ms + EPS) * gmul + shift).astype(BF16)
            return carry

        lax.fori_loop(0, tm // rows, body, 0)
        ps_ref[0] = _dot(h_ref[...], ws_ref[...])

    p_ref[0] = _dot(h_ref[...], w_ref[...]).astype(BF16)


def _proj(x, shift, scale, g, w_main, w_small):
    bsz, s, d = x.shape
    n = w_main.shape[1]
    tm = min(1024, s)
    tn = 1024
    kern = functools.partial(_proj_kernel, tm=tm, rows=min(128, tm))
    return pl.pallas_call(
        kern,
        grid=(bsz, s // tm, n // tn),
        in_specs=[pl.BlockSpec((1, tm, d), lambda b, i, j: (b, i, 0)),
                  pl.BlockSpec((1, 1, d), lambda b, i, j: (b, 0, 0)),
                  pl.BlockSpec((1, 1, d), lambda b, i, j: (b, 0, 0)),
                  pl.BlockSpec((1, d), lambda b, i, j: (0, 0)),
                  pl.BlockSpec((d, tn), lambda b, i, j: (0, j)),
                  pl.BlockSpec((d, GATE_COLS), lambda b, i, j: (0, 0))],
        out_specs=[pl.BlockSpec((1, tm, tn), lambda b, i, j: (b, i, j)),
                   pl.BlockSpec((1, tm, GATE_COLS), lambda b, i, j: (b, i, 0))],
        out_shape=[jax.ShapeDtypeStruct((bsz, s, n), BF16),
                   jax.ShapeDtypeStruct((bsz, s, GATE_COLS), F32)],
        scratch_shapes=[pltpu.VMEM((tm, d), BF16)],
        compiler_params=_params("arbitrary", "arbitrary", "arbitrary"),
        name="norm_in_proj",
    )(x, shift, scale, g, w_main, w_small)


def _gates_kernel(ps_ref, add_ref, alog_ref, ft_ref, fx_ref, ga_ref, gb_ref, sm_ref, carry_ref, *, tb):
    @pl.when(pl.program_id(1) == 0)
    def _():
        carry_ref[...] = jnp.zeros_like(carry_ref)

    x = ps_ref[0] + add_ref[...]
    col = lax.broadcasted_iota(jnp.int32, (tb, GATE_COLS), 1)
    grp = lambda start: (col >= start) & (col < start + HEADS)
    is_f = grp(COL_F) | grp(COL_F2) | grp(COL_F3)
    is_a = grp(COL_A) | grp(COL_A2) | grp(COL_A3) | grp(COL_A4)
    log_f = jax.nn.log_sigmoid(x)
    g = -jnp.exp(alog_ref[...]) * jax.nn.softplus(x)
    beta = jax.nn.sigmoid(x)
    vals = jnp.where(is_f, log_f, jnp.where(is_a, g, 0.0))

    r = lax.broadcasted_iota(jnp.int32, (tb, tb), 0)
    c = lax.broadcasted_iota(jnp.int32, (tb, tb), 1)
    same_chunk = (r // CHUNK) == (c // CHUNK)
    sums = jnp.concatenate([(r >= c).astype(F32),
                            ((r >= c) & same_chunk).astype(F32),
                            same_chunk.astype(F32)], axis=0).astype(BF16)

    def split3(t):
        hi = t.astype(BF16).astype(F32)
        mid = (t - hi).astype(BF16).astype(F32)
        return hi, mid, t - hi - mid

    pieces = _dot(sums, jnp.concatenate(split3(vals), axis=1).astype(BF16))
    summed = pieces[:, :GATE_COLS] + pieces[:, GATE_COLS:2 * GATE_COLS] + pieces[:, 2 * GATE_COLS:]
    carry = carry_ref[0:1, :]
    cs_all = summed[0:tb] + carry
    cs = summed[tb:2 * tb]
    tot = summed[2 * tb:]
    carry_ref[...] = jnp.broadcast_to(carry + jnp.sum(vals, axis=0, keepdims=True), carry_ref.shape)

    ft_ref[0] = jnp.transpose(cs_all)[0:HEADS, :]
    f_hi, f_mid, f_lo = split3(cs_all * (-LOG2E))
    fx_ref[0] = jnp.where(grp(COL_F), f_hi, jnp.where(grp(COL_F2), f_mid,
                                                      jnp.where(grp(COL_F3), f_lo, 0.0))).astype(BF16)
    c_hi, c_lo, _ = split3(cs)
    ones = grp(COL_A3) | grp(COL_A4)
    ga_ref[0] = jnp.where(grp(COL_A), c_hi, jnp.where(grp(COL_A2), c_lo, jnp.where(ones, 1.0, 0.0))).astype(BF16)
    gb_ref[0] = jnp.where(grp(COL_A) | grp(COL_A2), 1.0,
                          jnp.where(grp(COL_A3), -c_hi, jnp.where(grp(COL_A4), -c_lo, 0.0))).astype(BF16)
    sm_ref[0] = jnp.where(grp(COL_A), jnp.exp(cs),
                          jnp.where(grp(COL_B), beta,
                                    jnp.where(grp(COL_A2), jnp.exp(tot - cs),
                                              jnp.where(grp(COL_A3), jnp.exp(tot), 0.0))))


def _gates(ps, add_row, alog_row):
    bsz, s, _ = ps.shape
    tb = min(256, s)
    small = pl.BlockSpec((1, tb, GATE_COLS), lambda b, i: (b, i, 0))
    row = pl.BlockSpec((1, GATE_COLS), lambda b, i: (0, 0))
    small_shape = jax.ShapeDtypeStruct((bsz, s, GATE_COLS), F32)
    bf16_shape = jax.ShapeDtypeStruct((bsz, s, GATE_COLS), BF16)
    return pl.pallas_call(
        functools.partial(_gates_kernel, tb=tb),
        grid=(bsz, s // tb),
        in_specs=[small, row, row],
        out_specs=[pl.BlockSpec((1, HEADS, tb), lambda b, i: (b, 0, i)), small, small, small, small],
        out_shape=[jax.ShapeDtypeStruct((bsz, HEADS, s), F32), bf16_shape, bf16_shape, bf16_shape, small_shape],
        scratch_shapes=[pltpu.VMEM((8, GATE_COLS), F32)],
        compiler_params=_params("arbitrary", "arbitrary"),
        name="gates",
    )(ps, add_row, alog_row)


def _foxprep_kernel(q_ref, k_ref, qg_ref, kg_ref, qo_ref, ko_ref):
    def norm(src, gain, dst):
        for h in range(HEADS):
            sl = slice(h * HEAD_DIM, (h + 1) * HEAD_DIM)
            t = src[0, :, sl].astype(F32)
            ms = jnp.mean(t * t, axis=-1, keepdims=True)
            dst[0, :, sl] = (t * lax.rsqrt(ms + EPS) * gain).astype(BF16)

    norm(q_ref, qg_ref[...] * (HEAD_DIM ** -0.5 * LOG2E), qo_ref)
    norm(k_ref, kg_ref[...], ko_ref)


def _foxprep(p, qg, kg):
    bsz, s, _ = p.shape
    tb = min(512, s)
    blk = lambda j: pl.BlockSpec((1, tb, WIDTH), lambda b, i: (b, i, j))
    row = pl.BlockSpec((1, HEAD_DIM), lambda b, i: (0, 0))
    shape = jax.ShapeDtypeStruct((bsz, s, WIDTH), BF16)
    return pl.pallas_call(
        _foxprep_kernel,
        grid=(bsz, s // tb),
        in_specs=[blk(0), blk(1), row, row],
        out_specs=[blk(0), blk(0)],
        out_shape=[shape, shape],
        compiler_params=_params("arbitrary", "arbitrary"),
        name="fox_qk_norm",
    )(p, p, qg, kg)


def _fox_kernel(lo_ref, q_ref, k_ref, v_ref, fx_ref, o_ref, acc_ref, st_ref, *, tk):
    b = pl.program_id(0)
    h = pl.program_id(1)
    g = pl.program_id(2)
    tq = 2 * tk
    lo = lo_ref[b * HEADS + h, 2 * g]
    lane = lax.broadcasted_iota(jnp.int32, (tq, GATE_COLS), 1)
    pick = (lane == COL_F + h) | (lane == COL_F2 + h) | (lane == COL_F3 + h)
    qa = jnp.concatenate([q_ref[0], jnp.where(pick, 1.0, 0.0).astype(BF16)], axis=1)
    acc_ref[...] = jnp.zeros_like(acc_ref)
    late = slice(tk, tq)

    def key_rows(j):
        return pl.ds(pl.multiple_of(j * tk, tk), tk)

    def scores(j, queries=slice(None)):
        ka = jnp.concatenate([k_ref[0, key_rows(j), :], fx_ref[0, key_rows(j), :]], axis=1)
        return _dot_nt(ka, qa[queries])

    def causal(st):
        r = lax.broadcasted_iota(jnp.int32, st.shape, 0)
        c = lax.broadcasted_iota(jnp.int32, st.shape, 1)
        return jnp.where(r <= c, st, -jnp.inf)

    def issue(j, kind, slot):
        if kind == "full":
            st_ref[slot] = scores(j)
        elif kind == "diag":
            st_ref[slot] = causal(scores(j))
        else:
            st_ref[slot, :, late] = causal(scores(j, late))

    def step(cur_j, cur_slot, carry, nxt=None, only_late=False):
        if nxt is not None:
            issue(nxt[0], nxt[1], 1 - cur_slot)
        cols = late if only_late else slice(None)
        m_all, l_all = carry
        m, l = m_all[:, cols], l_all[:, cols]
        st = st_ref[cur_slot, :, cols]
        m_new = jnp.maximum(m, jnp.max(st, axis=0, keepdims=True))
        alpha = jnp.exp2(m - m_new)
        p = jnp.exp2(st - m_new)
        l_new = alpha * l + jnp.sum(p, axis=0, keepdims=True)
        pv = _dot_tn(v_ref[0, key_rows(cur_j), :], p.astype(BF16))
        acc_ref[:, cols] = alpha * acc_ref[:, cols] + pv
        if only_late:
            m_new = jnp.concatenate([m_all[:, :tk], m_new], axis=1)
            l_new = jnp.concatenate([l_all[:, :tk], l_new], axis=1)
        return m_new, l_new

    d = 2 * g
    n = d - lo

    @pl.when(n == 0)
    def _():
        issue(d, "diag", 0)

    @pl.when(n > 0)
    def _():
        issue(lo, "full", 0)

    def two_interior(t, carry):
        j = lo + 2 * t
        carry = step(j, 0, carry, nxt=(j + 1, "full"))
        return step(j + 1, 1, carry, nxt=(j + 2, "full"))

    init = (jnp.full((1, tq), -jnp.inf, F32), jnp.zeros((1, tq), F32))
    pairs = jnp.maximum(n - 1, 0) // 2
    carry = lax.fori_loop(0, pairs, two_interior, init)

    def finish(interior_left):
        def run(carry):
            slot = 0
            for back in range(interior_left, 0, -1):
                carry = step(d - back, slot, carry, nxt=(d - back + 1, "full" if back > 1 else "diag"))
                slot = 1 - slot
            carry = step(d, slot, carry, nxt=(d + 1, "late"))
            return step(d + 1, 1 - slot, carry, only_late=True)
        return run

    left = n - 2 * pairs
    m, l = lax.switch(left, (finish(0), finish(1), finish(2)), carry)
    o_ref[0] = jnp.transpose(acc_ref[...] / l).astype(BF16)


def _fox(lo, qn, kn, p, fx, tk):
    bsz, s, _ = qn.shape
    tq = 2 * tk
    grid_spec = pltpu.PrefetchScalarGridSpec(
        num_scalar_prefetch=1,
        grid=(bsz, HEADS, s // tq),
        in_specs=[pl.BlockSpec((1, tq, HEAD_DIM), lambda b, h, i, lo_r: (b, i, h)),
                  pl.BlockSpec((1, s, HEAD_DIM), lambda b, h, i, lo_r: (b, 0, h)),
                  pl.BlockSpec((1, s, HEAD_DIM), lambda b, h, i, lo_r: (b, 0, 2 * HEADS + h)),
                  pl.BlockSpec((1, s, GATE_COLS), lambda b, h, i, lo_r: (b, 0, 0))],
        out_specs=pl.BlockSpec((1, tq, HEAD_DIM), lambda b, h, i, lo_r: (b, i, h)),
        scratch_shapes=[pltpu.VMEM((HEAD_DIM, tq), F32), pltpu.VMEM((2, tk, tq), F32)],
    )
    return pl.pallas_call(
        functools.partial(_fox_kernel, tk=tk),
        grid_spec=grid_spec,
        out_shape=jax.ShapeDtypeStruct((bsz, s, WIDTH), BF16),
        compiler_params=_params("arbitrary", "arbitrary", "arbitrary"),
        name="fox_attention",
    )(lo, qn, kn, p, fx)


def _fox_block_start(ft, qg, kg, tq):
    bsz, _, s = ft.shape
    f_first = ft[:, :, 0::tq]
    f_last = ft[:, :, tq - 1::tq]
    qk_bound = 1.02 * (HEAD_DIM ** 0.5) * jnp.max(jnp.abs(qg)) * jnp.max(jnp.abs(kg))
    thresh = EXP_UNDERFLOW + 2.0 * qk_bound
    skip = f_last[:, :, None, :] > f_first[:, :, :, None] + thresh
    return jnp.sum(skip, axis=-1).astype(jnp.int32).reshape(bsz * HEADS, s // tq)


def _gdnprep_kernel(q_ref, k_ref, v_ref, qh_ref, kh_ref, vh_ref, cw_ref, sm_ref, ex_ref, sel_ref,
                    qo_ref, ko_ref, kbo_ref, vbo_ref, kbeo_ref, qeo_ref, kdo_ref, eglo_ref, *, tb):
    first = pl.program_id(1) == 0

    def conv_silu(src, halo, which):
        ext = jnp.concatenate([jnp.where(first, 0.0, halo[0].astype(F32)), src[0].astype(F32)], axis=0)
        acc = jnp.zeros((tb, WIDTH), F32)
        for t in range(CONV_WIDTH):
            w_row = cw_ref[t:t + 1, which * WIDTH:(which + 1) * WIDTH]
            back = CONV_WIDTH - 1 - t
            shifted = pltpu.roll(ext, back, axis=0) if back else ext
            acc = acc + shifted[8:, :] * w_row
        return _silu(acc)

    def spread(group):
        e = ex_ref[group].astype(BF16)
        return _dot(sm2, jnp.concatenate([e, e], axis=0))

    def l2norm(t):
        parts = []
        for h in range(HEADS):
            th = t[:, h * HEAD_DIM:(h + 1) * HEAD_DIM]
            parts.append(th * lax.rsqrt(jnp.sum(th * th, axis=-1, keepdims=True) + EPS))
        return jnp.concatenate(parts, axis=-1)

    sm = sm_ref[0]
    sm_hi = sm.astype(BF16)
    sm2 = jnp.concatenate([sm_hi, (sm - sm_hi.astype(F32)).astype(BF16)], axis=1)
    beta_x = spread(0)
    egc_x = spread(1)
    edec_x = spread(2)
    eglo_ref[0] = _dot(_dot(sel_ref[...], sm, HIGHEST), ex_ref[3], HIGHEST)

    k = l2norm(conv_silu(k_ref, kh_ref, 1))
    kb = k * beta_x
    ko_ref[0] = k.astype(BF16)
    kbo_ref[0] = kb.astype(BF16)
    kbeo_ref[0] = (kb * egc_x).astype(BF16)
    kdo_ref[0] = (k * edec_x).astype(BF16)
    q = l2norm(conv_silu(q_ref, qh_ref, 0)) * (HEAD_DIM ** -0.5)
    qo_ref[0] = q.astype(BF16)
    qeo_ref[0] = (q * egc_x).astype(BF16)
    v = conv_silu(v_ref, vh_ref, 2)
    vbo_ref[0] = (v * beta_x).astype(BF16)


def _gdnprep(p, conv_w, sm, expand, sel):
    bsz, s, _ = p.shape
    tb = min(512, s)
    nchunk = tb // CHUNK
    blk = lambda j: pl.BlockSpec((1, tb, WIDTH), lambda b, i: (b, i, j))
    halo = lambda j: pl.BlockSpec((1, 8, WIDTH), lambda b, i: (b, jnp.maximum(i * (tb // 8) - 1, 0), j))
    shape = jax.ShapeDtypeStruct((bsz, s, WIDTH), BF16)
    return pl.pallas_call(
        functools.partial(_gdnprep_kernel, tb=tb),
        grid=(bsz, s // tb),
        in_specs=[blk(4), blk(5), blk(6), halo(4), halo(5), halo(6),
                  pl.BlockSpec((CONV_WIDTH, 3 * WIDTH), lambda b, i: (0, 0)),
                  pl.BlockSpec((1, tb, GATE_COLS), lambda b, i: (b, i, 0)),
                  pl.BlockSpec((4, GATE_COLS, WIDTH), lambda b, i: (0, 0, 0)),
                  pl.BlockSpec((nchunk, tb), lambda b, i: (0, 0))],
        out_specs=[blk(0)] * 7 + [pl.BlockSpec((1, nchunk, WIDTH), lambda b, i: (b, i, 0))],
        out_shape=[shape] * 7 + [jax.ShapeDtypeStruct((bsz, s // CHUNK, WIDTH), F32)],
        compiler_params=_params("arbitrary", "arbitrary"),
        name="gdn_prep",
    )(p, p, p, p, p, p, conv_w, sm, expand, sel)


def _gdn_kernel(q_ref, k_ref, kb_ref, vb_ref, kbe_ref, ga_ref, gb_ref, qe_ref, kd_ref, egl_ref,
                o_ref, s_ref, u_ref, w_ref, attn_ref, *, nchunk, nblk):
    i = pl.program_id(1)

    @pl.when(i == 0)
    def _():
        s_ref[...] = jnp.zeros_like(s_ref)

    r = lax.broadcasted_iota(jnp.int32, (CHUNK, CHUNK), 0)
    c = lax.broadcasted_iota(jnp.int32, (CHUNK, CHUNK), 1)
    lower = r >= c
    strict = r > c
    eye = (r == c).astype(F32)
    diag8 = (r // 8) == (c // 8)
    levels = [((r // (2 * w)) == (c // (2 * w))) & ((r // w) != (c // w)) for w in (8, 16, 32)]
    gcol = lax.broadcasted_iota(jnp.int32, (CHUNK, GATE_COLS), 1)
    heads = range(HEADS)
    hsl = [slice(h * HEAD_DIM, (h + 1) * HEAD_DIM) for h in heads]
    head_cols = [(gcol == COL_A + h) | (gcol == COL_A2 + h) | (gcol == COL_A3 + h) | (gcol == COL_A4 + h)
                 for h in heads]

    unroll = range(SOLVE_UNROLL)

    def chunk_rows(it, t):
        return pl.ds(pl.multiple_of((it * SOLVE_UNROLL + t) * CHUNK, CHUNK), CHUNK)

    def solve_stages(it):
        probs = [(t, h) for t in unroll for h in heads]
        n = range(len(probs))
        rows = [chunk_rows(it, t) for t in unroll]
        ga = [ga_ref[0, rows[t], :] for t in unroll]
        gb = [gb_ref[0, rows[t], :] for t in unroll]
        gbh = [jnp.where(head_cols[h], gb[t], jnp.zeros((), BF16)) for t, h in probs]
        dlog = [_dot_nt(ga[t], gbh[p]) for p, (t, h) in enumerate(probs)]
        kk = [_dot_nt(kb_ref[0, rows[t], hsl[h]], k_ref[0, rows[t], hsl[h]]) for t, h in probs]
        qk = [_dot_nt(q_ref[0, rows[t], hsl[h]], k_ref[0, rows[t], hsl[h]]) for t, h in probs]
        yield
        decay = [jnp.exp(jnp.where(lower, dlog[p], -jnp.inf)) for p in n]
        m = [jnp.where(strict, kk[p] * decay[p], 0.0) for p in n]
        attn = [(qk[p] * decay[p]).astype(BF16) for p in n]
        md = [jnp.where(diag8, m[p], 0.0).astype(BF16) for p in n]
        m2 = [_dot(md[p], md[p]).astype(BF16) for p in n]
        yield
        inv = [eye - md[p].astype(F32) for p in n]
        m4 = [_dot(m2[p], m2[p]).astype(BF16) for p in n]
        inv = [inv[p] + _dot(inv[p].astype(BF16), m2[p]) for p in n]
        yield
        inv = [inv[p] + _dot(inv[p].astype(BF16), m4[p]) for p in n]
        yield
        for lvl in levels:
            off = [jnp.where(lvl, m[p], 0.0).astype(BF16) for p in n]
            invb = [inv[p].astype(BF16) for p in n]
            x = [_dot(invb[p], off[p]).astype(BF16) for p in n]
            yield
            inv = [inv[p] - _dot(x[p], invb[p]) for p in n]
            yield
        uw = [_dot(inv[p].astype(BF16),
                   jnp.concatenate([vb_ref[0, rows[t], hsl[h]], kbe_ref[0, rows[t], hsl[h]]], axis=-1))
              for p, (t, h) in enumerate(probs)]
        yield
        for p, (t, h) in enumerate(probs):
            attn_ref[rows[t], h * HEAD_DIM:h * HEAD_DIM + CHUNK] = attn[p]
            u_ref[rows[t], hsl[h]] = uw[p][:, :HEAD_DIM]
            w_ref[rows[t], hsl[h]] = uw[p][:, HEAD_DIM:].astype(BF16)

    def scan_stages(it):
        state = [s_ref[h] for h in heads]
        for t in unroll:
            rows = chunk_rows(it, t)
            egl = egl_ref[0, pl.ds(it * SOLVE_UNROLL + t, 1), :]
            u = [u_ref[rows, hsl[h]] for h in heads]
            attn = [attn_ref[rows, h * HEAD_DIM:h * HEAD_DIM + CHUNK] for h in heads]
            ws = [_dot(jnp.concatenate([w_ref[rows, hsl[h]], qe_ref[0, rows, hsl[h]]], axis=0),
                       state[h].astype(BF16)) for h in heads]
            yield
            v_new = [(u[h] - ws[h][:CHUNK]).astype(BF16) for h in heads]
            for h in heads:
                o_ref[0, rows, hsl[h]] = (ws[h][CHUNK:] + _dot(attn[h], v_new[h])).astype(BF16)
            state = [state[h] * egl[:, hsl[h]] + _dot_tn(kd_ref[0, rows, hsl[h]], v_new[h]) for h in heads]
            yield
        for h in heads:
            s_ref[h] = state[h]

    def drain(gen):
        for _ in gen:
            pass

    def solve_only(it, carry):
        drain(solve_stages(it))
        return carry

    def scan_only(it, carry):
        drain(scan_stages(it))
        return carry

    def solve_and_scan(it, carry):
        scan = scan_stages(it)
        solve = solve_stages(it)
        solve_yields = 5 + 2 * len(levels)
        scan_yields = 2 * SOLVE_UNROLL
        for stage in range(solve_yields):
            next(solve)
            if (stage * scan_yields) // solve_yields != ((stage + 1) * scan_yields) // solve_yields:
                next(scan, None)
        drain(scan)
        drain(solve)
        return carry

    iters = nchunk // SOLVE_UNROLL

    @pl.when(i == 0)
    def _():
        lax.fori_loop(0, iters, solve_only, 0)

    @pl.when((i > 0) & (i < nblk))
    def _():
        lax.fori_loop(0, iters, solve_and_scan, 0)

    @pl.when(i == nblk)
    def _():
        lax.fori_loop(0, iters, scan_only, 0)


def _gdn(q, k, kb, vb, kbe, qe, kd, ga, gb, egl):
    bsz, s, _ = q.shape
    tb = min(512, s)
    nchunk = tb // CHUNK
    nblk = s // tb
    cur = lambda b, i: (b, jnp.minimum(i, nblk - 1), 0)
    prev = lambda b, i: (b, jnp.maximum(i - 1, 0), 0)
    return pl.pallas_call(
        functools.partial(_gdn_kernel, nchunk=nchunk, nblk=nblk),
        grid=(bsz, nblk + 1),
        in_specs=[pl.BlockSpec((1, tb, WIDTH), cur)] * 5 + [pl.BlockSpec((1, tb, GATE_COLS), cur)] * 2
                 + [pl.BlockSpec((1, tb, WIDTH), prev)] * 2 + [pl.BlockSpec((1, nchunk, WIDTH), prev)],
        out_specs=pl.BlockSpec((1, tb, WIDTH), prev),
        out_shape=jax.ShapeDtypeStruct((bsz, s, WIDTH), BF16),
        scratch_shapes=[pltpu.VMEM((HEADS, HEAD_DIM, HEAD_DIM), F32),
                        pltpu.VMEM((tb, WIDTH), F32), pltpu.VMEM((tb, WIDTH), BF16),
                        pltpu.VMEM((tb, WIDTH), BF16)],
        compiler_params=_params("arbitrary", "arbitrary"),
        name="gdn_delta_rule",
    )(q, k, kb, vb, kbe, ga, gb, qe, kd, egl)


def _out_kernel(x_ref, fo_ref, fz_ref, go_ref, gz_ref, gate_ref, gng_ref, w_ref, fg_ref, o_ref, *, final_norm):
    a = fo_ref[0].astype(F32) * _silu(fz_ref[0].astype(F32))
    parts = []
    for h in range(HEADS):
        sl = slice(h * HEAD_DIM, (h + 1) * HEAD_DIM)
        t = go_ref[0, :, sl].astype(F32)
        ms = jnp.mean(t * t, axis=-1, keepdims=True)
        parts.append(t * lax.rsqrt(ms + EPS) * gng_ref[...])
    g = jnp.concatenate(parts, axis=-1) * _silu(gz_ref[0].astype(F32))
    y = _dot(a.astype(BF16), w_ref[0:WIDTH, :]) + _dot(g.astype(BF16), w_ref[WIDTH:2 * WIDTH, :])
    xn = x_ref[0] + gate_ref[0] * y
    if final_norm:
        ms = jnp.mean(xn * xn, axis=-1, keepdims=True)
        xn = xn * lax.rsqrt(ms + EPS) * fg_ref[...]
    o_ref[0] = xn


def _out(x, fox_o, p, gdn_o, gate, gng, w_out, final_g, final_norm):
    bsz, s, d = x.shape
    tm = min(512, s)
    blk = lambda j: pl.BlockSpec((1, tm, WIDTH), lambda b, i: (b, i, j))
    return pl.pallas_call(
        functools.partial(_out_kernel, final_norm=final_norm),
        grid=(bsz, s // tm),
        in_specs=[pl.BlockSpec((1, tm, d), lambda b, i: (b, i, 0)),
                  blk(0), blk(3), blk(0), blk(7),
                  pl.BlockSpec((1, 1, d), lambda b, i: (b, 0, 0)),
                  pl.BlockSpec((1, HEAD_DIM), lambda b, i: (0, 0)),
                  pl.BlockSpec((2 * WIDTH, d), lambda b, i: (0, 0)),
                  pl.BlockSpec((1, d), lambda b, i: (0, 0))],
        out_specs=pl.BlockSpec((1, tm, d), lambda b, i: (b, i, 0)),
        out_shape=jax.ShapeDtypeStruct((bsz, s, d), F32),
        compiler_params=_params("arbitrary", "arbitrary"),
        name="gate_out_proj",
    )(x, fox_o, p, gdn_o, p, gate, gng, w_out, final_g)


def _expand_matrices():
    lane_head = jnp.arange(WIDTH) // HEAD_DIM
    rows = jnp.arange(GATE_COLS)[:, None]
    mats = [(rows == grp + lane_head[None, :]).astype(F32) for grp in (COL_B, COL_A, COL_A2, COL_A3)]
    return jnp.stack(mats)


def _chunk_last_selector(tb):
    nchunk = tb // CHUNK
    return (jnp.arange(tb)[None, :] == (jnp.arange(nchunk)[:, None] * CHUNK + CHUNK - 1)).astype(F32)


def _split_w_in(w):
    fw = WIDTH
    o_ff = 4 * fw
    o_g = o_ff + HEADS
    o_ga = o_g + 4 * fw
    o_gb = o_ga + HEADS
    w_main = jnp.concatenate([w[:, :o_ff], w[:, o_g:o_ga]], axis=1).astype(BF16)
    cols = {"f": w[:, o_ff:o_g], "a": w[:, o_ga:o_gb], "b": w[:, o_gb:o_gb + HEADS]}
    pad = jnp.zeros((w.shape[0], GATE_COLS - len(GATE_GROUPS) * HEADS), w.dtype)
    w_small = jnp.concatenate([cols[t] for t in GATE_GROUPS] + [pad], axis=1).astype(BF16)
    return w_main, w_small


def _gate_rows(b_f, dt_bias, a_log):
    z = jnp.zeros((HEADS,), F32)
    pad = jnp.zeros((GATE_COLS - len(GATE_GROUPS) * HEADS,), F32)
    add = {"f": b_f, "a": dt_bias, "b": z}
    alog = {"f": z, "a": a_log, "b": z}
    add_row = jnp.concatenate([add[t] for t in GATE_GROUPS] + [pad]).reshape(1, GATE_COLS)
    alog_row = jnp.concatenate([alog[t] for t in GATE_GROUPS] + [pad]).reshape(1, GATE_COLS)
    return add_row.astype(F32), alog_row.astype(F32)


def kernel(x, c, norm_g, w_ada, b_ada, w_in, b_fgate, fox_qn_g, fox_kn_g, gdn_conv_w, gdn_A_log,
           gdn_dt_bias, gdn_norm_g, w_out, final_g):
    bsz, s, d = x.shape
    depth = w_in.shape[0]
    expand = _expand_matrices()
    sel = _chunk_last_selector(min(512, s))
    tk = min(512, s // 2)
    for l in range(depth):
        mod = _ada(c, w_ada[l], b_ada[l])
        shift, scale, gate = (mod[:, k * d:(k + 1) * d].reshape(bsz, 1, d) for k in range(3))
        w_main, w_small = _split_w_in(w_in[l])
        p, ps = _proj(x, shift, scale, norm_g[l].reshape(1, d), w_main, w_small)

        add_row, alog_row = _gate_rows(b_fgate[l], gdn_dt_bias[l], gdn_A_log[l])
        ft, fx, ga, gb, sm = _gates(ps, add_row, alog_row)

        qg = fox_qn_g[l].reshape(1, HEAD_DIM)
        kg = fox_kn_g[l].reshape(1, HEAD_DIM)
        qn, kn = _foxprep(p, qg, kg)
        lo = _fox_block_start(ft, qg, kg, tk)
        fox_o = _fox(lo, qn, kn, p, fx, tk)

        gq, gk, gkb, gvb, gkbe, gqe, gkd, egl = _gdnprep(p, gdn_conv_w[l], sm, expand, sel)
        gdn_o = _gdn(gq, gk, gkb, gvb, gkbe, gqe, gkd, ga, gb, egl)

        x = _out(x, fox_o, p, gdn_o, gate, gdn_norm_g[l].reshape(1, HEAD_DIM), w_out[l].astype(BF16),
                 final_g.reshape(1, d), final_norm=(l == depth - 1))
    return x
```

```python
import functools

import jax
import jax.numpy as jnp
from jax import lax
from jax.experimental import pallas as pl
from jax.experimental.pallas import tpu as pltpu

F32 = jnp.float32
BF16 = jnp.bfloat16
HIGHEST = lax.Precision.HIGHEST

HEADS = 8
HEAD_DIM = 128
WIDTH = HEADS * HEAD_DIM
CHUNK = 64
CONV_WIDTH = 4
EPS = 1e-6
GATE_COLS = 128
COL_F, COL_A, COL_B, COL_A2, COL_A3, COL_F2, COL_F3, COL_A4 = 0, 8, 16, 24, 32, 40, 48, 56
GATE_GROUPS = "fabaaffa"
LOG2E = 1.4426950408889634
SOLVE_UNROLL = 4
EXP_UNDERFLOW = 104.0
VMEM_LIMIT = 52 * 1024 * 1024

NT_DIMS = (((1,), (1,)), ((), ()))
TN_DIMS = (((0,), (0,)), ((), ()))


def _dot(a, b, precision=None):
    return jnp.dot(a, b, preferred_element_type=F32, precision=precision)


def _dot_nt(a, b, precision=None):
    return lax.dot_general(a, b, NT_DIMS, preferred_element_type=F32, precision=precision)


def _dot_tn(a, b):
    return lax.dot_general(a, b, TN_DIMS, preferred_element_type=F32)


def _silu(x):
    return x * jax.nn.sigmoid(x)


def _params(*sem):
    return pltpu.CompilerParams(dimension_semantics=sem, vmem_limit_bytes=VMEM_LIMIT)


def _ada_kernel(c_ref, w_ref, b_ref, o_ref):
    o_ref[...] = _dot(_silu(c_ref[...]), w_ref[...], HIGHEST) + b_ref[...]


def _ada(c, w, b):
    bsz, d = c.shape
    n = w.shape[1]
    rows = 8
    tn = 512 if n % 512 == 0 else 128
    cp =jnp.pad(c, ((0, rows - bsz), (0, 0)))
    out = pl.pallas_call(
        _ada_kernel,
        grid=(n // tn,),
        in_specs=[pl.BlockSpec((rows, d), lambda j: (0, 0)),
                  pl.BlockSpec((d, tn), lambda j: (0, j)),
                  pl.BlockSpec((1, tn), lambda j: (0, j))],
        out_specs=pl.BlockSpec((rows, tn), lambda j: (0, j)),
        out_shape=jax.ShapeDtypeStruct((rows, n), F32),
        compiler_params=_params("arbitrary"),
        name="ada_mod",
    )(cp, w, b.reshape(1, n))
    return out[:bsz]


def _head_rms(t, gain):
    parts = []
    for h in range(HEADS):
        th = t[:, h * HEAD_DIM:(h + 1) * HEAD_DIM]
        parts.append(th * lax.rsqrt(jnp.mean(th * th, axis=-1, keepdims=True) + EPS) * gain)
    return jnp.concatenate(parts, axis=-1)


def _head_l2(t, scale):
    parts = []
    for h in range(HEADS):
        th = t[:, h * HEAD_DIM:(h + 1) * HEAD_DIM]
        parts.append(th * (lax.rsqrt(jnp.sum(th * th, axis=-1, keepdims=True) + EPS) * scale))
    return jnp.concatenate(parts, axis=-1)


def _proj_kernel(x_ref, sh_ref, sc_ref, g_ref, w_ref, ws_ref, qg_ref, kg_ref, cw_ref,
                 p_ref, ps_ref, h_ref, raw_ref, halo_ref, *, tm, rows, ntiles):
    i = pl.program_id(1)
    j = pl.program_id(2)

    @pl.when((i == 0) & (j == 0))
    def _():
        halo_ref[...] = jnp.zeros_like(halo_ref)

    def conv_silu(t, which):
        ext = jnp.concatenate([halo_ref[which], t], axis=0)
        halo_ref[which] = t[tm - 8:, :]
        acc = jnp.zeros((tm, WIDTH), F32)
        for tap in range(CONV_WIDTH):
            w_row = cw_ref[tap:tap + 1, which * WIDTH:(which + 1) * WIDTH]
            back = CONV_WIDTH - 1 - tap
            shifted = pltpu.roll(ext, back, axis=0) if back else ext
            acc = acc + shifted[8:, :] * w_row
        return _silu(acc)

    post = {0: lambda t: _head_rms(t, qg_ref[...]),
            1: lambda t: _head_rms(t, kg_ref[...]),
            4: lambda t: _head_l2(conv_silu(t, 0), HEAD_DIM ** -0.5),
            5: lambda t: _head_l2(conv_silu(t, 1), 1.0),
            6: lambda t: conv_silu(t, 2)}

    def prologue():
        gmul = g_ref[...] * (1.0 + sc_ref[0])
        shift = sh_ref[0]

        def body(r, carry):
            sl = pl.ds(pl.multiple_of(r * rows, rows), rows)
            xs = x_ref[0, sl, :]
            ms = jnp.mean(xs * xs, axis=-1, keepdims=True)
            h_ref[sl, :] = (xs * lax.rsqrt(ms + EPS) * gmul + shift).astype(BF16)
            return carry

        lax.fori_loop(0, tm // rows, body, 0)
        ps_ref[0] = _dot(h_ref[...], ws_ref[...])

    for t in range(ntiles + 1):
        @pl.when(j == t)
        def _(t=t):
            if t == 0:
                prologue()
            if t < ntiles:
                raw_ref[t % 2] = _dot(h_ref[...], w_ref[...])
            if t >= 1:
                raw = raw_ref[(t - 1) % 2]
                p_ref[0] = post.get(t - 1, lambda v: v)(raw).astype(BF16)


def _proj(x, shift, scale, g, w_main, w_small, qg, kg, conv_w):
    bsz, s, d = x.shape
    n = w_main.shape[1]
    tm = min(512, s)
    tn = WIDTH
    ntiles = n // tn
    kern = functools.partial(_proj_kernel, tm=tm, rows=min(128, tm), ntiles=ntiles)
    const = lambda b, i, j: (0, 0)
    return pl.pallas_call(
        kern,
        grid=(bsz, s // tm, ntiles + 1),
        in_specs=[pl.BlockSpec((1, tm, d), lambda b, i, j: (b, i, 0)),
                  pl.BlockSpec((1, 1, d), lambda b, i, j: (b, 0, 0)),
                  pl.BlockSpec((1, 1, d), lambda b, i, j: (b, 0, 0)),
                  pl.BlockSpec((1, d), const),
                  pl.BlockSpec((d, tn), lambda b, i, j: (0, jnp.minimum(j, ntiles - 1))),
                  pl.BlockSpec((d, GATE_COLS), const),
                  pl.BlockSpec((1, HEAD_DIM), const),
                  pl.BlockSpec((1, HEAD_DIM), const),
                  pl.BlockSpec((CONV_WIDTH, 3 * WIDTH), const)],
        out_specs=[pl.BlockSpec((1, tm, tn), lambda b, i, j: (b, i, jnp.maximum(j - 1, 0))),
                   pl.BlockSpec((1, tm, GATE_COLS), lambda b, i, j: (b, i, 0))],
        out_shape=[jax.ShapeDtypeStruct((bsz, s, n), BF16),
                   jax.ShapeDtypeStruct((bsz, s, GATE_COLS), F32)],
        scratch_shapes=[pltpu.VMEM((tm, d), BF16), pltpu.VMEM((2, tm, tn), F32),
                        pltpu.VMEM((3, 8, tn), F32)],
        compiler_params=_params("arbitrary", "arbitrary", "arbitrary"),
        name="norm_in_proj",
    )(x, shift, scale, g, w_main, w_small, qg, kg, conv_w)


def _gates_kernel(ps_ref, add_ref, alog_ref, ft_ref, fx_ref, ga_ref, gb_ref, sm_ref, carry_ref, *, tb):
    @pl.when(pl.program_id(1) == 0)
    def _():
        carry_ref[...] = jnp.zeros_like(carry_ref)

    x = ps_ref[0] + add_ref[...]
    col = lax.broadcasted_iota(jnp.int32, (tb, GATE_COLS), 1)
    grp = lambda start: (col >= start) & (col < start + HEADS)
    is_f = grp(COL_F) | grp(COL_F2) | grp(COL_F3)
    is_a = grp(COL_A) | grp(COL_A2) | grp(COL_A3) | grp(COL_A4)
    log_f = jax.nn.log_sigmoid(x)
    g = -jnp.exp(alog_ref[...]) * jax.nn.softplus(x)
    beta = jax.nn.sigmoid(x)
    vals = jnp.where(is_f, log_f, jnp.where(is_a, g, 0.0))

    r = lax.broadcasted_iota(jnp.int32, (tb, tb), 0)
    c = lax.broadcasted_iota(jnp.int32, (tb, tb), 1)
    same_chunk = (r // CHUNK) == (c // CHUNK)
    sums = jnp.concatenate([(r >= c).astype(F32),
                            ((r >= c) & same_chunk).astype(F32),
                            same_chunk.astype(F32)], axis=0).astype(BF16)

    def split3(t):
        hi = t.astype(BF16).astype(F32)
        mid = (t - hi).astype(BF16).astype(F32)
        return hi, mid, t - hi - mid

    pieces = _dot(sums, jnp.concatenate(split3(vals), axis=1).astype(BF16))
    summed = pieces[:, :GATE_COLS] + pieces[:, GATE_COLS:2 * GATE_COLS] + pieces[:, 2 * GATE_COLS:]
    carry = carry_ref[0:1, :]
    cs_all = summed[0:tb] + carry
    cs = summed[tb:2 * tb]
    tot = summed[2 * tb:]
    carry_ref[...] = jnp.broadcast_to(carry + jnp.sum(vals, axis=0, keepdims=True), carry_ref.shape)

    ft_ref[0] = jnp.transpose(cs_all)[0:HEADS, :]
    f_hi, f_mid, f_lo = split3(cs_all * (-LOG2E))
    fx_ref[0] = jnp.where(grp(COL_F), f_hi, jnp.where(grp(COL_F2), f_mid,
                                                      jnp.where(grp(COL_F3), f_lo, 0.0))).astype(BF16)
    c_hi, c_lo, _ = split3(cs)
    ones = grp(COL_A3) | grp(COL_A4)
    ga_ref[0] = jnp.where(grp(COL_A), c_hi, jnp.where(grp(COL_A2), c_lo, jnp.where(ones, 1.0, 0.0))).astype(BF16)
    gb_ref[0] = jnp.where(grp(COL_A) | grp(COL_A2), 1.0,
                          jnp.where(grp(COL_A3), -c_hi, jnp.where(grp(COL_A4), -c_lo, 0.0))).astype(BF16)
    sm_ref[0] = jnp.where(grp(COL_A), jnp.exp(cs),
                          jnp.where(grp(COL_B), beta,
                                    jnp.where(grp(COL_A2), jnp.exp(tot - cs),
                                              jnp.where(grp(COL_A3), jnp.exp(tot), 0.0))))


def _gates(ps, add_row, alog_row):
    bsz, s, _ = ps.shape
    tb = min(256, s)
    small = pl.BlockSpec((1, tb, GATE_COLS), lambda b, i: (b, i, 0))
    row = pl.BlockSpec((1, GATE_COLS), lambda b, i: (0, 0))
    small_shape = jax.ShapeDtypeStruct((bsz, s, GATE_COLS), F32)
    bf16_shape = jax.ShapeDtypeStruct((bsz, s, GATE_COLS), BF16)
    return pl.pallas_call(
        functools.partial(_gates_kernel, tb=tb),
        grid=(bsz, s // tb),
        in_specs=[small, row, row],
        out_specs=[pl.BlockSpec((1, HEADS, tb), lambda b, i: (b, 0, i)), small, small, small, small],
        out_shape=[jax.ShapeDtypeStruct((bsz, HEADS, s), F32), bf16_shape, bf16_shape, bf16_shape, small_shape],
        scratch_shapes=[pltpu.VMEM((8, GATE_COLS), F32)],
        compiler_params=_params("arbitrary", "arbitrary"),
        name="gates",
    )(ps, add_row, alog_row)


def _fox_kernel(lo_ref, q_ref, k_ref, v_ref, fx_ref, o_ref, acc_ref, st_ref, *, tk):
    b = pl.program_id(0)
    h = pl.program_id(1)
    g = pl.program_id(2)
    tq = 2 * tk
    lo = lo_ref[b * HEADS + h, 2 * g]
    lane = lax.broadcasted_iota(jnp.int32, (tq, GATE_COLS), 1)
    pick = (lane == COL_F + h) | (lane == COL_F2 + h) | (lane == COL_F3 + h)
    qa = jnp.concatenate([q_ref[0], jnp.where(pick, 1.0, 0.0).astype(BF16)], axis=1)
    acc_ref[...] = jnp.zeros_like(acc_ref)
    late = slice(tk, tq)

    def key_rows(j):
        return pl.ds(pl.multiple_of(j * tk, tk), tk)

    def scores(j, queries=slice(None)):
        ka = jnp.concatenate([k_ref[0, key_rows(j), :], fx_ref[0, key_rows(j), :]], axis=1)
        return _dot_nt(ka, qa[queries])

    def causal(st):
        r = lax.broadcasted_iota(jnp.int32, st.shape, 0)
        c = lax.broadcasted_iota(jnp.int32, st.shape, 1)
        return jnp.where(r <= c, st, -jnp.inf)

    def issue(j, kind, slot):
        if kind == "full":
            st_ref[slot] = scores(j)
        elif kind == "diag":
            st_ref[slot] = causal(scores(j))
        else:
            st_ref[slot, :, late] = causal(scores(j, late))

    def step(cur_j, cur_slot, carry, nxt=None, only_late=False):
        if nxt is not None:
            issue(nxt[0], nxt[1], 1 - cur_slot)
        cols = late if only_late else slice(None)
        m_all, l_all = carry
        m, l = m_all[:, cols], l_all[:, cols]
        st = st_ref[cur_slot, :, cols]
        m_new = jnp.maximum(m, jnp.max(st, axis=0, keepdims=True))
        alpha = jnp.exp2(m - m_new)
        p = jnp.exp2(st - m_new)
        l_new = alpha * l + jnp.sum(p, axis=0, keepdims=True)
        pv = _dot_tn(v_ref[0, key_rows(cur_j), :], p.astype(BF16))
        acc_ref[:, cols] = alpha * acc_ref[:, cols] + pv
        if only_late:
            m_new = jnp.concatenate([m_all[:, :tk], m_new], axis=1)
            l_new = jnp.concatenate([l_all[:, :tk], l_new], axis=1)
        return m_new, l_new

    d = 2 * g
    n = d - lo

    @pl.when(n == 0)
    def _():
        issue(d, "diag", 0)

    @pl.when(n > 0)
    def _():
        issue(lo, "full", 0)

    def two_interior(t, carry):
        j = lo + 2 * t
        carry = step(j, 0, carry, nxt=(j + 1, "full"))
        return step(j + 1, 1, carry, nxt=(j + 2, "full"))

    init = (jnp.full((1, tq), -jnp.inf, F32), jnp.zeros((1, tq), F32))
    pairs = jnp.maximum(n - 1, 0) // 2
    carry = lax.fori_loop(0, pairs, two_interior, init)

    def finish(interior_left):
        def run(carry):
            slot = 0
            for back in range(interior_left, 0, -1):
                carry = step(d - back, slot, carry, nxt=(d - back + 1, "full" if back > 1 else "diag"))
                slot = 1 - slot
            carry = step(d, slot, carry, nxt=(d + 1, "late"))
            return step(d + 1, 1 - slot, carry, only_late=True)
        return run

    left = n - 2 * pairs
    m, l = lax.switch(left, (finish(0), finish(1), finish(2)), carry)
    o_ref[0] = jnp.transpose(acc_ref[...] / l).astype(BF16)


def _fox(lo, p, fx, tk):
    bsz, s, _ = p.shape
    tq = 2 * tk
    grid_spec = pltpu.PrefetchScalarGridSpec(
        num_scalar_prefetch=1,
        grid=(bsz, HEADS, s // tq),
        in_specs=[pl.BlockSpec((1, tq, HEAD_DIM), lambda b, h, i, lo_r: (b, i, h)),
                  pl.BlockSpec((1, s, HEAD_DIM), lambda b, h, i, lo_r: (b, 0, HEADS + h)),
                  pl.BlockSpec((1, s, HEAD_DIM), lambda b, h, i, lo_r: (b, 0, 2 * HEADS + h)),
                  pl.BlockSpec((1, s, GATE_COLS), lambda b, h, i, lo_r: (b, 0, 0))],
        out_specs=pl.BlockSpec((1, tq, HEAD_DIM), lambda b, h, i, lo_r: (b, i, h)),
        scratch_shapes=[pltpu.VMEM((HEAD_DIM, tq), F32), pltpu.VMEM((2, tk, tq), F32)],
    )
    return pl.pallas_call(
        functools.partial(_fox_kernel, tk=tk),
        grid_spec=grid_spec,
        out_shape=jax.ShapeDtypeStruct((bsz, s, WIDTH), BF16),
        compiler_params=_params("arbitrary", "arbitrary", "arbitrary"),
        name="fox_attention",
    )(lo, p, p, p, fx)


def _fox_block_start(ft, qg, kg, tq):
    bsz, _, s = ft.shape
    f_first = ft[:, :, 0::tq]
    f_last = ft[:, :, tq - 1::tq]
    qk_bound = 1.02 * (HEAD_DIM ** 0.5) * jnp.max(jnp.abs(qg)) * jnp.max(jnp.abs(kg))
    thresh = EXP_UNDERFLOW + 2.0 * qk_bound
    skip = f_last[:, :, None, :] > f_first[:, :, :, None] + thresh
    return jnp.sum(skip, axis=-1).astype(jnp.int32).reshape(bsz * HEADS, s // tq)


def _gdnprep_kernel(q_ref, k_ref, v_ref, sm_ref, ex_ref, sel_ref,
                    kbo_ref, vbo_ref, kbeo_ref, qeo_ref, kdo_ref, eglo_ref):
    def spread(group):
        e = ex_ref[group].astype(BF16)
        return _dot(sm2, jnp.concatenate([e, e], axis=0))

    sm = sm_ref[0]
    sm_hi = sm.astype(BF16)
    sm2 = jnp.concatenate([sm_hi, (sm - sm_hi.astype(F32)).astype(BF16)], axis=1)
    beta_x = spread(0)
    egc_x = spread(1)
    edec_x = spread(2)
    eglo_ref[0] = _dot(_dot(sel_ref[...], sm, HIGHEST), ex_ref[3], HIGHEST)

    k = k_ref[0].astype(F32)
    kb = k * beta_x
    kbo_ref[0] = kb.astype(BF16)
    kbeo_ref[0] = (kb * egc_x).astype(BF16)
    kdo_ref[0] = (k * edec_x).astype(BF16)
    qeo_ref[0] = (q_ref[0].astype(F32) * egc_x).astype(BF16)
    vbo_ref[0] = (v_ref[0].astype(F32) * beta_x).astype(BF16)


def _gdnprep(p, sm, expand, sel):
    bsz, s, _ = p.shape
    tb = min(512, s)
    nchunk = tb // CHUNK
    blk = lambda j: pl.BlockSpec((1, tb, WIDTH), lambda b, i: (b, i, j))
    shape = jax.ShapeDtypeStruct((bsz, s, WIDTH), BF16)
    return pl.pallas_call(
        _gdnprep_kernel,
        grid=(bsz, s // tb),
        in_specs=[blk(4), blk(5), blk(6),
                  pl.BlockSpec((1, tb, GATE_COLS), lambda b, i: (b, i, 0)),
                  pl.BlockSpec((4, GATE_COLS, WIDTH), lambda b, i: (0, 0, 0)),
                  pl.BlockSpec((nchunk, tb), lambda b, i: (0, 0))],
        out_specs=[blk(0)] * 5 + [pl.BlockSpec((1, nchunk, WIDTH), lambda b, i: (b, i, 0))],
        out_shape=[shape] * 5 + [jax.ShapeDtypeStruct((bsz, s // CHUNK, WIDTH), F32)],
        compiler_params=_params("arbitrary", "arbitrary"),
        name="gdn_prep",
    )(p, p, p, sm, expand, sel)


def _gdn_kernel(q_ref, k_ref, kb_ref, vb_ref, kbe_ref, ga_ref, gb_ref, qe_ref, kd_ref, egl_ref,
                o_ref, s_ref, u_ref, w_ref, attn_ref, *, nchunk, nblk):
    i = pl.program_id(1)

    @pl.when(i == 0)
    def _():
        s_ref[...] = jnp.zeros_like(s_ref)

    r = lax.broadcasted_iota(jnp.int32, (CHUNK, CHUNK), 0)
    c = lax.broadcasted_iota(jnp.int32, (CHUNK, CHUNK), 1)
    lower = r >= c
    strict = r > c
    eye = (r == c).astype(F32)
    diag8 = (r // 8) == (c // 8)
    levels = [((r // (2 * w)) == (c // (2 * w))) & ((r // w) != (c // w)) for w in (8, 16, 32)]
    gcol = lax.broadcasted_iota(jnp.int32, (CHUNK, GATE_COLS), 1)
    heads = range(HEADS)
    hsl = [slice(h * HEAD_DIM, (h + 1) * HEAD_DIM) for h in heads]
    head_cols = [(gcol == COL_A + h) | (gcol == COL_A2 + h) | (gcol == COL_A3 + h) | (gcol == COL_A4 + h)
                 for h in heads]

    unroll = range(SOLVE_UNROLL)

    def chunk_rows(it, t):
        return pl.ds(pl.multiple_of((it * SOLVE_UNROLL + t) * CHUNK, CHUNK), CHUNK)

    def solve_stages(it):
        probs = [(t, h) for t in unroll for h in heads]
        n = range(len(probs))
        rows = [chunk_rows(it, t) for t in unroll]
        ga = [ga_ref[0, rows[t], :] for t in unroll]
        gb = [gb_ref[0, rows[t], :] for t in unroll]
        gbh = [jnp.where(head_cols[h], gb[t], jnp.zeros((), BF16)) for t, h in probs]
        dlog = [_dot_nt(ga[t], gbh[p]) for p, (t, h) in enumerate(probs)]
        kk = [_dot_nt(kb_ref[0, rows[t], hsl[h]], k_ref[0, rows[t], hsl[h]]) for t, h in probs]
        qk = [_dot_nt(q_ref[0, rows[t], hsl[h]], k_ref[0, rows[t], hsl[h]]) for t, h in probs]
        yield
        decay = [jnp.exp(jnp.where(lower, dlog[p], -jnp.inf)) for p in n]
        m = [jnp.where(strict, kk[p] * decay[p], 0.0) for p in n]
        attn = [(qk[p] * decay[p]).astype(BF16) for p in n]
        md = [jnp.where(diag8, m[p], 0.0).astype(BF16) for p in n]
        m2 = [_dot(md[p], md[p]).astype(BF16) for p in n]
        yield
        inv = [eye - md[p].astype(F32) for p in n]
        m4 = [_dot(m2[p], m2[p]).astype(BF16) for p in n]
        inv = [inv[p] + _dot(inv[p].astype(BF16), m2[p]) for p in n]
        yield
        inv = [inv[p] + _dot(inv[p].astype(BF16), m4[p]) for p in n]
        yield
        for lvl in levels:
            off = [jnp.where(lvl, m[p], 0.0).astype(BF16) for p in n]
            invb = [inv[p].astype(BF16) for p in n]
            x = [_dot(invb[p], off[p]).astype(BF16) for p in n]
            yield
            inv = [inv[p] - _dot(x[p], invb[p]) for p in n]
            yield
        uw = [_dot(inv[p].astype(BF16),
                   jnp.concatenate([vb_ref[0, rows[t], hsl[h]], kbe_ref[0, rows[t], hsl[h]]], axis=-1))
              for p, (t, h) in enumerate(probs)]
        yield
        for p, (t, h) in enumerate(probs):
            attn_ref[rows[t], h * HEAD_DIM:h * HEAD_DIM + CHUNK] = attn[p]
            u_ref[rows[t], hsl[h]] = uw[p][:, :HEAD_DIM]
            w_ref[rows[t], hsl[h]] = uw[p][:, HEAD_DIM:].astype(BF16)

    def scan_stages(it):
        state = [s_ref[h] for h in heads]
        for t in unroll:
            rows = chunk_rows(it, t)
            egl = egl_ref[0, pl.ds(it * SOLVE_UNROLL + t, 1), :]
            u = [u_ref[rows, hsl[h]] for h in heads]
            attn = [attn_ref[rows, h * HEAD_DIM:h * HEAD_DIM + CHUNK] for h in heads]
            ws = [_dot(jnp.concatenate([w_ref[rows, hsl[h]], qe_ref[0, rows, hsl[h]]], axis=0),
                       state[h].astype(BF16)) for h in heads]
            yield
            v_new = [(u[h] - ws[h][:CHUNK]).astype(BF16) for h in heads]
            for h in heads:
                o_ref[0, rows, hsl[h]] = (ws[h][CHUNK:] + _dot(attn[h], v_new[h])).astype(BF16)
            state = [state[h] * egl[:, hsl[h]] + _dot_tn(kd_ref[0, rows, hsl[h]], v_new[h]) for h in heads]
            yield
        for h in heads:
            s_ref[h] = state[h]

    def drain(gen):
        for _ in gen:
            pass

    def solve_only(it, carry):
        drain(solve_stages(it))
        return carry

    def scan_only(it, carry):
        drain(scan_stages(it))
        return carry

    def solve_and_scan(it, carry):
        scan = scan_stages(it)
        solve = solve_stages(it)
        solve_yields = 5 + 2 * len(levels)
        scan_yields = 2 * SOLVE_UNROLL
        for stage in range(solve_yields):
            next(solve)
            if (stage * scan_yields) // solve_yields != ((stage + 1) * scan_yields) // solve_yields:
                next(scan, None)
        drain(scan)
        drain(solve)
        return carry

    iters = nchunk // SOLVE_UNROLL

    @pl.when(i == 0)
    def _():
        lax.fori_loop(0, iters, solve_only, 0)

    @pl.when((i > 0) & (i < nblk))
    def _():
        lax.fori_loop(0, iters, solve_and_scan, 0)

    @pl.when(i == nblk)
    def _():
        lax.fori_loop(0, iters, scan_only, 0)


def _gdn(p, kb, vb, kbe, qe, kd, ga, gb, egl):
    bsz, s, _ = kb.shape
    tb = min(512, s)
    nchunk = tb // CHUNK
    nblk = s // tb
    cur = lambda b, i: (b, jnp.minimum(i, nblk - 1), 0)
    prev = lambda b, i: (b, jnp.maximum(i - 1, 0), 0)
    return pl.pallas_call(
        functools.partial(_gdn_kernel, nchunk=nchunk, nblk=nblk),
        grid=(bsz, nblk + 1),
        in_specs=[pl.BlockSpec((1, tb, WIDTH), lambda b, i: cur(b, i)[:2] + (4,)),
                  pl.BlockSpec((1, tb, WIDTH), lambda b, i: cur(b, i)[:2] + (5,))]
                 + [pl.BlockSpec((1, tb, WIDTH), cur)] * 3 + [pl.BlockSpec((1, tb, GATE_COLS), cur)] * 2
                 + [pl.BlockSpec((1, tb, WIDTH), prev)] * 2 + [pl.BlockSpec((1, nchunk, WIDTH), prev)],
        out_specs=pl.BlockSpec((1, tb, WIDTH), prev),
        out_shape=jax.ShapeDtypeStruct((bsz, s, WIDTH), BF16),
        scratch_shapes=[pltpu.VMEM((HEADS, HEAD_DIM, HEAD_DIM), F32),
                        pltpu.VMEM((tb, WIDTH), F32), pltpu.VMEM((tb, WIDTH), BF16),
                        pltpu.VMEM((tb, WIDTH), BF16)],
        compiler_params=_params("arbitrary", "arbitrary"),
        name="gdn_delta_rule",
    )(p, p, kb, vb, kbe, ga, gb, qe, kd, egl)


def _out_kernel(x_ref, fo_ref, fz_ref, go_ref, gz_ref, gate_ref, gng_ref, w_ref, fg_ref, o_ref, *, final_norm):
    a = fo_ref[0].astype(F32) * _silu(fz_ref[0].astype(F32))
    parts = []
    for h in range(HEADS):
        sl = slice(h * HEAD_DIM, (h + 1) * HEAD_DIM)
        t = go_ref[0, :, sl].astype(F32)
        ms = jnp.mean(t * t, axis=-1, keepdims=True)
        parts.append(t * lax.rsqrt(ms + EPS) * gng_ref[...])
    g = jnp.concatenate(parts, axis=-1) * _silu(gz_ref[0].astype(F32))
    y = _dot(a.astype(BF16), w_ref[0:WIDTH, :]) + _dot(g.astype(BF16), w_ref[WIDTH:2 * WIDTH, :])
    xn = x_ref[0] + gate_ref[0] * y
    if final_norm:
        ms = jnp.mean(xn * xn, axis=-1, keepdims=True)
        xn = xn * lax.rsqrt(ms + EPS) * fg_ref[...]
    o_ref[0] = xn


def _out(x, fox_o, p, gdn_o, gate, gng, w_out, final_g, final_norm):
    bsz, s, d = x.shape
    tm = min(512, s)
    blk = lambda j: pl.BlockSpec((1, tm, WIDTH), lambda b, i: (b, i, j))
    return pl.pallas_call(
        functools.partial(_out_kernel, final_norm=final_norm),
        grid=(bsz, s // tm),
        in_specs=[pl.BlockSpec((1, tm, d), lambda b, i: (b, i, 0)),
                  blk(0), blk(3), blk(0), blk(7),
                  pl.BlockSpec((1, 1, d), lambda b, i: (b, 0, 0)),
                  pl.BlockSpec((1, HEAD_DIM), lambda b, i: (0, 0)),
                  pl.BlockSpec((2 * WIDTH, d), lambda b, i: (0, 0)),
                  pl.BlockSpec((1, d), lambda b, i: (0, 0))],
        out_specs=pl.BlockSpec((1, tm, d), lambda b, i: (b, i, 0)),
        out_shape=jax.ShapeDtypeStruct((bsz, s, d), F32),
        compiler_params=_params("arbitrary", "arbitrary"),
        name="gate_out_proj",
    )(x, fox_o, p, gdn_o, p, gate, gng, w_out, final_g)


def _expand_matrices():
    lane_head = jnp.arange(WIDTH) // HEAD_DIM
    rows = jnp.arange(GATE_COLS)[:, None]
    mats = [(rows == grp + lane_head[None, :]).astype(F32) for grp in (COL_B, COL_A, COL_A2, COL_A3)]
    return jnp.stack(mats)


def _chunk_last_selector(tb):
    nchunk = tb // CHUNK
    return (jnp.arange(tb)[None, :] == (jnp.arange(nchunk)[:, None] * CHUNK + CHUNK - 1)).astype(F32)


def _split_w_in(w):
    fw = WIDTH
    o_ff = 4 * fw
    o_g = o_ff + HEADS
    o_ga = o_g + 4 * fw
    o_gb = o_ga + HEADS
    w_main = jnp.concatenate([w[:, :o_ff], w[:, o_g:o_ga]], axis=1).astype(BF16)
    cols = {"f": w[:, o_ff:o_g], "a": w[:, o_ga:o_gb], "b": w[:, o_gb:o_gb + HEADS]}
    pad = jnp.zeros((w.shape[0], GATE_COLS - len(GATE_GROUPS) * HEADS), w.dtype)
    w_small = jnp.concatenate([cols[t] for t in GATE_GROUPS] + [pad], axis=1).astype(BF16)
    return w_main, w_small


def _gate_rows(b_f, dt_bias, a_log):
    z = jnp.zeros((HEADS,), F32)
    pad = jnp.zeros((GATE_COLS - len(GATE_GROUPS) * HEADS,), F32)
    add = {"f": b_f, "a": dt_bias, "b": z}
    alog = {"f": z, "a": a_log, "b": z}
    add_row = jnp.concatenate([add[t] for t in GATE_GROUPS] + [pad]).reshape(1, GATE_COLS)
    alog_row = jnp.concatenate([alog[t] for t in GATE_GROUPS] + [pad]).reshape(1, GATE_COLS)
    return add_row.astype(F32), alog_row.astype(F32)


def kernel(x, c, norm_g, w_ada, b_ada, w_in, b_fgate, fox_qn_g, fox_kn_g, gdn_conv_w, gdn_A_log,
           gdn_dt_bias, gdn_norm_g, w_out, final_g):
    bsz, s, d = x.shape
    depth = w_in.shape[0]
    expand = _expand_matrices()
    sel = _chunk_last_selector(min(512, s))
    tk = min(512, s // 2)
    for l in range(depth):
        mod = _ada(c, w_ada[l], b_ada[l])
        shift, scale, gate = (mod[:, k * d:(k + 1) * d].reshape(bsz, 1, d) for k in range(3))
        w_main, w_small = _split_w_in(w_in[l])
        qg = fox_qn_g[l].reshape(1, HEAD_DIM)
        kg = fox_kn_g[l].reshape(1, HEAD_DIM)
        p, ps = _proj(x, shift, scale, norm_g[l].reshape(1, d), w_main, w_small,
                      qg * (HEAD_DIM ** -0.5 * LOG2E), kg, gdn_conv_w[l])

        add_row, alog_row = _gate_rows(b_fgate[l], gdn_dt_bias[l], gdn_A_log[l])
        ft, fx, ga, gb, sm = _gates(ps, add_row, alog_row)

        lo = _fox_block_start(ft, qg, kg, tk)
        fox_o = _fox(lo, p, fx, tk)

        gkb, gvb, gkbe, gqe, gkd, egl = _gdnprep(p, sm, expand, sel)
        gdn_o = _gdn(p, gkb, gvb, gkbe, gqe, gkd, ga, gb, egl)

        x = _out(x, fox_o, p, gdn_o, gate, gdn_norm_g[l].reshape(1, HEAD_DIM), w_out[l].astype(BF16),
                 final_g.reshape(1, d), final_norm=(l == depth - 1))
    return x
```

```python
import functools

import jax
import jax.numpy as jnp
from jax import lax
from jax.experimental import pallas as pl
from jax.experimental.pallas import tpu as pltpu

F32 = jnp.float32
BF16 = jnp.bfloat16
HIGHEST = lax.Precision.HIGHEST

HEADS = 8
HEAD_DIM = 128
WIDTH = HEADS * HEAD_DIM
CHUNK = 64
CONV_WIDTH = 4
EPS = 1e-6
GATE_COLS = 128
COL_F, COL_A, COL_B, COL_A2, COL_A3, COL_F2, COL_F3, COL_A4 = 0, 8, 16, 24, 32, 40, 48, 56
GATE_GROUPS = "fabaaffa"
LOG2E = 1.4426950408889634
FOX_UNROLL = 4
SOLVE_UNROLL = 4
EXP_UNDERFLOW = 104.0
VMEM_LIMIT = 52 * 1024 * 1024

NT_DIMS = (((1,), (1,)), ((), ()))
TN_DIMS = (((0,), (0,)), ((), ()))


def _dot(a, b, precision=None):
    return jnp.dot(a, b, preferred_element_type=F32, precision=precision)


def _dot_nt(a, b, precision=None):
    return lax.dot_general(a, b, NT_DIMS, preferred_element_type=F32, precision=precision)


def _dot_tn(a, b):
    return lax.dot_general(a, b, TN_DIMS, preferred_element_type=F32)


def _silu(x):
    return x * jax.nn.sigmoid(x)


def _params(*sem):
    return pltpu.CompilerParams(dimension_semantics=sem, vmem_limit_bytes=VMEM_LIMIT)


def _ada_kernel(c_ref, w_ref, b_ref, o_ref):
    o_ref[...] = _dot(_silu(c_ref[...]), w_ref[...], HIGHEST) + b_ref[...]


def _ada(c, w, b):
    bsz, d = c.shape
    n = w.shape[1]
    rows = 8
    tn = 512 if n % 512 == 0 else 128
    cp =jnp.pad(c, ((0, rows - bsz), (0, 0)))
    out = pl.pallas_call(
        _ada_kernel,
        grid=(n // tn,),
        in_specs=[pl.BlockSpec((rows, d), lambda j: (0, 0)),
                  pl.BlockSpec((d, tn), lambda j: (0, j)),
                  pl.BlockSpec((1, tn), lambda j: (0, j))],
        out_specs=pl.BlockSpec((rows, tn), lambda j: (0, j)),
        out_shape=jax.ShapeDtypeStruct((rows, n), F32),
        compiler_params=_params("arbitrary"),
        name="ada_mod",
    )(cp, w, b.reshape(1, n))
    return out[:bsz]


def _proj_kernel(x_ref, sh_ref, sc_ref, g_ref, wf_ref, wg_ref, ws_ref, p_ref, ps_ref, h_ref, *, tm, rows, fox_tiles):
    j = pl.program_id(2)

    @pl.when(j == 0)
    def _():
        gmul = g_ref[...] * (1.0 + sc_ref[0])
        shift = sh_ref[0]

        def body(r, carry):
            sl = pl.ds(pl.multiple_of(r * rows, rows), rows)
            xs = x_ref[0, sl, :]
            ms = jnp.mean(xs * xs, axis=-1, keepdims=True)
            h_ref[sl, :] = (xs * lax.rsqrt(ms + EPS) * gmul + shift).astype(BF16)
            return carry

        lax.fori_loop(0, tm // rows, body, 0)
        ps_ref[0] = _dot(h_ref[...], ws_ref[...])

    @pl.when(j < fox_tiles)
    def _():
        p_ref[0] = _dot(h_ref[...], wf_ref[...]).astype(BF16)

    @pl.when(j >= fox_tiles)
    def _():
        p_ref[0] = _dot(h_ref[...], wg_ref[...]).astype(BF16)


def _proj(x, shift, scale, g, w_fox, w_gdn, w_small):
    bsz, s, d = x.shape
    tm = min(1024, s)
    tn = WIDTH
    fox_tiles = w_fox.shape[1] // tn
    ntiles = fox_tiles + w_gdn.shape[1] // tn
    kern = functools.partial(_proj_kernel, tm=tm, rows=min(128, tm), fox_tiles=fox_tiles)
    return pl.pallas_call(
        kern,
        grid=(bsz, s // tm, ntiles),
        in_specs=[pl.BlockSpec((1, tm, d), lambda b, i, j: (b, i, 0)),
                  pl.BlockSpec((1, 1, d), lambda b, i, j: (b, 0, 0)),
                  pl.BlockSpec((1, 1, d), lambda b, i, j: (b, 0, 0)),
                  pl.BlockSpec((1, d), lambda b, i, j: (0, 0)),
                  pl.BlockSpec((d, tn), lambda b, i, j: (0, jnp.minimum(j, fox_tiles - 1))),
                  pl.BlockSpec((d, tn), lambda b, i, j: (0, jnp.maximum(j - fox_tiles, 0))),
                  pl.BlockSpec((d, GATE_COLS), lambda b, i, j: (0, 0))],
        out_specs=[pl.BlockSpec((1, tm, tn), lambda b, i, j: (b, i, j)),
                   pl.BlockSpec((1, tm, GATE_COLS), lambda b, i, j: (b, i, 0))],
        out_shape=[jax.ShapeDtypeStruct((bsz, s, ntiles * tn), BF16),
                   jax.ShapeDtypeStruct((bsz, s, GATE_COLS), F32)],
        scratch_shapes=[pltpu.VMEM((tm, d), BF16)],
        compiler_params=_params("arbitrary", "arbitrary", "arbitrary"),
        name="norm_in_proj",
    )(x, shift, scale, g, w_fox, w_gdn, w_small)


def _gates_kernel(ps_ref, add_ref, alog_ref, ft_ref, fx_ref, ga_ref, gb_ref, sm_ref, carry_ref, *, tb):
    @pl.when(pl.program_id(1) == 0)
    def _():
        carry_ref[...] = jnp.zeros_like(carry_ref)

    x = ps_ref[0] + add_ref[...]
    col = lax.broadcasted_iota(jnp.int32, (tb, GATE_COLS), 1)
    grp = lambda start: (col >= start) & (col < start + HEADS)
    is_f = grp(COL_F) | grp(COL_F2) | grp(COL_F3)
    is_a = grp(COL_A) | grp(COL_A2) | grp(COL_A3) | grp(COL_A4)
    log_f = jax.nn.log_sigmoid(x)
    g = -jnp.exp(alog_ref[...]) * jax.nn.softplus(x)
    beta = jax.nn.sigmoid(x)
    vals = jnp.where(is_f, log_f, jnp.where(is_a, g, 0.0))

    r = lax.broadcasted_iota(jnp.int32, (tb, tb), 0)
    c = lax.broadcasted_iota(jnp.int32, (tb, tb), 1)
    same_chunk = (r // CHUNK) == (c // CHUNK)
    sums = jnp.concatenate([(r >= c).astype(F32),
                            ((r >= c) & same_chunk).astype(F32),
                            same_chunk.astype(F32)], axis=0).astype(BF16)

    def split3(t):
        hi = t.astype(BF16).astype(F32)
        mid = (t - hi).astype(BF16).astype(F32)
        return hi, mid, t - hi - mid

    pieces = _dot(sums, jnp.concatenate(split3(vals), axis=1).astype(BF16))
    summed = pieces[:, :GATE_COLS] + pieces[:, GATE_COLS:2 * GATE_COLS] + pieces[:, 2 * GATE_COLS:]
    carry = carry_ref[0:1, :]
    cs_all = summed[0:tb] + carry
    cs = summed[tb:2 * tb]
    tot = summed[2 * tb:]
    carry_ref[...] = jnp.broadcast_to(carry + jnp.sum(vals, axis=0, keepdims=True), carry_ref.shape)

    ft_ref[0] = jnp.transpose(cs_all)[0:HEADS, :]
    f_hi, f_mid, f_lo = split3(cs_all * (-LOG2E))
    fx_ref[0] = jnp.where(grp(COL_F), f_hi, jnp.where(grp(COL_F2), f_mid,
                                                      jnp.where(grp(COL_F3), f_lo, 0.0))).astype(BF16)
    c_hi, c_lo, _ = split3(cs)
    ones = grp(COL_A3) | grp(COL_A4)
    ga_ref[0] = jnp.where(grp(COL_A), c_hi, jnp.where(grp(COL_A2), c_lo, jnp.where(ones, 1.0, 0.0))).astype(BF16)
    gb_ref[0] = jnp.where(grp(COL_A) | grp(COL_A2), 1.0,
                          jnp.where(grp(COL_A3), -c_hi, jnp.where(grp(COL_A4), -c_lo, 0.0))).astype(BF16)
    sm_ref[0] = jnp.where(grp(COL_A), jnp.exp(cs),
                          jnp.where(grp(COL_B), beta,
                                    jnp.where(grp(COL_A2), jnp.exp(tot - cs),
                                              jnp.where(grp(COL_A3), jnp.exp(tot), 0.0))))


def _gates(ps, add_row, alog_row):
    bsz, s, _ = ps.shape
    tb = min(256, s)
    small = pl.BlockSpec((1, tb, GATE_COLS), lambda b, i: (b, i, 0))
    row = pl.BlockSpec((1, GATE_COLS), lambda b, i: (0, 0))
    small_shape = jax.ShapeDtypeStruct((bsz, s, GATE_COLS), F32)
    bf16_shape = jax.ShapeDtypeStruct((bsz, s, GATE_COLS), BF16)
    return pl.pallas_call(
        functools.partial(_gates_kernel, tb=tb),
        grid=(bsz, s // tb),
        in_specs=[small, row, row],
        out_specs=[pl.BlockSpec((1, HEADS, tb), lambda b, i: (b, 0, i)), small, small, small, small],
        out_shape=[jax.ShapeDtypeStruct((bsz, HEADS, s), F32), bf16_shape, bf16_shape, bf16_shape, small_shape],
        scratch_shapes=[pltpu.VMEM((8, GATE_COLS), F32)],
        compiler_params=_params("arbitrary", "arbitrary"),
        name="gates",
    )(ps, add_row, alog_row)


def _foxprep_kernel(q_ref, k_ref, qg_ref, kg_ref, qo_ref, ko_ref):
    def norm(src, gain, dst):
        for h in range(HEADS):
            sl = slice(h * HEAD_DIM, (h + 1) * HEAD_DIM)
            t = src[0, :, sl].astype(F32)
            ms = jnp.mean(t * t, axis=-1, keepdims=True)
            dst[0, :, sl] = (t * lax.rsqrt(ms + EPS) * gain).astype(BF16)

    norm(q_ref, qg_ref[...] * (HEAD_DIM ** -0.5 * LOG2E), qo_ref)
    norm(k_ref, kg_ref[...], ko_ref)


def _foxprep(p, qg, kg):
    bsz, s, _ = p.shape
    tb = min(512, s)
    blk = lambda j: pl.BlockSpec((1, tb, WIDTH), lambda b, i: (b, i, j))
    row = pl.BlockSpec((1, HEAD_DIM), lambda b, i: (0, 0))
    shape = jax.ShapeDtypeStruct((bsz, s, WIDTH), BF16)
    return pl.pallas_call(
        _foxprep_kernel,
        grid=(bsz, s // tb),
        in_specs=[blk(0), blk(1), row, row],
        out_specs=[blk(0), blk(0)],
        out_shape=[shape, shape],
        compiler_params=_params("arbitrary", "arbitrary"),
        name="fox_qk_norm",
    )(p, p, qg, kg)


def _fox_kernel(lo_ref, q_ref, k_ref, v_ref, fx_ref, o_ref, acc_ref, st_ref, *, tk):
    b = pl.program_id(0)
    h = pl.program_id(1)
    g = pl.program_id(2)
    tq = 2 * tk
    lo = lo_ref[b * HEADS + h, 2 * g]
    lane = lax.broadcasted_iota(jnp.int32, (tq, GATE_COLS), 1)
    pick = (lane == COL_F + h) | (lane == COL_F2 + h) | (lane == COL_F3 + h)
    qa = jnp.concatenate([q_ref[0], jnp.where(pick, 1.0, 0.0).astype(BF16)], axis=1)
    acc_ref[...] = jnp.zeros_like(acc_ref)
    late = slice(tk, tq)

    def key_rows(j):
        return pl.ds(pl.multiple_of(j * tk, tk), tk)

    def scores(j, queries=slice(None)):
        ka = jnp.concatenate([k_ref[0, key_rows(j), :], fx_ref[0, key_rows(j), :]], axis=1)
        return _dot_nt(ka, qa[queries])

    def causal(st):
        r = lax.broadcasted_iota(jnp.int32, st.shape, 0)
        c = lax.broadcasted_iota(jnp.int32, st.shape, 1)
        return jnp.where(r <= c, st, -jnp.inf)

    def issue(j, kind, slot):
        if kind == "full":
            st_ref[slot] = scores(j)
        elif kind == "diag":
            st_ref[slot] = causal(scores(j))
        else:
            st_ref[slot, :, late] = causal(scores(j, late))

    def step(cur_j, cur_slot, carry, nxt=None, only_late=False):
        if nxt is not None:
            issue(nxt[0], nxt[1], 1 - cur_slot)
        cols = late if only_late else slice(None)
        m_all, l_all = carry
        m, l = m_all[:, cols], l_all[:, cols]
        st = st_ref[cur_slot, :, cols]
        m_new = jnp.maximum(m, jnp.max(st, axis=0, keepdims=True))
        alpha = jnp.exp2(m - m_new)
        p = jnp.exp2(st - m_new)
        l_new = alpha * l + jnp.sum(p, axis=0, keepdims=True)
        pv = _dot_tn(v_ref[0, key_rows(cur_j), :], p.astype(BF16))
        acc_ref[:, cols] = alpha * acc_ref[:, cols] + pv
        if only_late:
            m_new = jnp.concatenate([m_all[:, :tk], m_new], axis=1)
            l_new = jnp.concatenate([l_all[:, :tk], l_new], axis=1)
        return m_new, l_new

    d = 2 * g
    n = d - lo

    @pl.when(n == 0)
    def _():
        issue(d, "diag", 0)

    @pl.when(n > 0)
    def _():
        issue(lo, "full", 0)

    def interior_steps(t, carry):
        j = lo + FOX_UNROLL * t
        for u in range(FOX_UNROLL):
            carry = step(j + u, u % 2, carry, nxt=(j + u + 1, "full"))
        return carry

    init = (jnp.full((1, tq), -jnp.inf, F32), jnp.zeros((1, tq), F32))
    loops = jnp.maximum(n - 1, 0) // FOX_UNROLL
    carry = lax.fori_loop(0, loops, interior_steps, init)

    def finish(interior_left):
        def run(carry):
            slot = 0
            for back in range(interior_left, 0, -1):
                carry = step(d - back, slot, carry, nxt=(d - back + 1, "full" if back > 1 else "diag"))
                slot = 1 - slot
            carry = step(d, slot, carry, nxt=(d + 1, "late"))
            return step(d + 1, 1 - slot, carry, only_late=True)
        return run

    left = n - FOX_UNROLL * loops
    m, l = lax.switch(left, [finish(k) for k in range(FOX_UNROLL + 1)], carry)
    o_ref[0] = jnp.transpose(acc_ref[...] / l).astype(BF16)


def _fox(lo, qn, kn, p, fx, tk):
    bsz, s, _ = qn.shape
    tq = 2 * tk
    grid_spec = pltpu.PrefetchScalarGridSpec(
        num_scalar_prefetch=1,
        grid=(bsz, HEADS, s // tq),
        in_specs=[pl.BlockSpec((1, tq, HEAD_DIM), lambda b, h, i, lo_r: (b, i, h)),
                  pl.BlockSpec((1, s, HEAD_DIM), lambda b, h, i, lo_r: (b, 0, h)),
                  pl.BlockSpec((1, s, HEAD_DIM), lambda b, h, i, lo_r: (b, 0, 2 * HEADS + h)),
                  pl.BlockSpec((1, s, GATE_COLS), lambda b, h, i, lo_r: (b, 0, 0))],
        out_specs=pl.BlockSpec((1, tq, HEAD_DIM), lambda b, h, i, lo_r: (b, i, h)),
        scratch_shapes=[pltpu.VMEM((HEAD_DIM, tq), F32), pltpu.VMEM((2, tk, tq), F32)],
    )
    return pl.pallas_call(
        functools.partial(_fox_kernel, tk=tk),
        grid_spec=grid_spec,
        out_shape=jax.ShapeDtypeStruct((bsz, s, WIDTH), BF16),
        compiler_params=_params("arbitrary", "arbitrary", "arbitrary"),
        name="fox_attention",
    )(lo, qn, kn, p, fx)


def _fox_block_start(ft, qg, kg, tq):
    bsz, _, s = ft.shape
    f_first = ft[:, :, 0::tq]
    f_last = ft[:, :, tq - 1::tq]
    qk_bound = 1.02 * (HEAD_DIM ** 0.5) * jnp.max(jnp.abs(qg)) * jnp.max(jnp.abs(kg))
    thresh = EXP_UNDERFLOW + 2.0 * qk_bound
    skip = f_last[:, :, None, :] > f_first[:, :, :, None] + thresh
    return jnp.sum(skip, axis=-1).astype(jnp.int32).reshape(bsz * HEADS, s // tq)


def _gdnprep_kernel(q_ref, k_ref, v_ref, qh_ref, kh_ref, vh_ref, cw_ref, sm_ref, ex_ref, sel_ref,
                    qo_ref, ko_ref, kbo_ref, vbo_ref, kbeo_ref, qeo_ref, kdo_ref, eglo_ref, *, tb):
    first = pl.program_id(1) == 0

    def conv_silu(src, halo, which):
        ext = jnp.concatenate([jnp.where(first, 0.0, halo[0].astype(F32)), src[0].astype(F32)], axis=0)
        acc = jnp.zeros((tb, WIDTH), F32)
        for t in range(CONV_WIDTH):
            w_row = cw_ref[t:t + 1, which * WIDTH:(which + 1) * WIDTH]
            back = CONV_WIDTH - 1 - t
            shifted = pltpu.roll(ext, back, axis=0) if back else ext
            acc = acc + shifted[8:, :] * w_row
        return _silu(acc)

    def spread(group):
        e = ex_ref[group].astype(BF16)
        return _dot(sm2, jnp.concatenate([e, e], axis=0))

    def l2norm(t):
        parts = []
        for h in range(HEADS):
            th = t[:, h * HEAD_DIM:(h + 1) * HEAD_DIM]
            parts.append(th * lax.rsqrt(jnp.sum(th * th, axis=-1, keepdims=True) + EPS))
        return jnp.concatenate(parts, axis=-1)

    sm = sm_ref[0]
    sm_hi = sm.astype(BF16)
    sm2 = jnp.concatenate([sm_hi, (sm - sm_hi.astype(F32)).astype(BF16)], axis=1)
    beta_x = spread(0)
    egc_x = spread(1)
    edec_x = spread(2)
    eglo_ref[0] = _dot(_dot(sel_ref[...], sm, HIGHEST), ex_ref[3], HIGHEST)

    k = l2norm(conv_silu(k_ref, kh_ref, 1))
    kb = k * beta_x
    ko_ref[0] = k.astype(BF16)
    kbo_ref[0] = kb.astype(BF16)
    kbeo_ref[0] = (kb * egc_x).astype(BF16)
    kdo_ref[0] = (k * edec_x).astype(BF16)
    q = l2norm(conv_silu(q_ref, qh_ref, 0)) * (HEAD_DIM ** -0.5)
    qo_ref[0] = q.astype(BF16)
    qeo_ref[0] = (q * egc_x).astype(BF16)
    v = conv_silu(v_ref, vh_ref, 2)
    vbo_ref[0] = (v * beta_x).astype(BF16)


def _gdnprep(p, conv_w, sm, expand, sel):
    bsz, s, _ = p.shape
    tb = min(512, s)
    nchunk = tb // CHUNK
    blk = lambda j: pl.BlockSpec((1, tb, WIDTH), lambda b, i: (b, i, j))
    halo = lambda j: pl.BlockSpec((1, 8, WIDTH), lambda b, i: (b, jnp.maximum(i * (tb // 8) - 1, 0), j))
    shape = jax.ShapeDtypeStruct((bsz, s, WIDTH), BF16)
    return pl.pallas_call(
        functools.partial(_gdnprep_kernel, tb=tb),
        grid=(bsz, s // tb),
        in_specs=[blk(4), blk(5), blk(6), halo(4), halo(5), halo(6),
                  pl.BlockSpec((CONV_WIDTH, 3 * WIDTH), lambda b, i: (0, 0)),
                  pl.BlockSpec((1, tb, GATE_COLS), lambda b, i: (b, i, 0)),
                  pl.BlockSpec((4, GATE_COLS, WIDTH), lambda b, i: (0, 0, 0)),
                  pl.BlockSpec((nchunk, tb), lambda b, i: (0, 0))],
        out_specs=[blk(0)] * 7 + [pl.BlockSpec((1, nchunk, WIDTH), lambda b, i: (b, i, 0))],
        out_shape=[shape] * 7 + [jax.ShapeDtypeStruct((bsz, s // CHUNK, WIDTH), F32)],
        compiler_params=_params("arbitrary", "arbitrary"),
        name="gdn_prep",
    )(p, p, p, p, p, p, conv_w, sm, expand, sel)


def _gdn_kernel(q_ref, k_ref, kb_ref, vb_ref, kbe_ref, ga_ref, gb_ref, qe_ref, kd_ref, egl_ref,
                o_ref, s_ref, u_ref, w_ref, attn_ref, *, nchunk, nblk):
    i = pl.program_id(1)

    @pl.when(i == 0)
    def _():
        s_ref[...] = jnp.zeros_like(s_ref)

    r = lax.broadcasted_iota(jnp.int32, (CHUNK, CHUNK), 0)
    c = lax.broadcasted_iota(jnp.int32, (CHUNK, CHUNK), 1)
    lower = r >= c
    strict = r > c
    eye = (r == c).astype(F32)
    diag8 = (r // 8) == (c // 8)
    levels = [((r // (2 * w)) == (c // (2 * w))) & ((r // w) != (c // w)) for w in (8, 16, 32)]
    gcol = lax.broadcasted_iota(jnp.int32, (CHUNK, GATE_COLS), 1)
    heads = range(HEADS)
    hsl = [slice(h * HEAD_DIM, (h + 1) * HEAD_DIM) for h in heads]
    head_cols = [(gcol == COL_A + h) | (gcol == COL_A2 + h) | (gcol == COL_A3 + h) | (gcol == COL_A4 + h)
                 for h in heads]

    unroll = range(SOLVE_UNROLL)

    def chunk_rows(it, t):
        return pl.ds(pl.multiple_of((it * SOLVE_UNROLL + t) * CHUNK, CHUNK), CHUNK)

    def solve_stages(it):
        probs = [(t, h) for t in unroll for h in heads]
        n = range(len(probs))
        rows = [chunk_rows(it, t) for t in unroll]
        ga = [ga_ref[0, rows[t], :] for t in unroll]
        gb = [gb_ref[0, rows[t], :] for t in unroll]
        gbh = [jnp.where(head_cols[h], gb[t], jnp.zeros((), BF16)) for t, h in probs]
        dlog = [_dot_nt(ga[t], gbh[p]) for p, (t, h) in enumerate(probs)]
        kk = [_dot_nt(kb_ref[0, rows[t], hsl[h]], k_ref[0, rows[t], hsl[h]]) for t, h in probs]
        qk = [_dot_nt(q_ref[0, rows[t], hsl[h]], k_ref[0, rows[t], hsl[h]]) for t, h in probs]
        yield
        decay = [jnp.exp(jnp.where(lower, dlog[p], -jnp.inf)) for p in n]
        m = [jnp.where(strict, kk[p] * decay[p], 0.0) for p in n]
        attn = [(qk[p] * decay[p]).astype(BF16) for p in n]
        md = [jnp.where(diag8, m[p], 0.0).astype(BF16) for p in n]
        m2 = [_dot(md[p], md[p]).astype(BF16) for p in n]
        yield
        inv = [eye - md[p].astype(F32) for p in n]
        m4 = [_dot(m2[p], m2[p]).astype(BF16) for p in n]
        inv = [inv[p] + _dot(inv[p].astype(BF16), m2[p]) for p in n]
        yield
        inv = [inv[p] + _dot(inv[p].astype(BF16), m4[p]) for p in n]
        yield
        for lvl in levels:
            off = [jnp.where(lvl, m[p], 0.0).astype(BF16) for p in n]
            invb = [inv[p].astype(BF16) for p in n]
            x = [_dot(invb[p], off[p]).astype(BF16) for p in n]
            yield
            inv = [inv[p] - _dot(x[p], invb[p]) for p in n]
            yield
        uw = [_dot(inv[p].astype(BF16),
                   jnp.concatenate([vb_ref[0, rows[t], hsl[h]], kbe_ref[0, rows[t], hsl[h]]], axis=-1))
              for p, (t, h) in enumerate(probs)]
        yield
        for p, (t, h) in enumerate(probs):
            attn_ref[rows[t], h * HEAD_DIM:h * HEAD_DIM + CHUNK] = attn[p]
            u_ref[rows[t], hsl[h]] = uw[p][:, :HEAD_DIM]
            w_ref[rows[t], hsl[h]] = uw[p][:, HEAD_DIM:].astype(BF16)

    def scan_stages(it):
        state = [s_ref[h] for h in heads]
        for t in unroll:
            rows = chunk_rows(it, t)
            egl = egl_ref[0, pl.ds(it * SOLVE_UNROLL + t, 1), :]
            u = [u_ref[rows, hsl[h]] for h in heads]
            attn = [attn_ref[rows, h * HEAD_DIM:h * HEAD_DIM + CHUNK] for h in heads]
            ws = [_dot(jnp.concatenate([w_ref[rows, hsl[h]], qe_ref[0, rows, hsl[h]]], axis=0),
                       state[h].astype(BF16)) for h in heads]
            yield
            v_new = [(u[h] - ws[h][:CHUNK]).astype(BF16) for h in heads]
            for h in heads:
                o_ref[0, rows, hsl[h]] = (ws[h][CHUNK:] + _dot(attn[h], v_new[h])).astype(BF16)
            state = [state[h] * egl[:, hsl[h]] + _dot_tn(kd_ref[0, rows, hsl[h]], v_new[h]) for h in heads]
            yield
        for h in heads:
            s_ref[h] = state[h]

    def drain(gen):
        for _ in gen:
            pass

    def solve_only(it, carry):
        drain(solve_stages(it))
        return carry

    def scan_only(it, carry):
        drain(scan_stages(it))
        return carry

    def solve_and_scan(it, carry):
        scan = scan_stages(it)
        solve = solve_stages(it)
        solve_yields = 5 + 2 * len(levels)
        scan_yields = 2 * SOLVE_UNROLL
        for stage in range(solve_yields):
            next(solve)
            if (stage * scan_yields) // solve_yields != ((stage + 1) * scan_yields) // solve_yields:
                next(scan, None)
        drain(scan)
        drain(solve)
        return carry

    iters = nchunk // SOLVE_UNROLL

    @pl.when(i == 0)
    def _():
        lax.fori_loop(0, iters, solve_only, 0)

    @pl.when((i > 0) & (i < nblk))
    def _():
        lax.fori_loop(0, iters, solve_and_scan, 0)

    @pl.when(i == nblk)
    def _():
        lax.fori_loop(0, iters, scan_only, 0)


def _gdn(q, k, kb, vb, kbe, qe, kd, ga, gb, egl):
    bsz, s, _ = q.shape
    tb = min(512, s)
    nchunk = tb // CHUNK
    nblk = s // tb
    cur = lambda b, i: (b, jnp.minimum(i, nblk - 1), 0)
    prev = lambda b, i: (b, jnp.maximum(i - 1, 0), 0)
    return pl.pallas_call(
        functools.partial(_gdn_kernel, nchunk=nchunk, nblk=nblk),
        grid=(bsz, nblk + 1),
        in_specs=[pl.BlockSpec((1, tb, WIDTH), cur)] * 5 + [pl.BlockSpec((1, tb, GATE_COLS), cur)] * 2
                 + [pl.BlockSpec((1, tb, WIDTH), prev)] * 2 + [pl.BlockSpec((1, nchunk, WIDTH), prev)],
        out_specs=pl.BlockSpec((1, tb, WIDTH), prev),
        out_shape=jax.ShapeDtypeStruct((bsz, s, WIDTH), BF16),
        scratch_shapes=[pltpu.VMEM((HEADS, HEAD_DIM, HEAD_DIM), F32),
                        pltpu.VMEM((tb, WIDTH), F32), pltpu.VMEM((tb, WIDTH), BF16),
                        pltpu.VMEM((tb, WIDTH), BF16)],
        compiler_params=_params("arbitrary", "arbitrary"),
        name="gdn_delta_rule",
    )(q, k, kb, vb, kbe, ga, gb, qe, kd, egl)


def _out_kernel(x_ref, fo_ref, fz_ref, go_ref, gz_ref, gate_ref, gng_ref, w_ref, fg_ref, o_ref, *, final_norm):
    a = fo_ref[0].astype(F32) * _silu(fz_ref[0].astype(F32))
    parts = []
    for h in range(HEADS):
        sl = slice(h * HEAD_DIM, (h + 1) * HEAD_DIM)
        t = go_ref[0, :, sl].astype(F32)
        ms = jnp.mean(t * t, axis=-1, keepdims=True)
        parts.append(t * lax.rsqrt(ms + EPS) * gng_ref[...])
    g = jnp.concatenate(parts, axis=-1) * _silu(gz_ref[0].astype(F32))
    y = _dot(a.astype(BF16), w_ref[0:WIDTH, :]) + _dot(g.astype(BF16), w_ref[WIDTH:2 * WIDTH, :])
    xn = x_ref[0] + gate_ref[0] * y
    if final_norm:
        ms = jnp.mean(xn * xn, axis=-1, keepdims=True)
        xn = xn * lax.rsqrt(ms + EPS) * fg_ref[...]
    o_ref[0] = xn


def _out(x, fox_o, p, gdn_o, gate, gng, w_out, final_g, final_norm):
    bsz, s, d = x.shape
    tm = min(512, s)
    blk = lambda j: pl.BlockSpec((1, tm, WIDTH), lambda b, i: (b, i, j))
    return pl.pallas_call(
        functools.partial(_out_kernel, final_norm=final_norm),
        grid=(bsz, s // tm),
        in_specs=[pl.BlockSpec((1, tm, d), lambda b, i: (b, i, 0)),
                  blk(0), blk(3), blk(0), blk(7),
                  pl.BlockSpec((1, 1, d), lambda b, i: (b, 0, 0)),
                  pl.BlockSpec((1, HEAD_DIM), lambda b, i: (0, 0)),
                  pl.BlockSpec((2 * WIDTH, d), lambda b, i: (0, 0)),
                  pl.BlockSpec((1, d), lambda b, i: (0, 0))],
        out_specs=pl.BlockSpec((1, tm, d), lambda b, i: (b, i, 0)),
        out_shape=jax.ShapeDtypeStruct((bsz, s, d), F32),
        compiler_params=_params("arbitrary", "arbitrary"),
        name="gate_out_proj",
    )(x, fox_o, p, gdn_o, p, gate, gng, w_out, final_g)


def _expand_matrices():
    lane_head = jnp.arange(WIDTH) // HEAD_DIM
    rows = jnp.arange(GATE_COLS)[:, None]
    mats = [(rows == grp + lane_head[None, :]).astype(F32) for grp in (COL_B, COL_A, COL_A2, COL_A3)]
    return jnp.stack(mats)


def _chunk_last_selector(tb):
    nchunk = tb // CHUNK
    return (jnp.arange(tb)[None, :] == (jnp.arange(nchunk)[:, None] * CHUNK + CHUNK - 1)).astype(F32)


def _split_w_in(w):
    fw = WIDTH
    o_ff = 4 * fw
    o_g = o_ff + HEADS
    o_ga = o_g + 4 * fw
    o_gb = o_ga + HEADS
    w_fox = w[:, :o_ff].astype(BF16)
    w_gdn = w[:, o_g:o_ga].astype(BF16)
    cols = {"f": w[:, o_ff:o_g], "a": w[:, o_ga:o_gb], "b": w[:, o_gb:o_gb + HEADS]}
    pad = jnp.zeros((w.shape[0], GATE_COLS - len(GATE_GROUPS) * HEADS), w.dtype)
    w_small = jnp.concatenate([cols[t] for t in GATE_GROUPS] + [pad], axis=1).astype(BF16)
    return w_fox, w_gdn, w_small


def _gate_rows(b_f, dt_bias, a_log):
    z = jnp.zeros((HEADS,), F32)
    pad = jnp.zeros((GATE_COLS - len(GATE_GROUPS) * HEADS,), F32)
    add = {"f": b_f, "a": dt_bias, "b": z}
    alog = {"f": z, "a": a_log, "b": z}
    add_row = jnp.concatenate([add[t] for t in GATE_GROUPS] + [pad]).reshape(1, GATE_COLS)
    alog_row = jnp.concatenate([alog[t] for t in GATE_GROUPS] + [pad]).reshape(1, GATE_COLS)
    return add_row.astype(F32), alog_row.astype(F32)


def kernel(x, c, norm_g, w_ada, b_ada, w_in, b_fgate, fox_qn_g, fox_kn_g, gdn_conv_w, gdn_A_log,
           gdn_dt_bias, gdn_norm_g, w_out, final_g):
    bsz, s, d = x.shape
    depth = w_in.shape[0]
    expand = _expand_matrices()
    sel = _chunk_last_selector(min(512, s))
    tk = min(512, s // 2)
    for l in range(depth):
        mod = _ada(c, w_ada[l], b_ada[l])
        shift, scale, gate = (mod[:, k * d:(k + 1) * d].reshape(bsz, 1, d) for k in range(3))
        w_fox, w_gdn, w_small = _split_w_in(w_in[l])
        p, ps = _proj(x, shift, scale, norm_g[l].reshape(1, d), w_fox, w_gdn, w_small)

        add_row, alog_row = _gate_rows(b_fgate[l], gdn_dt_bias[l], gdn_A_log[l])
        ft, fx, ga, gb, sm = _gates(ps, add_row, alog_row)

        qg = fox_qn_g[l].reshape(1, HEAD_DIM)
        kg = fox_kn_g[l].reshape(1, HEAD_DIM)
        qn, kn = _foxprep(p, qg, kg)
        lo = _fox_block_start(ft, qg, kg, tk)
        fox_o = _fox(lo, qn, kn, p, fx, tk)

        gq, gk, gkb, gvb, gkbe, gqe, gkd, egl = _gdnprep(p, gdn_conv_w[l], sm, expand, sel)
        gdn_o = _gdn(gq, gk, gkb, gvb, gkbe, gqe, gkd, ga, gb, egl)

        x = _out(x, fox_o, p, gdn_o, gate, gdn_norm_g[l].reshape(1, HEAD_DIM), w_out[l].astype(BF16),
                 final_g.reshape(1, d), final_norm=(l == depth - 1))
    return x
```

```python
import functools

import jax
import jax.numpy as jnp
from jax import lax
from jax.experimental import pallas as pl
from jax.experimental.pallas import tpu as pltpu

F32 = jnp.float32
BF16 = jnp.bfloat16
HIGHEST = lax.Precision.HIGHEST

HEADS = 8
HEAD_DIM = 128
WIDTH = HEADS * HEAD_DIM
CHUNK = 64
NEUMANN_BLOCK = 8
CONV_WIDTH = 4
EPS = 1e-6
GATE_COLS = 128
COL_F, COL_A, COL_B, COL_A2, COL_A3, COL_F2, COL_F3, COL_A4 = 0, 8, 16, 24, 32, 40, 48, 56
GATE_GROUPS = "fabaaffa"
LOG2E = 1.4426950408889634
FOX_UNROLL = 4
SOLVE_UNROLL = 4
EXP_UNDERFLOW = 104.0

SUBLANES = 8
PROJ_ROWS = 1024
PROJ_COLS = WIDTH
NORM_ROWS = 128
ADA_COLS = 512
GATE_ROWS = 256
ROW_BLOCK = 512
FOX_KEYS = 512
VMEM_LIMIT = 52 * 1024 * 1024

NT_DIMS = (((1,), (1,)), ((), ()))
TN_DIMS = (((0,), (0,)), ((), ()))


def _dot(a, b, precision=None):
    return jnp.dot(a, b, preferred_element_type=F32, precision=precision)


def _dot_nt(a, b, precision=None):
    return lax.dot_general(a, b, NT_DIMS, preferred_element_type=F32, precision=precision)


def _dot_tn(a, b):
    return lax.dot_general(a, b, TN_DIMS, preferred_element_type=F32)


def _silu(x):
    return x * jax.nn.sigmoid(x)


def _params(*sem):
    return pltpu.CompilerParams(dimension_semantics=sem, vmem_limit_bytes=VMEM_LIMIT)


def _ada_kernel(c_ref, w_ref, b_ref, o_ref):
    o_ref[...] = _dot(_silu(c_ref[...]), w_ref[...], HIGHEST) + b_ref[...]


def _ada(c, w, b):
    bsz, d = c.shape
    n = w.shape[1]
    rows = SUBLANES
    tn = ADA_COLS if n % ADA_COLS == 0 else HEAD_DIM
    cp = jnp.pad(c, ((0, rows - bsz), (0, 0)))
    out = pl.pallas_call(
        _ada_kernel,
        grid=(n // tn,),
        in_specs=[pl.BlockSpec((rows, d), lambda j: (0, 0)),
                  pl.BlockSpec((d, tn), lambda j: (0, j)),
                  pl.BlockSpec((1, tn), lambda j: (0, j))],
        out_specs=pl.BlockSpec((rows, tn), lambda j: (0, j)),
        out_shape=jax.ShapeDtypeStruct((rows, n), F32),
        compiler_params=_params("arbitrary"),
        name="ada_mod",
    )(cp, w, b.reshape(1, n))
    return out[:bsz]


def _proj_kernel(x_ref, sh_ref, sc_ref, g_ref, w_ref, ws_ref, p_ref, ps_ref, h_ref, *, tm, rows):
    @pl.when(pl.program_id(2) == 0)
    def _():
        gmul = g_ref[...] * (1.0 + sc_ref[0])
        shift = sh_ref[0]

        def body(r, carry):
            sl = pl.ds(pl.multiple_of(r * rows, rows), rows)
            xs = x_ref[0, sl, :]
            ms = jnp.mean(xs * xs, axis=-1, keepdims=True)
            h_ref[sl, :] = (xs * lax.rsqrt(ms + EPS) * gmul + shift).astype(BF16)
            return carry

        lax.fori_loop(0, tm // rows, body, 0)
        ps_ref[0] = _dot(h_ref[...], ws_ref[...])

    p_ref[0] = _dot(h_ref[...], w_ref[...]).astype(BF16)


def _proj(x, shift, scale, g, w_main, w_small):
    bsz, s, d = x.shape
    n = w_main.shape[1]
    tm = min(PROJ_ROWS, s)
    tn = PROJ_COLS
    kern = functools.partial(_proj_kernel, tm=tm, rows=min(NORM_ROWS, tm))
    return pl.pallas_call(
        kern,
        grid=(bsz, s // tm, n // tn),
        in_specs=[pl.BlockSpec((1, tm, d), lambda b, i, j: (b, i, 0)),
                  pl.BlockSpec((1, 1, d), lambda b, i, j: (b, 0, 0)),
                  pl.BlockSpec((1, 1, d), lambda b, i, j: (b, 0, 0)),
                  pl.BlockSpec((1, d), lambda b, i, j: (0, 0)),
                  pl.BlockSpec((d, tn), lambda b, i, j: (0, j)),
                  pl.BlockSpec((d, GATE_COLS), lambda b, i, j: (0, 0))],
        out_specs=[pl.BlockSpec((1, tm, tn), lambda b, i, j: (b, i, j)),
                   pl.BlockSpec((1, tm, GATE_COLS), lambda b, i, j: (b, i, 0))],
        out_shape=[jax.ShapeDtypeStruct((bsz, s, n), BF16),
                   jax.ShapeDtypeStruct((bsz, s, GATE_COLS), F32)],
        scratch_shapes=[pltpu.VMEM((tm, d), BF16)],
        compiler_params=_params("arbitrary", "arbitrary", "arbitrary"),
        name="norm_in_proj",
    )(x, shift, scale, g, w_main, w_small)


def _gates_kernel(ps_ref, add_ref, alog_ref, ft_ref, fx_ref, ga_ref, gb_ref, sm_ref, carry_ref, *, tb):
    @pl.when(pl.program_id(1) == 0)
    def _():
        carry_ref[...] = jnp.zeros_like(carry_ref)

    x = ps_ref[0] + add_ref[...]
    col = lax.broadcasted_iota(jnp.int32, (tb, GATE_COLS), 1)
    grp = lambda start: (col >= start) & (col < start + HEADS)
    is_f = grp(COL_F) | grp(COL_F2) | grp(COL_F3)
    is_a = grp(COL_A) | grp(COL_A2) | grp(COL_A3) | grp(COL_A4)
    log_f = jax.nn.log_sigmoid(x)
    g = -jnp.exp(alog_ref[...]) * jax.nn.softplus(x)
    beta = jax.nn.sigmoid(x)
    vals = jnp.where(is_f, log_f, jnp.where(is_a, g, 0.0))

    r = lax.broadcasted_iota(jnp.int32, (tb, tb), 0)
    c = lax.broadcasted_iota(jnp.int32, (tb, tb), 1)
    same_chunk = (r // CHUNK) == (c // CHUNK)
    sums = jnp.concatenate([(r >= c).astype(F32),
                            ((r >= c) & same_chunk).astype(F32),
                            same_chunk.astype(F32)], axis=0).astype(BF16)

    def split3(t):
        hi = t.astype(BF16).astype(F32)
        mid = (t - hi).astype(BF16).astype(F32)
        return hi, mid, t - hi - mid

    pieces = _dot(sums, jnp.concatenate(split3(vals), axis=1).astype(BF16))
    summed = pieces[:, :GATE_COLS] + pieces[:, GATE_COLS:2 * GATE_COLS] + pieces[:, 2 * GATE_COLS:]
    carry = carry_ref[0:1, :]
    cs_all = summed[0:tb] + carry
    cs = summed[tb:2 * tb]
    tot = summed[2 * tb:]
    carry_ref[...] = jnp.broadcast_to(carry + jnp.sum(vals, axis=0, keepdims=True), carry_ref.shape)

    ft_ref[0] = jnp.transpose(cs_all)[0:HEADS, :]
    f_hi, f_mid, f_lo = split3(cs_all * (-LOG2E))
    fx_ref[0] = jnp.where(grp(COL_F), f_hi, jnp.where(grp(COL_F2), f_mid,
                                                      jnp.where(grp(COL_F3), f_lo, 0.0))).astype(BF16)
    c_hi, c_lo, _ = split3(cs)
    ones = grp(COL_A3) | grp(COL_A4)
    ga_ref[0] = jnp.where(grp(COL_A), c_hi, jnp.where(grp(COL_A2), c_lo, jnp.where(ones, 1.0, 0.0))).astype(BF16)
    gb_ref[0] = jnp.where(grp(COL_A) | grp(COL_A2), 1.0,
                          jnp.where(grp(COL_A3), -c_hi, jnp.where(grp(COL_A4), -c_lo, 0.0))).astype(BF16)
    sm_ref[0] = jnp.where(grp(COL_A), jnp.exp(cs),
                          jnp.where(grp(COL_B), beta,
                                    jnp.where(grp(COL_A2), jnp.exp(tot - cs),
                                              jnp.where(grp(COL_A3), jnp.exp(tot), 0.0))))


def _gates(ps, add_row, alog_row):
    bsz, s, _ = ps.shape
    tb = min(GATE_ROWS, s)
    small = pl.BlockSpec((1, tb, GATE_COLS), lambda b, i: (b, i, 0))
    row = pl.BlockSpec((1, GATE_COLS), lambda b, i: (0, 0))
    small_shape = jax.ShapeDtypeStruct((bsz, s, GATE_COLS), F32)
    bf16_shape = jax.ShapeDtypeStruct((bsz, s, GATE_COLS), BF16)
    return pl.pallas_call(
        functools.partial(_gates_kernel, tb=tb),
        grid=(bsz, s // tb),
        in_specs=[small, row, row],
        out_specs=[pl.BlockSpec((1, HEADS, tb), lambda b, i: (b, 0, i)), small, small, small, small],
        out_shape=[jax.ShapeDtypeStruct((bsz, HEADS, s), F32), bf16_shape, bf16_shape, bf16_shape, small_shape],
        scratch_shapes=[pltpu.VMEM((SUBLANES, GATE_COLS), F32)],
        compiler_params=_params("arbitrary", "arbitrary"),
        name="gates",
    )(ps, add_row, alog_row)


def _foxprep_kernel(q_ref, k_ref, qg_ref, kg_ref, qo_ref, ko_ref):
    def norm(src, gain, dst):
        for h in range(HEADS):
            sl = slice(h * HEAD_DIM, (h + 1) * HEAD_DIM)
            t = src[0, :, sl].astype(F32)
            ms = jnp.mean(t * t, axis=-1, keepdims=True)
            dst[0, :, sl] = (t * lax.rsqrt(ms + EPS) * gain).astype(BF16)

    norm(q_ref, qg_ref[...] * (HEAD_DIM ** -0.5 * LOG2E), qo_ref)
    norm(k_ref, kg_ref[...], ko_ref)


def _foxprep(p, qg, kg):
    bsz, s, _ = p.shape
    tb = min(ROW_BLOCK, s)
    blk = lambda j: pl.BlockSpec((1, tb, WIDTH), lambda b, i: (b, i, j))
    row = pl.BlockSpec((1, HEAD_DIM), lambda b, i: (0, 0))
    shape = jax.ShapeDtypeStruct((bsz, s, WIDTH), BF16)
    return pl.pallas_call(
        _foxprep_kernel,
        grid=(bsz, s // tb),
        in_specs=[blk(0), blk(1), row, row],
        out_specs=[blk(0), blk(0)],
        out_shape=[shape, shape],
        compiler_params=_params("arbitrary", "arbitrary"),
        name="fox_qk_norm",
    )(p, p, qg, kg)


def _fox_kernel(lo_ref, q_ref, k_ref, v_ref, fx_ref, o_ref, acc_ref, st_ref, *, tk):
    b = pl.program_id(0)
    h = pl.program_id(1)
    g = pl.program_id(2)
    tq = 2 * tk
    lo = lo_ref[b * HEADS + h, 2 * g]
    lane = lax.broadcasted_iota(jnp.int32, (tq, GATE_COLS), 1)
    pick = (lane == COL_F + h) | (lane == COL_F2 + h) | (lane == COL_F3 + h)
    qa = jnp.concatenate([q_ref[0], jnp.where(pick, 1.0, 0.0).astype(BF16)], axis=1)
    acc_ref[...] = jnp.zeros_like(acc_ref)
    late = slice(tk, tq)

    def key_rows(j):
        return pl.ds(pl.multiple_of(j * tk, tk), tk)

    def scores(j, queries=slice(None)):
        ka = jnp.concatenate([k_ref[0, key_rows(j), :], fx_ref[0, key_rows(j), :]], axis=1)
        return _dot_nt(ka, qa[queries])

    def causal(st):
        r = lax.broadcasted_iota(jnp.int32, st.shape, 0)
        c = lax.broadcasted_iota(jnp.int32, st.shape, 1)
        return jnp.where(r <= c, st, -jnp.inf)

    def issue(j, kind, slot):
        if kind == "full":
            st_ref[slot] = scores(j)
        elif kind == "diag":
            st_ref[slot] = causal(scores(j))
        else:
            st_ref[slot, :, late] = causal(scores(j, late))

    def step(cur_j, cur_slot, carry, nxt=None, only_late=False):
        if nxt is not None:
            issue(nxt[0], nxt[1], 1 - cur_slot)
        cols = late if only_late else slice(None)
        m_all, l_all = carry
        m, l = m_all[:, cols], l_all[:, cols]
        st = st_ref[cur_slot, :, cols]
        m_new = jnp.maximum(m, jnp.max(st, axis=0, keepdims=True))
        alpha = jnp.exp2(m - m_new)
        p = jnp.exp2(st - m_new)
        l_new = alpha * l + jnp.sum(p, axis=0, keepdims=True)
        pv = _dot_tn(v_ref[0, key_rows(cur_j), :], p.astype(BF16))
        acc_ref[:, cols] = alpha * acc_ref[:, cols] + pv
        if only_late:
            m_new = jnp.concatenate([m_all[:, :tk], m_new], axis=1)
            l_new = jnp.concatenate([l_all[:, :tk], l_new], axis=1)
        return m_new, l_new

    d = 2 * g
    n = d - lo

    @pl.when(n == 0)
    def _():
        issue(d, "diag", 0)

    @pl.when(n > 0)
    def _():
        issue(lo, "full", 0)

    def interior_steps(t, carry):
        j = lo + FOX_UNROLL * t
        for u in range(FOX_UNROLL):
            carry = step(j + u, u % 2, carry, nxt=(j + u + 1, "full"))
        return carry

    init = (jnp.full((1, tq), -jnp.inf, F32), jnp.zeros((1, tq), F32))
    loops = jnp.maximum(n - 1, 0) // FOX_UNROLL
    carry = lax.fori_loop(0, loops, interior_steps, init)

    def finish(interior_left):
        def run(carry):
            slot = 0
            for back in range(interior_left, 0, -1):
                carry = step(d - back, slot, carry, nxt=(d - back + 1, "full" if back > 1 else "diag"))
                slot = 1 - slot
            carry = step(d, slot, carry, nxt=(d + 1, "late"))
            return step(d + 1, 1 - slot, carry, only_late=True)
        return run

    left = n - FOX_UNROLL * loops
    m, l = lax.switch(left, [finish(k) for k in range(FOX_UNROLL + 1)], carry)
    o_ref[0] = jnp.transpose(acc_ref[...] / l).astype(BF16)


def _fox(lo, qn, kn, p, fx, tk):
    bsz, s, _ = qn.shape
    tq = 2 * tk
    grid_spec = pltpu.PrefetchScalarGridSpec(
        num_scalar_prefetch=1,
        grid=(bsz, HEADS, s // tq),
        in_specs=[pl.BlockSpec((1, tq, HEAD_DIM), lambda b, h, i, lo_r: (b, i, h)),
                  pl.BlockSpec((1, s, HEAD_DIM), lambda b, h, i, lo_r: (b, 0, h)),
                  pl.BlockSpec((1, s, HEAD_DIM), lambda b, h, i, lo_r: (b, 0, 2 * HEADS + h)),
                  pl.BlockSpec((1, s, GATE_COLS), lambda b, h, i, lo_r: (b, 0, 0))],
        out_specs=pl.BlockSpec((1, tq, HEAD_DIM), lambda b, h, i, lo_r: (b, i, h)),
        scratch_shapes=[pltpu.VMEM((HEAD_DIM, tq), F32), pltpu.VMEM((2, tk, tq), F32)],
    )
    return pl.pallas_call(
        functools.partial(_fox_kernel, tk=tk),
        grid_spec=grid_spec,
        out_shape=jax.ShapeDtypeStruct((bsz, s, WIDTH), BF16),
        compiler_params=_params("arbitrary", "arbitrary", "arbitrary"),
        name="fox_attention",
    )(lo, qn, kn, p, fx)


def _fox_block_start(ft, qg, kg, tq):
    bsz, _, s = ft.shape
    f_first = ft[:, :, 0::tq]
    f_last = ft[:, :, tq - 1::tq]
    qk_bound = 1.02 * (HEAD_DIM ** 0.5) * jnp.max(jnp.abs(qg)) * jnp.max(jnp.abs(kg))
    thresh = EXP_UNDERFLOW + 2.0 * qk_bound
    skip = f_last[:, :, None, :] > f_first[:, :, :, None] + thresh
    return jnp.sum(skip, axis=-1).astype(jnp.int32).reshape(bsz * HEADS, s // tq)


def _gdnprep_kernel(q_ref, k_ref, v_ref, qh_ref, kh_ref, vh_ref, cw_ref, sm_ref, ex_ref, sel_ref,
                    qo_ref, ko_ref, kbo_ref, vbo_ref, kbeo_ref, qeo_ref, kdo_ref, eglo_ref, *, tb):
    first = pl.program_id(1) == 0

    def conv_silu(src, halo, which):
        ext = jnp.concatenate([jnp.where(first, 0.0, halo[0].astype(F32)), src[0].astype(F32)], axis=0)
        acc = jnp.zeros((tb, WIDTH), F32)
        for t in range(CONV_WIDTH):
            w_row = cw_ref[t:t + 1, which * WIDTH:(which + 1) * WIDTH]
            back = CONV_WIDTH - 1 - t
            shifted = pltpu.roll(ext, back, axis=0) if back else ext
            acc = acc + shifted[SUBLANES:, :] * w_row
        return _silu(acc)

    def spread(group):
        e = ex_ref[group].astype(BF16)
        return _dot(sm2, jnp.concatenate([e, e], axis=0))

    def l2norm(t):
        parts = []
        for h in range(HEADS):
            th = t[:, h * HEAD_DIM:(h + 1) * HEAD_DIM]
            parts.append(th * lax.rsqrt(jnp.sum(th * th, axis=-1, keepdims=True) + EPS))
        return jnp.concatenate(parts, axis=-1)

    sm = sm_ref[0]
    sm_hi = sm.astype(BF16)
    sm2 = jnp.concatenate([sm_hi, (sm - sm_hi.astype(F32)).astype(BF16)], axis=1)
    beta_x = spread(0)
    egc_x = spread(1)
    edec_x = spread(2)
    eglo_ref[0] = _dot(_dot(sel_ref[...], sm, HIGHEST), ex_ref[3], HIGHEST)

    k = l2norm(conv_silu(k_ref, kh_ref, 1))
    kb = k * beta_x
    ko_ref[0] = k.astype(BF16)
    kbo_ref[0] = kb.astype(BF16)
    kbeo_ref[0] = (kb * egc_x).astype(BF16)
    kdo_ref[0] = (k * edec_x).astype(BF16)
    q = l2norm(conv_silu(q_ref, qh_ref, 0)) * (HEAD_DIM ** -0.5)
    qo_ref[0] = q.astype(BF16)
    qeo_ref[0] = (q * egc_x).astype(BF16)
    v = conv_silu(v_ref, vh_ref, 2)
    vbo_ref[0] = (v * beta_x).astype(BF16)


def _gdnprep(p, conv_w, sm, expand, sel):
    bsz, s, _ = p.shape
    tb = min(ROW_BLOCK, s)
    nchunk = tb // CHUNK
    blk = lambda j: pl.BlockSpec((1, tb, WIDTH), lambda b, i: (b, i, j))
    halo = lambda j: pl.BlockSpec((1, SUBLANES, WIDTH),
                                  lambda b, i: (b, jnp.maximum(i * (tb // SUBLANES) - 1, 0), j))
    shape = jax.ShapeDtypeStruct((bsz, s, WIDTH), BF16)
    return pl.pallas_call(
        functools.partial(_gdnprep_kernel, tb=tb),
        grid=(bsz, s // tb),
        in_specs=[blk(4), blk(5), blk(6), halo(4), halo(5), halo(6),
                  pl.BlockSpec((CONV_WIDTH, 3 * WIDTH), lambda b, i: (0, 0)),
                  pl.BlockSpec((1, tb, GATE_COLS), lambda b, i: (b, i, 0)),
                  pl.BlockSpec((4, GATE_COLS, WIDTH), lambda b, i: (0, 0, 0)),
                  pl.BlockSpec((nchunk, tb), lambda b, i: (0, 0))],
        out_specs=[blk(0)] * 7 + [pl.BlockSpec((1, nchunk, WIDTH), lambda b, i: (b, i, 0))],
        out_shape=[shape] * 7 + [jax.ShapeDtypeStruct((bsz, s // CHUNK, WIDTH), F32)],
        compiler_params=_params("arbitrary", "arbitrary"),
        name="gdn_prep",
    )(p, p, p, p, p, p, conv_w, sm, expand, sel)


def _gdn_kernel(q_ref, k_ref, kb_ref, vb_ref, kbe_ref, ga_ref, gb_ref, qe_ref, kd_ref, egl_ref,
                o_ref, s_ref, u_ref, w_ref, attn_ref, *, nchunk, nblk):
    i = pl.program_id(1)

    @pl.when(i == 0)
    def _():
        s_ref[...] = jnp.zeros_like(s_ref)

    r = lax.broadcasted_iota(jnp.int32, (CHUNK, CHUNK), 0)
    c = lax.broadcasted_iota(jnp.int32, (CHUNK, CHUNK), 1)
    lower = r >= c
    strict = r > c
    eye = (r == c).astype(F32)
    diag8 = (r // NEUMANN_BLOCK) == (c // NEUMANN_BLOCK)
    widths = [NEUMANN_BLOCK << lvl for lvl in range((CHUNK // NEUMANN_BLOCK).bit_length() - 1)]
    levels = [((r // (2 * w)) == (c // (2 * w))) & ((r // w) != (c // w)) for w in widths]
    gcol = lax.broadcasted_iota(jnp.int32, (CHUNK, GATE_COLS), 1)
    heads = range(HEADS)
    hsl = [slice(h * HEAD_DIM, (h + 1) * HEAD_DIM) for h in heads]
    head_cols = [(gcol == COL_A + h) | (gcol == COL_A2 + h) | (gcol == COL_A3 + h) | (gcol == COL_A4 + h)
                 for h in heads]

    unroll = range(SOLVE_UNROLL)

    def chunk_rows(it, t):
        return pl.ds(pl.multiple_of((it * SOLVE_UNROLL + t) * CHUNK, CHUNK), CHUNK)

    def solve_stages(it):
        probs = [(t, h) for t in unroll for h in heads]
        n = range(len(probs))
        rows = [chunk_rows(it, t) for t in unroll]
        ga = [ga_ref[0, rows[t], :] for t in unroll]
        gb = [gb_ref[0, rows[t], :] for t in unroll]
        gbh = [jnp.where(head_cols[h], gb[t], jnp.zeros((), BF16)) for t, h in probs]
        dlog = [_dot_nt(ga[t], gbh[p]) for p, (t, h) in enumerate(probs)]
        kk = [_dot_nt(kb_ref[0, rows[t], hsl[h]], k_ref[0, rows[t], hsl[h]]) for t, h in probs]
        qk = [_dot_nt(q_ref[0, rows[t], hsl[h]], k_ref[0, rows[t], hsl[h]]) for t, h in probs]
        yield
        decay = [jnp.exp(jnp.where(lower, dlog[p], -jnp.inf)) for p in n]
        m = [jnp.where(strict, kk[p] * decay[p], 0.0) for p in n]
        attn = [(qk[p] * decay[p]).astype(BF16) for p in n]
        md = [jnp.where(diag8, m[p], 0.0).astype(BF16) for p in n]
        m2 = [_dot(md[p], md[p]).astype(BF16) for p in n]
        yield
        inv = [eye - md[p].astype(F32) for p in n]
        m4 = [_dot(m2[p], m2[p]).astype(BF16) for p in n]
        inv = [inv[p] + _dot(inv[p].astype(BF16), m2[p]) for p in n]
        yield
        inv = [inv[p] + _dot(inv[p].astype(BF16), m4[p]) for p in n]
        yield
        for lvl in levels:
            off = [jnp.where(lvl, m[p], 0.0).astype(BF16) for p in n]
            invb = [inv[p].astype(BF16) for p in n]
            x = [_dot(invb[p], off[p]).astype(BF16) for p in n]
            yield
            inv = [inv[p] - _dot(x[p], invb[p]) for p in n]
            yield
        uw = [_dot(inv[p].astype(BF16),
                   jnp.concatenate([vb_ref[0, rows[t], hsl[h]], kbe_ref[0, rows[t], hsl[h]]], axis=-1))
              for p, (t, h) in enumerate(probs)]
        yield
        for p, (t, h) in enumerate(probs):
            attn_ref[rows[t], h * HEAD_DIM:h * HEAD_DIM + CHUNK] = attn[p]
            u_ref[rows[t], hsl[h]] = uw[p][:, :HEAD_DIM]
            w_ref[rows[t], hsl[h]] = uw[p][:, HEAD_DIM:].astype(BF16)

    def scan_stages(it):
        state = [s_ref[h] for h in heads]
        for t in unroll:
            rows = chunk_rows(it, t)
            egl = egl_ref[0, pl.ds(it * SOLVE_UNROLL + t, 1), :]
            u = [u_ref[rows, hsl[h]] for h in heads]
            attn = [attn_ref[rows, h * HEAD_DIM:h * HEAD_DIM + CHUNK] for h in heads]
            ws = [_dot(jnp.concatenate([w_ref[rows, hsl[h]], qe_ref[0, rows, hsl[h]]], axis=0),
                       state[h].astype(BF16)) for h in heads]
            yield
            v_new = [(u[h] - ws[h][:CHUNK]).astype(BF16) for h in heads]
            for h in heads:
                o_ref[0, rows, hsl[h]] = (ws[h][CHUNK:] + _dot(attn[h], v_new[h])).astype(BF16)
            state = [state[h] * egl[:, hsl[h]] + _dot_tn(kd_ref[0, rows, hsl[h]], v_new[h]) for h in heads]
            yield
        for h in heads:
            s_ref[h] = state[h]

    def drain(gen):
        for _ in gen:
            pass

    def solve_only(it, carry):
        drain(solve_stages(it))
        return carry

    def scan_only(it, carry):
        drain(scan_stages(it))
        return carry

    def solve_and_scan(it, carry):
        scan = scan_stages(it)
        solve = solve_stages(it)
        solve_yields = 5 + 2 * len(levels)
        scan_yields = 2 * SOLVE_UNROLL
        for stage in range(solve_yields):
            next(solve)
            if (stage * scan_yields) // solve_yields != ((stage + 1) * scan_yields) // solve_yields:
                next(scan, None)
        drain(scan)
        drain(solve)
        return carry

    iters = nchunk // SOLVE_UNROLL

    @pl.when(i == 0)
    def _():
        lax.fori_loop(0, iters, solve_only, 0)

    @pl.when((i > 0) & (i < nblk))
    def _():
        lax.fori_loop(0, iters, solve_and_scan, 0)

    @pl.when(i == nblk)
    def _():
        lax.fori_loop(0, iters, scan_only, 0)


def _gdn(q, k, kb, vb, kbe, qe, kd, ga, gb, egl):
    bsz, s, _ = q.shape
    tb = min(ROW_BLOCK, s)
    nchunk = tb // CHUNK
    nblk = s // tb
    cur = lambda b, i: (b, jnp.minimum(i, nblk - 1), 0)
    prev = lambda b, i: (b, jnp.maximum(i - 1, 0), 0)
    return pl.pallas_call(
        functools.partial(_gdn_kernel, nchunk=nchunk, nblk=nblk),
        grid=(bsz, nblk + 1),
        in_specs=[pl.BlockSpec((1, tb, WIDTH), cur)] * 5 + [pl.BlockSpec((1, tb, GATE_COLS), cur)] * 2
                 + [pl.BlockSpec((1, tb, WIDTH), prev)] * 2 + [pl.BlockSpec((1, nchunk, WIDTH), prev)],
        out_specs=pl.BlockSpec((1, tb, WIDTH), prev),
        out_shape=jax.ShapeDtypeStruct((bsz, s, WIDTH), BF16),
        scratch_shapes=[pltpu.VMEM((HEADS, HEAD_DIM, HEAD_DIM), F32),
                        pltpu.VMEM((tb, WIDTH), F32), pltpu.VMEM((tb, WIDTH), BF16),
                        pltpu.VMEM((tb, WIDTH), BF16)],
        compiler_params=_params("arbitrary", "arbitrary"),
        name="gdn_delta_rule",
    )(q, k, kb, vb, kbe, ga, gb, qe, kd, egl)


def _out_kernel(x_ref, fo_ref, fz_ref, go_ref, gz_ref, gate_ref, gng_ref, w_ref, fg_ref, o_ref, *, final_norm):
    a = fo_ref[0].astype(F32) * _silu(fz_ref[0].astype(F32))
    parts = []
    for h in range(HEADS):
        sl = slice(h * HEAD_DIM, (h + 1) * HEAD_DIM)
        t = go_ref[0, :, sl].astype(F32)
        ms = jnp.mean(t * t, axis=-1, keepdims=True)
        parts.append(t * lax.rsqrt(ms + EPS) * gng_ref[...])
    g = jnp.concatenate(parts, axis=-1) * _silu(gz_ref[0].astype(F32))
    y = _dot(a.astype(BF16), w_ref[0:WIDTH, :]) + _dot(g.astype(BF16), w_ref[WIDTH:2 * WIDTH, :])
    xn = x_ref[0] + gate_ref[0] * y
    if final_norm:
        ms = jnp.mean(xn * xn, axis=-1, keepdims=True)
        xn = xn * lax.rsqrt(ms + EPS) * fg_ref[...]
    o_ref[0] = xn


def _out(x, fox_o, p, gdn_o, gate, gng, w_out, final_g, final_norm):
    bsz, s, d = x.shape
    tm = min(ROW_BLOCK, s)
    blk = lambda j: pl.BlockSpec((1, tm, WIDTH), lambda b, i: (b, i, j))
    return pl.pallas_call(
        functools.partial(_out_kernel, final_norm=final_norm),
        grid=(bsz, s // tm),
        in_specs=[pl.BlockSpec((1, tm, d), lambda b, i: (b, i, 0)),
                  blk(0), blk(3), blk(0), blk(7),
                  pl.BlockSpec((1, 1, d), lambda b, i: (b, 0, 0)),
                  pl.BlockSpec((1, HEAD_DIM), lambda b, i: (0, 0)),
                  pl.BlockSpec((2 * WIDTH, d), lambda b, i: (0, 0)),
                  pl.BlockSpec((1, d), lambda b, i: (0, 0))],
        out_specs=pl.BlockSpec((1, tm, d), lambda b, i: (b, i, 0)),
        out_shape=jax.ShapeDtypeStruct((bsz, s, d), F32),
        compiler_params=_params("arbitrary", "arbitrary"),
        name="gate_out_proj",
    )(x, fox_o, p, gdn_o, p, gate, gng, w_out, final_g)


def _expand_matrices():
    lane_head = jnp.arange(WIDTH) // HEAD_DIM
    rows = jnp.arange(GATE_COLS)[:, None]
    mats = [(rows == grp + lane_head[None, :]).astype(F32) for grp in (COL_B, COL_A, COL_A2, COL_A3)]
    return jnp.stack(mats)


def _chunk_last_selector(tb):
    nchunk = tb // CHUNK
    return (jnp.arange(tb)[None, :] == (jnp.arange(nchunk)[:, None] * CHUNK + CHUNK - 1)).astype(F32)


def _split_w_in(w):
    fw = WIDTH
    o_ff = 4 * fw
    o_g = o_ff + HEADS
    o_ga = o_g + 4 * fw
    o_gb = o_ga + HEADS
    w_main = jnp.concatenate([w[:, :o_ff], w[:, o_g:o_ga]], axis=1).astype(BF16)
    cols = {"f": w[:, o_ff:o_g], "a": w[:, o_ga:o_gb], "b": w[:, o_gb:o_gb + HEADS]}
    pad = jnp.zeros((w.shape[0], GATE_COLS - len(GATE_GROUPS) * HEADS), w.dtype)
    w_small = jnp.concatenate([cols[t] for t in GATE_GROUPS] + [pad], axis=1).astype(BF16)
    return w_main, w_small


def _gate_rows(b_f, dt_bias, a_log):
    z = jnp.zeros((HEADS,), F32)
    pad = jnp.zeros((GATE_COLS - len(GATE_GROUPS) * HEADS,), F32)
    add = {"f": b_f, "a": dt_bias, "b": z}
    alog = {"f": z, "a": a_log, "b": z}
    add_row = jnp.concatenate([add[t] for t in GATE_GROUPS] + [pad]).reshape(1, GATE_COLS)
    alog_row = jnp.concatenate([alog[t] for t in GATE_GROUPS] + [pad]).reshape(1, GATE_COLS)
    return add_row.astype(F32), alog_row.astype(F32)


def kernel(x, c, norm_g, w_ada, b_ada, w_in, b_fgate, fox_qn_g, fox_kn_g, gdn_conv_w, gdn_A_log,
           gdn_dt_bias, gdn_norm_g, w_out, final_g):
    bsz, s, d = x.shape
    depth = w_in.shape[0]
    expand = _expand_matrices()
    sel = _chunk_last_selector(min(ROW_BLOCK, s))
    tk = min(FOX_KEYS, s // 2)
    for l in range(depth):
        mod = _ada(c, w_ada[l], b_ada[l])
        shift, scale, gate = (mod[:, k * d:(k + 1) * d].reshape(bsz, 1, d) for k in range(3))
        w_main, w_small = _split_w_in(w_in[l])
        p, ps = _proj(x, shift, scale, norm_g[l].reshape(1, d), w_main, w_small)

        add_row, alog_row = _gate_rows(b_fgate[l], gdn_dt_bias[l], gdn_A_log[l])
        ft, fx, ga, gb, sm = _gates(ps, add_row, alog_row)

        qg = fox_qn_g[l].reshape(1, HEAD_DIM)
        kg = fox_kn_g[l].reshape(1, HEAD_DIM)
        qn, kn = _foxprep(p, qg, kg)
        lo = _fox_block_start(ft, qg, kg, tk)
        fox_o = _fox(lo, qn, kn, p, fx, tk)

        gq, gk, gkb, gvb, gkbe, gqe, gkd, egl = _gdnprep(p, gdn_conv_w[l], sm, expand, sel)
        gdn_o = _gdn(gq, gk, gkb, gvb, gkbe, gqe, gkd, ga, gb, egl)

        x = _out(x, fox_o, p, gdn_o, gate, gdn_norm_g[l].reshape(1, HEAD_DIM), w_out[l].astype(BF16),
                 final_g.reshape(1, d), final_norm=(l == depth - 1))
    return x
```

```python
import functools

import jax
import jax.numpy as jnp
from jax import lax
from jax.experimental import pallas as pl
from jax.experimental.pallas import tpu as pltpu

F32 = jnp.float32
BF16 = jnp.bfloat16
HIGHEST = lax.Precision.HIGHEST

HEADS = 8
HEAD_DIM = 128
WIDTH = HEADS * HEAD_DIM
CHUNK = 64
NEUMANN_BLOCK = 8
CONV_WIDTH = 4
EPS = 1e-6
GATE_COLS = 128
COL_F, COL_A, COL_B, COL_A2, COL_A3, COL_F2, COL_F3, COL_A4 = 0, 8, 16, 24, 32, 40, 48, 56
GATE_GROUPS = "fabaaffa"
LOG2E = 1.4426950408889634
FOX_UNROLL = 4
SOLVE_UNROLL = 4
EXP_UNDERFLOW = 104.0

SUBLANES = 8
PROJ_ROWS = 1024
PROJ_COLS = WIDTH
NORM_ROWS = 128
ADA_COLS = 512
GATE_ROWS = 256
ROW_BLOCK = 512
FOX_KEYS = 512
VMEM_LIMIT = 52 * 1024 * 1024

NT_DIMS = (((1,), (1,)), ((), ()))
TN_DIMS = (((0,), (0,)), ((), ()))


def _dot(a, b, precision=None):
    return jnp.dot(a, b, preferred_element_type=F32, precision=precision)


def _dot_nt(a, b, precision=None):
    return lax.dot_general(a, b, NT_DIMS, preferred_element_type=F32, precision=precision)


def _dot_tn(a, b):
    return lax.dot_general(a, b, TN_DIMS, preferred_element_type=F32)


def _silu(x):
    return x * jax.nn.sigmoid(x)


def _params(*sem):
    return pltpu.CompilerParams(dimension_semantics=sem, vmem_limit_bytes=VMEM_LIMIT)


def _ada_kernel(c_ref, w_ref, b_ref, o_ref):
    o_ref[...] = _dot(_silu(c_ref[...]), w_ref[...], HIGHEST) + b_ref[...]


def _ada(c, w, b):
    bsz, d = c.shape
    n = w.shape[1]
    rows = SUBLANES
    tn = ADA_COLS if n % ADA_COLS == 0 else HEAD_DIM
    cp = jnp.pad(c, ((0, rows - bsz), (0, 0)))
    out = pl.pallas_call(
        _ada_kernel,
        grid=(n // tn,),
        in_specs=[pl.BlockSpec((rows, d), lambda j: (0, 0)),
                  pl.BlockSpec((d, tn), lambda j: (0, j)),
                  pl.BlockSpec((1, tn), lambda j: (0, j))],
        out_specs=pl.BlockSpec((rows, tn), lambda j: (0, j)),
        out_shape=jax.ShapeDtypeStruct((rows, n), F32),
        compiler_params=_params("arbitrary"),
        name="ada_mod",
    )(cp, w, b.reshape(1, n))
    return out[:bsz]


def _head_rms(t, gain):
    parts = []
    for h in range(HEADS):
        th = t[:, h * HEAD_DIM:(h + 1) * HEAD_DIM]
        parts.append(th * lax.rsqrt(jnp.mean(th * th, axis=-1, keepdims=True) + EPS) * gain)
    return jnp.concatenate(parts, axis=-1)


def _head_l2(t, scale):
    parts = []
    for h in range(HEADS):
        th = t[:, h * HEAD_DIM:(h + 1) * HEAD_DIM]
        parts.append(th * (lax.rsqrt(jnp.sum(th * th, axis=-1, keepdims=True) + EPS) * scale))
    return jnp.concatenate(parts, axis=-1)


def _proj_kernel(x_ref, sh_ref, sc_ref, g_ref, w_ref, ws_ref, qg_ref, kg_ref, cw_ref,
                 p_ref, ps_ref, h_ref, raw_ref, halo_ref, *, tm, rows, ntiles):
    i = pl.program_id(1)
    j = pl.program_id(2)

    @pl.when((i == 0) & (j == 0))
    def _():
        halo_ref[...] = jnp.zeros_like(halo_ref)

    def conv_silu(t, which):
        ext = jnp.concatenate([halo_ref[which], t], axis=0)
        halo_ref[which] = t[tm - SUBLANES:, :]
        acc = jnp.zeros((tm, WIDTH), F32)
        for tap in range(CONV_WIDTH):
            w_row = cw_ref[tap:tap + 1, which * WIDTH:(which + 1) * WIDTH]
            back = CONV_WIDTH - 1 - tap
            shifted = pltpu.roll(ext, back, axis=0) if back else ext
            acc = acc + shifted[SUBLANES:, :] * w_row
        return _silu(acc)

    post = {0: lambda t: _head_rms(t, qg_ref[...]),
            1: lambda t: _head_rms(t, kg_ref[...]),
            4: lambda t: _head_l2(conv_silu(t, 0), HEAD_DIM ** -0.5),
            5: lambda t: _head_l2(conv_silu(t, 1), 1.0),
            6: lambda t: conv_silu(t, 2)}

    def prologue():
        gmul = g_ref[...] * (1.0 + sc_ref[0])
        shift = sh_ref[0]

        def body(r, carry):
            sl = pl.ds(pl.multiple_of(r * rows, rows), rows)
            xs = x_ref[0, sl, :]
            ms = jnp.mean(xs * xs, axis=-1, keepdims=True)
            h_ref[sl, :] = (xs * lax.rsqrt(ms + EPS) * gmul + shift).astype(BF16)
            return carry

        lax.fori_loop(0, tm // rows, body, 0)
        ps_ref[0] = _dot(h_ref[...], ws_ref[...])

    for t in range(ntiles + 1):
        @pl.when(j == t)
        def _(t=t):
            if t == 0:
                prologue()
            if t < ntiles:
                raw_ref[t % 2] = _dot(h_ref[...], w_ref[...])
            if t >= 1:
                raw = raw_ref[(t - 1) % 2]
                p_ref[0] = post.get(t - 1, lambda v: v)(raw).astype(BF16)


def _proj(x, shift, scale, g, w_main, w_small, qg, kg, conv_w):
    bsz, s, d = x.shape
    n = w_main.shape[1]
    tm = min(PROJ_ROWS, s)
    tn = PROJ_COLS
    ntiles = n // tn
    kern = functools.partial(_proj_kernel, tm=tm, rows=min(NORM_ROWS, tm), ntiles=ntiles)
    const = lambda b, i, j: (0, 0)
    return pl.pallas_call(
        kern,
        grid=(bsz, s // tm, ntiles + 1),
        in_specs=[pl.BlockSpec((1, tm, d), lambda b, i, j: (b, i, 0)),
                  pl.BlockSpec((1, 1, d), lambda b, i, j: (b, 0, 0)),
                  pl.BlockSpec((1, 1, d), lambda b, i, j: (b, 0, 0)),
                  pl.BlockSpec((1, d), const),
                  pl.BlockSpec((d, tn), lambda b, i, j: (0, jnp.minimum(j, ntiles - 1))),
                  pl.BlockSpec((d, GATE_COLS), const),
                  pl.BlockSpec((1, HEAD_DIM), const),
                  pl.BlockSpec((1, HEAD_DIM), const),
                  pl.BlockSpec((CONV_WIDTH, 3 * WIDTH), const)],
        out_specs=[pl.BlockSpec((1, tm, tn), lambda b, i, j: (b, i, jnp.maximum(j - 1, 0))),
                   pl.BlockSpec((1, tm, GATE_COLS), lambda b, i, j: (b, i, 0))],
        out_shape=[jax.ShapeDtypeStruct((bsz, s, n), BF16),
                   jax.ShapeDtypeStruct((bsz, s, GATE_COLS), F32)],
        scratch_shapes=[pltpu.VMEM((tm, d), BF16), pltpu.VMEM((2, tm, tn), F32),
                        pltpu.VMEM((3, SUBLANES, tn), F32)],
        compiler_params=_params("arbitrary", "arbitrary", "arbitrary"),
        name="norm_in_proj",
    )(x, shift, scale, g, w_main, w_small, qg, kg, conv_w)


def _gates_kernel(ps_ref, add_ref, alog_ref, ft_ref, fx_ref, ga_ref, gb_ref, sm_ref, carry_ref, *, tb):
    @pl.when(pl.program_id(1) == 0)
    def _():
        carry_ref[...] = jnp.zeros_like(carry_ref)

    x = ps_ref[0] + add_ref[...]
    col = lax.broadcasted_iota(jnp.int32, (tb, GATE_COLS), 1)
    grp = lambda start: (col >= start) & (col < start + HEADS)
    is_f = grp(COL_F) | grp(COL_F2) | grp(COL_F3)
    is_a = grp(COL_A) | grp(COL_A2) | grp(COL_A3) | grp(COL_A4)
    log_f = jax.nn.log_sigmoid(x)
    g = -jnp.exp(alog_ref[...]) * jax.nn.softplus(x)
    beta = jax.nn.sigmoid(x)
    vals = jnp.where(is_f, log_f, jnp.where(is_a, g, 0.0))

    r = lax.broadcasted_iota(jnp.int32, (tb, tb), 0)
    c = lax.broadcasted_iota(jnp.int32, (tb, tb), 1)
    same_chunk = (r // CHUNK) == (c // CHUNK)
    sums = jnp.concatenate([(r >= c).astype(F32),
                            ((r >= c) & same_chunk).astype(F32),
                            same_chunk.astype(F32)], axis=0).astype(BF16)

    def split3(t):
        hi = t.astype(BF16).astype(F32)
        mid = (t - hi).astype(BF16).astype(F32)
        return hi, mid, t - hi - mid

    pieces = _dot(sums, jnp.concatenate(split3(vals), axis=1).astype(BF16))
    summed = pieces[:, :GATE_COLS] + pieces[:, GATE_COLS:2 * GATE_COLS] + pieces[:, 2 * GATE_COLS:]
    carry = carry_ref[0:1, :]
    cs_all = summed[0:tb] + carry
    cs = summed[tb:2 * tb]
    tot = summed[2 * tb:]
    carry_ref[...] = jnp.broadcast_to(carry + jnp.sum(vals, axis=0, keepdims=True), carry_ref.shape)

    ft_ref[0] = jnp.transpose(cs_all)[0:HEADS, :]
    f_hi, f_mid, f_lo = split3(cs_all * (-LOG2E))
    fx_ref[0] = jnp.where(grp(COL_F), f_hi, jnp.where(grp(COL_F2), f_mid,
                                                      jnp.where(grp(COL_F3), f_lo, 0.0))).astype(BF16)
    c_hi, c_lo, _ = split3(cs)
    ones = grp(COL_A3) | grp(COL_A4)
    ga_ref[0] = jnp.where(grp(COL_A), c_hi, jnp.where(grp(COL_A2), c_lo, jnp.where(ones, 1.0, 0.0))).astype(BF16)
    gb_ref[0] = jnp.where(grp(COL_A) | grp(COL_A2), 1.0,
                          jnp.where(grp(COL_A3), -c_hi, jnp.where(grp(COL_A4), -c_lo, 0.0))).astype(BF16)
    sm_ref[0] = jnp.where(grp(COL_A), jnp.exp(cs),
                          jnp.where(grp(COL_B), beta,
                                    jnp.where(grp(COL_A2), jnp.exp(tot - cs),
                                              jnp.where(grp(COL_A3), jnp.exp(tot), 0.0))))


def _gates(ps, add_row, alog_row):
    bsz, s, _ = ps.shape
    tb = min(GATE_ROWS, s)
    small = pl.BlockSpec((1, tb, GATE_COLS), lambda b, i: (b, i, 0))
    row = pl.BlockSpec((1, GATE_COLS), lambda b, i: (0, 0))
    small_shape = jax.ShapeDtypeStruct((bsz, s, GATE_COLS), F32)
    bf16_shape = jax.ShapeDtypeStruct((bsz, s, GATE_COLS), BF16)
    return pl.pallas_call(
        functools.partial(_gates_kernel, tb=tb),
        grid=(bsz, s // tb),
        in_specs=[small, row, row],
        out_specs=[pl.BlockSpec((1, HEADS, tb), lambda b, i: (b, 0, i)), small, small, small, small],
        out_shape=[jax.ShapeDtypeStruct((bsz, HEADS, s), F32), bf16_shape, bf16_shape, bf16_shape, small_shape],
        scratch_shapes=[pltpu.VMEM((SUBLANES, GATE_COLS), F32)],
        compiler_params=_params("arbitrary", "arbitrary"),
        name="gates",
    )(ps, add_row, alog_row)


def _fox_kernel(lo_ref, q_ref, k_ref, v_ref, fx_ref, o_ref, acc_ref, st_ref, *, tk):
    b = pl.program_id(0)
    h = pl.program_id(1)
    g = pl.program_id(2)
    tq = 2 * tk
    lo = lo_ref[b * HEADS + h, 2 * g]
    lane = lax.broadcasted_iota(jnp.int32, (tq, GATE_COLS), 1)
    pick = (lane == COL_F + h) | (lane == COL_F2 + h) | (lane == COL_F3 + h)
    qa = jnp.concatenate([q_ref[0], jnp.where(pick, 1.0, 0.0).astype(BF16)], axis=1)
    acc_ref[...] = jnp.zeros_like(acc_ref)
    late = slice(tk, tq)

    def key_rows(j):
        return pl.ds(pl.multiple_of(j * tk, tk), tk)

    def scores(j, queries=slice(None)):
        ka = jnp.concatenate([k_ref[0, key_rows(j), :], fx_ref[0, key_rows(j), :]], axis=1)
        return _dot_nt(ka, qa[queries])

    def causal(st):
        r = lax.broadcasted_iota(jnp.int32, st.shape, 0)
        c = lax.broadcasted_iota(jnp.int32, st.shape, 1)
        return jnp.where(r <= c, st, -jnp.inf)

    def issue(j, kind, slot):
        if kind == "full":
            st_ref[slot] = scores(j)
        elif kind == "diag":
            st_ref[slot] = causal(scores(j))
        else:
            st_ref[slot, :, late] = causal(scores(j, late))

    def step(cur_j, cur_slot, carry, nxt=None, only_late=False):
        if nxt is not None:
            issue(nxt[0], nxt[1], 1 - cur_slot)
        cols = late if only_late else slice(None)
        m_all, l_all = carry
        m, l = m_all[:, cols], l_all[:, cols]
        st = st_ref[cur_slot, :, cols]
        m_new = jnp.maximum(m, jnp.max(st, axis=0, keepdims=True))
        alpha = jnp.exp2(m - m_new)
        p = jnp.exp2(st - m_new)
        l_new = alpha * l + jnp.sum(p, axis=0, keepdims=True)
        pv = _dot_tn(v_ref[0, key_rows(cur_j), :], p.astype(BF16))
        acc_ref[:, cols] = alpha * acc_ref[:, cols] + pv
        if only_late:
            m_new = jnp.concatenate([m_all[:, :tk], m_new], axis=1)
            l_new = jnp.concatenate([l_all[:, :tk], l_new], axis=1)
        return m_new, l_new

    d = 2 * g
    n = d - lo

    @pl.when(n == 0)
    def _():
        issue(d, "diag", 0)

    @pl.when(n > 0)
    def _():
        issue(lo, "full", 0)

    def interior_steps(t, carry):
        j = lo + FOX_UNROLL * t
        for u in range(FOX_UNROLL):
            carry = step(j + u, u % 2, carry, nxt=(j + u + 1, "full"))
        return carry

    init = (jnp.full((1, tq), -jnp.inf, F32), jnp.zeros((1, tq), F32))
    loops = jnp.maximum(n - 1, 0) // FOX_UNROLL
    carry = lax.fori_loop(0, loops, interior_steps, init)

    def finish(interior_left):
        def run(carry):
            slot = 0
            for back in range(interior_left, 0, -1):
                carry = step(d - back, slot, carry, nxt=(d - back + 1, "full" if back > 1 else "diag"))
                slot = 1 - slot
            carry = step(d, slot, carry, nxt=(d + 1, "late"))
            return step(d + 1, 1 - slot, carry, only_late=True)
        return run

    left = n - FOX_UNROLL * loops
    m, l = lax.switch(left, [finish(k) for k in range(FOX_UNROLL + 1)], carry)
    o_ref[0] = jnp.transpose(acc_ref[...] / l).astype(BF16)


def _fox(lo, p, fx, tk):
    bsz, s, _ = p.shape
    tq = 2 * tk
    grid_spec = pltpu.PrefetchScalarGridSpec(
        num_scalar_prefetch=1,
        grid=(bsz, HEADS, s // tq),
        in_specs=[pl.BlockSpec((1, tq, HEAD_DIM), lambda b, h, i, lo_r: (b, i, h)),
                  pl.BlockSpec((1, s, HEAD_DIM), lambda b, h, i, lo_r: (b, 0, HEADS + h)),
                  pl.BlockSpec((1, s, HEAD_DIM), lambda b, h, i, lo_r: (b, 0, 2 * HEADS + h)),
                  pl.BlockSpec((1, s, GATE_COLS), lambda b, h, i, lo_r: (b, 0, 0))],
        out_specs=pl.BlockSpec((1, tq, HEAD_DIM), lambda b, h, i, lo_r: (b, i, h)),
        scratch_shapes=[pltpu.VMEM((HEAD_DIM, tq), F32), pltpu.VMEM((2, tk, tq), F32)],
    )
    return pl.pallas_call(
        functools.partial(_fox_kernel, tk=tk),
        grid_spec=grid_spec,
        out_shape=jax.ShapeDtypeStruct((bsz, s, WIDTH), BF16),
        compiler_params=_params("arbitrary", "arbitrary", "arbitrary"),
        name="fox_attention",
    )(lo, p, p, p, fx)


def _fox_block_start(ft, qg, kg, tq):
    bsz, _, s = ft.shape
    f_first = ft[:, :, 0::tq]
    f_last = ft[:, :, tq - 1::tq]
    qk_bound = 1.02 * (HEAD_DIM ** 0.5) * jnp.max(jnp.abs(qg)) * jnp.max(jnp.abs(kg))
    thresh = EXP_UNDERFLOW + 2.0 * qk_bound
    skip = f_last[:, :, None, :] > f_first[:, :, :, None] + thresh
    return jnp.sum(skip, axis=-1).astype(jnp.int32).reshape(bsz * HEADS, s // tq)


def _gdnprep_kernel(q_ref, k_ref, v_ref, sm_ref, ex_ref, sel_ref,
                    kbo_ref, vbo_ref, kbeo_ref, qeo_ref, kdo_ref, eglo_ref):
    def spread(group):
        e = ex_ref[group].astype(BF16)
        return _dot(sm2, jnp.concatenate([e, e], axis=0))

    sm = sm_ref[0]
    sm_hi = sm.astype(BF16)
    sm2 = jnp.concatenate([sm_hi, (sm - sm_hi.astype(F32)).astype(BF16)], axis=1)
    beta_x = spread(0)
    egc_x = spread(1)
    edec_x = spread(2)
    eglo_ref[0] = _dot(_dot(sel_ref[...], sm, HIGHEST), ex_ref[3], HIGHEST)

    k = k_ref[0].astype(F32)
    kb = k * beta_x
    kbo_ref[0] = kb.astype(BF16)
    kbeo_ref[0] = (kb * egc_x).astype(BF16)
    kdo_ref[0] = (k * edec_x).astype(BF16)
    qeo_ref[0] = (q_ref[0].astype(F32) * egc_x).astype(BF16)
    vbo_ref[0] = (v_ref[0].astype(F32) * beta_x).astype(BF16)


def _gdnprep(p, sm, expand, sel):
    bsz, s, _ = p.shape
    tb = min(ROW_BLOCK, s)
    nchunk = tb // CHUNK
    blk = lambda j: pl.BlockSpec((1, tb, WIDTH), lambda b, i: (b, i, j))
    shape = jax.ShapeDtypeStruct((bsz, s, WIDTH), BF16)
    return pl.pallas_call(
        _gdnprep_kernel,
        grid=(bsz, s // tb),
        in_specs=[blk(4), blk(5), blk(6),
                  pl.BlockSpec((1, tb, GATE_COLS), lambda b, i: (b, i, 0)),
                  pl.BlockSpec((4, GATE_COLS, WIDTH), lambda b, i: (0, 0, 0)),
                  pl.BlockSpec((nchunk, tb), lambda b, i: (0, 0))],
        out_specs=[blk(0)] * 5 + [pl.BlockSpec((1, nchunk, WIDTH), lambda b, i: (b, i, 0))],
        out_shape=[shape] * 5 + [jax.ShapeDtypeStruct((bsz, s // CHUNK, WIDTH), F32)],
        compiler_params=_params("arbitrary", "arbitrary"),
        name="gdn_prep",
    )(p, p, p, sm, expand, sel)


def _gdn_kernel(q_ref, k_ref, kb_ref, vb_ref, kbe_ref, ga_ref, gb_ref, qe_ref, kd_ref, egl_ref,
                o_ref, s_ref, u_ref, w_ref, attn_ref, *, nchunk, nblk):
    i = pl.program_id(1)

    @pl.when(i == 0)
    def _():
        s_ref[...] = jnp.zeros_like(s_ref)

    r = lax.broadcasted_iota(jnp.int32, (CHUNK, CHUNK), 0)
    c = lax.broadcasted_iota(jnp.int32, (CHUNK, CHUNK), 1)
    lower = r >= c
    strict = r > c
    eye = (r == c).astype(F32)
    diag8 = (r // NEUMANN_BLOCK) == (c // NEUMANN_BLOCK)
    widths = [NEUMANN_BLOCK << lvl for lvl in range((CHUNK // NEUMANN_BLOCK).bit_length() - 1)]
    levels = [((r // (2 * w)) == (c // (2 * w))) & ((r // w) != (c // w)) for w in widths]
    gcol = lax.broadcasted_iota(jnp.int32, (CHUNK, GATE_COLS), 1)
    heads = range(HEADS)
    hsl = [slice(h * HEAD_DIM, (h + 1) * HEAD_DIM) for h in heads]
    head_cols = [(gcol == COL_A + h) | (gcol == COL_A2 + h) | (gcol == COL_A3 + h) | (gcol == COL_A4 + h)
                 for h in heads]

    unroll = range(SOLVE_UNROLL)

    def chunk_rows(it, t):
        return pl.ds(pl.multiple_of((it * SOLVE_UNROLL + t) * CHUNK, CHUNK), CHUNK)

    def solve_stages(it):
        probs = [(t, h) for t in unroll for h in heads]
        n = range(len(probs))
        rows = [chunk_rows(it, t) for t in unroll]
        ga = [ga_ref[0, rows[t], :] for t in unroll]
        gb = [gb_ref[0, rows[t], :] for t in unroll]
        gbh = [jnp.where(head_cols[h], gb[t], jnp.zeros((), BF16)) for t, h in probs]
        dlog = [_dot_nt(ga[t], gbh[p]) for p, (t, h) in enumerate(probs)]
        kk = [_dot_nt(kb_ref[0, rows[t], hsl[h]], k_ref[0, rows[t], hsl[h]]) for t, h in probs]
        qk = [_dot_nt(q_ref[0, rows[t], hsl[h]], k_ref[0, rows[t], hsl[h]]) for t, h in probs]
        yield
        decay = [jnp.exp(jnp.where(lower, dlog[p], -jnp.inf)) for p in n]
        m = [jnp.where(strict, kk[p] * decay[p], 0.0) for p in n]
        attn = [(qk[p] * decay[p]).astype(BF16) for p in n]
        md = [jnp.where(diag8, m[p], 0.0).astype(BF16) for p in n]
        m2 = [_dot(md[p], md[p]).astype(BF16) for p in n]
        yield
        inv = [eye - md[p].astype(F32) for p in n]
        m4 = [_dot(m2[p], m2[p]).astype(BF16) for p in n]
        inv = [inv[p] + _dot(inv[p].astype(BF16), m2[p]) for p in n]
        yield
        inv = [inv[p] + _dot(inv[p].astype(BF16), m4[p]) for p in n]
        yield
        for lvl in levels:
            off = [jnp.where(lvl, m[p], 0.0).astype(BF16) for p in n]
            invb = [inv[p].astype(BF16) for p in n]
            x = [_dot(invb[p], off[p]).astype(BF16) for p in n]
            yield
            inv = [inv[p] - _dot(x[p], invb[p]) for p in n]
            yield
        uw = [_dot(inv[p].astype(BF16),
                   jnp.concatenate([vb_ref[0, rows[t], hsl[h]], kbe_ref[0, rows[t], hsl[h]]], axis=-1))
              for p, (t, h) in enumerate(probs)]
        yield
        for p, (t, h) in enumerate(probs):
            attn_ref[rows[t], h * HEAD_DIM:h * HEAD_DIM + CHUNK] = attn[p]
            u_ref[rows[t], hsl[h]] = uw[p][:, :HEAD_DIM]
            w_ref[rows[t], hsl[h]] = uw[p][:, HEAD_DIM:].astype(BF16)

    def scan_stages(it):
        state = [s_ref[h] for h in heads]
        for t in unroll:
            rows = chunk_rows(it, t)
            egl = egl_ref[0, pl.ds(it * SOLVE_UNROLL + t, 1), :]
            u = [u_ref[rows, hsl[h]] for h in heads]
            attn = [attn_ref[rows, h * HEAD_DIM:h * HEAD_DIM + CHUNK] for h in heads]
            ws = [_dot(jnp.concatenate([w_ref[rows, hsl[h]], qe_ref[0, rows, hsl[h]]], axis=0),
                       state[h].astype(BF16)) for h in heads]
            yield
            v_new = [(u[h] - ws[h][:CHUNK]).astype(BF16) for h in heads]
            for h in heads:
                o_ref[0, rows, hsl[h]] = (ws[h][CHUNK:] + _dot(attn[h], v_new[h])).astype(BF16)
            state = [state[h] * egl[:, hsl[h]] + _dot_tn(kd_ref[0, rows, hsl[h]], v_new[h]) for h in heads]
            yield
        for h in heads:
            s_ref[h] = state[h]

    def drain(gen):
        for _ in gen:
            pass

    def solve_only(it, carry):
        drain(solve_stages(it))
        return carry

    def scan_only(it, carry):
        drain(scan_stages(it))
        return carry

    def solve_and_scan(it, carry):
        scan = scan_stages(it)
        solve = solve_stages(it)
        solve_yields = 5 + 2 * len(levels)
        scan_yields = 2 * SOLVE_UNROLL
        for stage in range(solve_yields):
            next(solve)
            if (stage * scan_yields) // solve_yields != ((stage + 1) * scan_yields) // solve_yields:
                next(scan, None)
        drain(scan)
        drain(solve)
        return carry

    iters = nchunk // SOLVE_UNROLL

    @pl.when(i == 0)
    def _():
        lax.fori_loop(0, iters, solve_only, 0)

    @pl.when((i > 0) & (i < nblk))
    def _():
        lax.fori_loop(0, iters, solve_and_scan, 0)

    @pl.when(i == nblk)
    def _():
        lax.fori_loop(0, iters, scan_only, 0)


def _gdn(p, kb, vb, kbe, qe, kd, ga, gb, egl):
    bsz, s, _ = kb.shape
    tb = min(ROW_BLOCK, s)
    nchunk = tb // CHUNK
    nblk = s // tb
    cur = lambda b, i: (b, jnp.minimum(i, nblk - 1), 0)
    prev = lambda b, i: (b, jnp.maximum(i - 1, 0), 0)
    return pl.pallas_call(
        functools.partial(_gdn_kernel, nchunk=nchunk, nblk=nblk),
        grid=(bsz, nblk + 1),
        in_specs=[pl.BlockSpec((1, tb, WIDTH), lambda b, i: cur(b, i)[:2] + (4,)),
                  pl.BlockSpec((1, tb, WIDTH), lambda b, i: cur(b, i)[:2] + (5,))]
                 + [pl.BlockSpec((1, tb, WIDTH), cur)] * 3 + [pl.BlockSpec((1, tb, GATE_COLS), cur)] * 2
                 + [pl.BlockSpec((1, tb, WIDTH), prev)] * 2 + [pl.BlockSpec((1, nchunk, WIDTH), prev)],
        out_specs=pl.BlockSpec((1, tb, WIDTH), prev),
        out_shape=jax.ShapeDtypeStruct((bsz, s, WIDTH), BF16),
        scratch_shapes=[pltpu.VMEM((HEADS, HEAD_DIM, HEAD_DIM), F32),
                        pltpu.VMEM((tb, WIDTH), F32), pltpu.VMEM((tb, WIDTH), BF16),
                        pltpu.VMEM((tb, WIDTH), BF16)],
        compiler_params=_params("arbitrary", "arbitrary"),
        name="gdn_delta_rule",
    )(p, p, kb, vb, kbe, ga, gb, qe, kd, egl)


def _out_kernel(x_ref, fo_ref, fz_ref, go_ref, gz_ref, gate_ref, gng_ref, w_ref, fg_ref, o_ref, *, final_norm):
    a = fo_ref[0].astype(F32) * _silu(fz_ref[0].astype(F32))
    parts = []
    for h in range(HEADS):
        sl = slice(h * HEAD_DIM, (h + 1) * HEAD_DIM)
        t = go_ref[0, :, sl].astype(F32)
        ms = jnp.mean(t * t, axis=-1, keepdims=True)
        parts.append(t * lax.rsqrt(ms + EPS) * gng_ref[...])
    g = jnp.concatenate(parts, axis=-1) * _silu(gz_ref[0].astype(F32))
    y = _dot(a.astype(BF16), w_ref[0:WIDTH, :]) + _dot(g.astype(BF16), w_ref[WIDTH:2 * WIDTH, :])
    xn = x_ref[0] + gate_ref[0] * y
    if final_norm:
        ms = jnp.mean(xn * xn, axis=-1, keepdims=True)
        xn = xn * lax.rsqrt(ms + EPS) * fg_ref[...]
    o_ref[0] = xn


def _out(x, fox_o, p, gdn_o, gate, gng, w_out, final_g, final_norm):
    bsz, s, d = x.shape
    tm = min(ROW_BLOCK, s)
    blk = lambda j: pl.BlockSpec((1, tm, WIDTH), lambda b, i: (b, i, j))
    return pl.pallas_call(
        functools.partial(_out_kernel, final_norm=final_norm),
        grid=(bsz, s // tm),
        in_specs=[pl.BlockSpec((1, tm, d), lambda b, i: (b, i, 0)),
                  blk(0), blk(3), blk(0), blk(7),
                  pl.BlockSpec((1, 1, d), lambda b, i: (b, 0, 0)),
                  pl.BlockSpec((1, HEAD_DIM), lambda b, i: (0, 0)),
                  pl.BlockSpec((2 * WIDTH, d), lambda b, i: (0, 0)),
                  pl.BlockSpec((1, d), lambda b, i: (0, 0))],
        out_specs=pl.BlockSpec((1, tm, d), lambda b, i: (b, i, 0)),
        out_shape=jax.ShapeDtypeStruct((bsz, s, d), F32),
        compiler_params=_params("arbitrary", "arbitrary"),
        name="gate_out_proj",
    )(x, fox_o, p, gdn_o, p, gate, gng, w_out, final_g)


def _expand_matrices():
    lane_head = jnp.arange(WIDTH) // HEAD_DIM
    rows = jnp.arange(GATE_COLS)[:, None]
    mats = [(rows == grp + lane_head[None, :]).astype(F32) for grp in (COL_B, COL_A, COL_A2, COL_A3)]
    return jnp.stack(mats)


def _chunk_last_selector(tb):
    nchunk = tb // CHUNK
    return (jnp.arange(tb)[None, :] == (jnp.arange(nchunk)[:, None] * CHUNK + CHUNK - 1)).astype(F32)


def _split_w_in(w):
    fw = WIDTH
    o_ff = 4 * fw
    o_g = o_ff + HEADS
    o_ga = o_g + 4 * fw
    o_gb = o_ga + HEADS
    w_main = jnp.concatenate([w[:, :o_ff], w[:, o_g:o_ga]], axis=1).astype(BF16)
    cols = {"f": w[:, o_ff:o_g], "a": w[:, o_ga:o_gb], "b": w[:, o_gb:o_gb + HEADS]}
    pad = jnp.zeros((w.shape[0], GATE_COLS - len(GATE_GROUPS) * HEADS), w.dtype)
    w_small = jnp.concatenate([cols[t] for t in GATE_GROUPS] + [pad], axis=1).astype(BF16)
    return w_main, w_small


def _gate_rows(b_f, dt_bias, a_log):
    z = jnp.zeros((HEADS,), F32)
    pad = jnp.zeros((GATE_COLS - len(GATE_GROUPS) * HEADS,), F32)
    add = {"f": b_f, "a": dt_bias, "b": z}
    alog = {"f": z, "a": a_log, "b": z}
    add_row = jnp.concatenate([add[t] for t in GATE_GROUPS] + [pad]).reshape(1, GATE_COLS)
    alog_row = jnp.concatenate([alog[t] for t in GATE_GROUPS] + [pad]).reshape(1, GATE_COLS)
    return add_row.astype(F32), alog_row.astype(F32)


def kernel(x, c, norm_g, w_ada, b_ada, w_in, b_fgate, fox_qn_g, fox_kn_g, gdn_conv_w, gdn_A_log,
           gdn_dt_bias, gdn_norm_g, w_out, final_g):
    bsz, s, d = x.shape
    depth = w_in.shape[0]
    expand = _expand_matrices()
    sel = _chunk_last_selector(min(ROW_BLOCK, s))
    tk = min(FOX_KEYS, s // 2)
    for l in range(depth):
        mod = _ada(c, w_ada[l], b_ada[l])
        shift, scale, gate = (mod[:, k * d:(k + 1) * d].reshape(bsz, 1, d) for k in range(3))
        w_main, w_small = _split_w_in(w_in[l])
        qg = fox_qn_g[l].reshape(1, HEAD_DIM)
        kg = fox_kn_g[l].reshape(1, HEAD_DIM)
        p, ps = _proj(x, shift, scale, norm_g[l].reshape(1, d), w_main, w_small,
                      qg * (HEAD_DIM ** -0.5 * LOG2E), kg, gdn_conv_w[l])

        add_row, alog_row = _gate_rows(b_fgate[l], gdn_dt_bias[l], gdn_A_log[l])
        ft, fx, ga, gb, sm = _gates(ps, add_row, alog_row)

        lo = _fox_block_start(ft, qg, kg, tk)
        fox_o = _fox(lo, p, fx, tk)

        gkb, gvb, gkbe, gqe, gkd, egl = _gdnprep(p, sm, expand, sel)
        gdn_o = _gdn(p, gkb, gvb, gkbe, gqe, gkd, ga, gb, egl)

        x = _out(x, fox_o, p, gdn_o, gate, gdn_norm_g[l].reshape(1, HEAD_DIM), w_out[l].astype(BF16),
                 final_g.reshape(1, d), final_norm=(l == depth - 1))
    return x
```

```python
import functools

import jax
import jax.numpy as jnp
from jax import lax
from jax.experimental import pallas as pl
from jax.experimental.pallas import tpu as pltpu

F32 = jnp.float32
BF16 = jnp.bfloat16
HIGHEST = lax.Precision.HIGHEST

HEADS = 8
HEAD_DIM = 128
WIDTH = HEADS * HEAD_DIM
CHUNK = 64
NEUMANN_BLOCK = 8
CONV_WIDTH = 4
EPS = 1e-6
GATE_COLS = 128
COL_F, COL_A, COL_B, COL_A2, COL_A3, COL_F2, COL_F3, COL_A4 = 0, 8, 16, 24, 32, 40, 48, 56
GATE_GROUPS = "fabaaffa"
LOG2E = 1.4426950408889634
FOX_UNROLL = 4
SOLVE_UNROLL = 4
EXP_UNDERFLOW = 104.0

SUBLANES = 8
PROJ_ROWS = 1024
PROJ_COLS = WIDTH
NORM_ROWS = 128
ADA_COLS = 512
GATE_ROWS = 256
ROW_BLOCK = 512
FOX_KEYS = 512
VMEM_LIMIT = 52 * 1024 * 1024

NT_DIMS = (((1,), (1,)), ((), ()))
TN_DIMS = (((0,), (0,)), ((), ()))


def _dot(a, b, precision=None):
    return jnp.dot(a, b, preferred_element_type=F32, precision=precision)


def _dot_nt(a, b, precision=None):
    return lax.dot_general(a, b, NT_DIMS, preferred_element_type=F32, precision=precision)


def _dot_tn(a, b):
    return lax.dot_general(a, b, TN_DIMS, preferred_element_type=F32)


def _silu(x):
    return x * jax.nn.sigmoid(x)


def _params(*sem):
    return pltpu.CompilerParams(dimension_semantics=sem, vmem_limit_bytes=VMEM_LIMIT)


def _ada_kernel(c_ref, w_ref, b_ref, o_ref):
    o_ref[...] = _dot(_silu(c_ref[...]), w_ref[...], HIGHEST) + b_ref[...]


def _ada(c, w, b):
    bsz, d = c.shape
    n = w.shape[1]
    rows = SUBLANES
    tn = ADA_COLS if n % ADA_COLS == 0 else HEAD_DIM
    cp = jnp.pad(c, ((0, rows - bsz), (0, 0)))
    out = pl.pallas_call(
        _ada_kernel,
        grid=(n // tn,),
        in_specs=[pl.BlockSpec((rows, d), lambda j: (0, 0)),
                  pl.BlockSpec((d, tn), lambda j: (0, j)),
                  pl.BlockSpec((1, tn), lambda j: (0, j))],
        out_specs=pl.BlockSpec((rows, tn), lambda j: (0, j)),
        out_shape=jax.ShapeDtypeStruct((rows, n), F32),
        compiler_params=_params("arbitrary"),
        name="ada_mod",
    )(cp, w, b.reshape(1, n))
    return out[:bsz]


def _proj_kernel(x_ref, sh_ref, sc_ref, g_ref, w_ref, ws_ref, p_ref, ps_ref, h_ref, *, tm, rows):
    @pl.when(pl.program_id(2) == 0)
    def _():
        gmul = g_ref[...] * (1.0 + sc_ref[0])
        shift = sh_ref[0]

        def body(r, carry):
            sl = pl.ds(pl.multiple_of(r * rows, rows), rows)
            xs = x_ref[0, sl, :]
            ms = jnp.mean(xs * xs, axis=-1, keepdims=True)
            h_ref[sl, :] = (xs * lax.rsqrt(ms + EPS) * gmul + shift).astype(BF16)
            return carry

        lax.fori_loop(0, tm // rows, body, 0)
        ps_ref[0] = _dot(h_ref[...], ws_ref[...])

    p_ref[0] = _dot(h_ref[...], w_ref[...]).astype(BF16)


def _proj(x, shift, scale, g, w_main, w_small):
    bsz, s, d = x.shape
    n = w_main.shape[1]
    tm = min(PROJ_ROWS, s)
    tn = PROJ_COLS
    kern = functools.partial(_proj_kernel, tm=tm, rows=min(NORM_ROWS, tm))
    return pl.pallas_call(
        kern,
        grid=(bsz, s // tm, n // tn),
        in_specs=[pl.BlockSpec((1, tm, d), lambda b, i, j: (b, i, 0)),
                  pl.BlockSpec((1, 1, d), lambda b, i, j: (b, 0, 0)),
                  pl.BlockSpec((1, 1, d), lambda b, i, j: (b, 0, 0)),
                  pl.BlockSpec((1, d), lambda b, i, j: (0, 0)),
                  pl.BlockSpec((d, tn), lambda b, i, j: (0, j)),
                  pl.BlockSpec((d, GATE_COLS), lambda b, i, j: (0, 0))],
        out_specs=[pl.BlockSpec((1, tm, tn), lambda b, i, j: (b, i, j)),
                   pl.BlockSpec((1, tm, GATE_COLS), lambda b, i, j: (b, i, 0))],
        out_shape=[jax.ShapeDtypeStruct((bsz, s, n), BF16),
                   jax.ShapeDtypeStruct((bsz, s, GATE_COLS), F32)],
        scratch_shapes=[pltpu.VMEM((tm, d), BF16)],
        compiler_params=_params("arbitrary", "arbitrary", "arbitrary"),
        name="norm_in_proj",
    )(x, shift, scale, g, w_main, w_small)


def _gates_kernel(ps_ref, add_ref, alog_ref, ft_ref, fx_ref, ga_ref, gb_ref, sm_ref, carry_ref, *, tb):
    @pl.when(pl.program_id(1) == 0)
    def _():
        carry_ref[...] = jnp.zeros_like(carry_ref)

    x = ps_ref[0] + add_ref[...]
    col = lax.broadcasted_iota(jnp.int32, (tb, GATE_COLS), 1)
    grp = lambda start: (col >= start) & (col < start + HEADS)
    is_f = grp(COL_F) | grp(COL_F2) | grp(COL_F3)
    is_a = grp(COL_A) | grp(COL_A2) | grp(COL_A3) | grp(COL_A4)
    log_f = jax.nn.log_sigmoid(x)
    g = -jnp.exp(alog_ref[...]) * jax.nn.softplus(x)
    beta = jax.nn.sigmoid(x)
    vals = jnp.where(is_f, log_f, jnp.where(is_a, g, 0.0))

    r = lax.broadcasted_iota(jnp.int32, (tb, tb), 0)
    c = lax.broadcasted_iota(jnp.int32, (tb, tb), 1)
    same_chunk = (r // CHUNK) == (c // CHUNK)
    sums = jnp.concatenate([(r >= c).astype(F32),
                            ((r >= c) & same_chunk).astype(F32),
                            same_chunk.astype(F32)], axis=0).astype(BF16)

    def split3(t):
        hi = t.astype(BF16).astype(F32)
        mid = (t - hi).astype(BF16).astype(F32)
        return hi, mid, t - hi - mid

    pieces = _dot(sums, jnp.concatenate(split3(vals), axis=1).astype(BF16))
    summed = pieces[:, :GATE_COLS] + pieces[:, GATE_COLS:2 * GATE_COLS] + pieces[:, 2 * GATE_COLS:]
    carry = carry_ref[0:1, :]
    cs_all = summed[0:tb] + carry
    cs = summed[tb:2 * tb]
    tot = summed[2 * tb:]
    carry_ref[...] = jnp.broadcast_to(carry + jnp.sum(vals, axis=0, keepdims=True), carry_ref.shape)

    ft_ref[0] = jnp.transpose(cs_all)[0:HEADS, :]
    f_hi, f_mid, f_lo = split3(cs_all * (-LOG2E))
    fx_ref[0] = jnp.where(grp(COL_F), f_hi, jnp.where(grp(COL_F2), f_mid,
                                                      jnp.where(grp(COL_F3), f_lo, 0.0))).astype(BF16)
    c_hi, c_lo, _ = split3(cs)
    ones = grp(COL_A3) | grp(COL_A4)
    ga_ref[0] = jnp.where(grp(COL_A), c_hi, jnp.where(grp(COL_A2), c_lo, jnp.where(ones, 1.0, 0.0))).astype(BF16)
    gb_ref[0] = jnp.where(grp(COL_A) | grp(COL_A2), 1.0,
                          jnp.where(grp(COL_A3), -c_hi, jnp.where(grp(COL_A4), -c_lo, 0.0))).astype(BF16)
    sm_ref[0] = jnp.where(grp(COL_A), jnp.exp(cs),
                          jnp.where(grp(COL_B), beta,
                                    jnp.where(grp(COL_A2), jnp.exp(tot - cs),
                                              jnp.where(grp(COL_A3), jnp.exp(tot), 0.0))))


def _gates(ps, add_row, alog_row):
    bsz, s, _ = ps.shape
    tb = min(GATE_ROWS, s)
    small = pl.BlockSpec((1, tb, GATE_COLS), lambda b, i: (b, i, 0))
    row = pl.BlockSpec((1, GATE_COLS), lambda b, i: (0, 0))
    small_shape = jax.ShapeDtypeStruct((bsz, s, GATE_COLS), F32)
    bf16_shape = jax.ShapeDtypeStruct((bsz, s, GATE_COLS), BF16)
    return pl.pallas_call(
        functools.partial(_gates_kernel, tb=tb),
        grid=(bsz, s // tb),
        in_specs=[small, row, row],
        out_specs=[pl.BlockSpec((1, HEADS, tb), lambda b, i: (b, 0, i)), small, small, small, small],
        out_shape=[jax.ShapeDtypeStruct((bsz, HEADS, s), F32), bf16_shape, bf16_shape, bf16_shape, small_shape],
        scratch_shapes=[pltpu.VMEM((SUBLANES, GATE_COLS), F32)],
        compiler_params=_params("arbitrary", "arbitrary"),
        name="gates",
    )(ps, add_row, alog_row)


def _foxprep_kernel(q_ref, k_ref, qg_ref, kg_ref, qo_ref, ko_ref):
    def norm(src, gain, dst):
        for h in range(HEADS):
            sl = slice(h * HEAD_DIM, (h + 1) * HEAD_DIM)
            t = src[0, :, sl].astype(F32)
            ms = jnp.mean(t * t, axis=-1, keepdims=True)
            dst[0, :, sl] = (t * lax.rsqrt(ms + EPS) * gain).astype(BF16)

    norm(q_ref, qg_ref[...] * (HEAD_DIM ** -0.5 * LOG2E), qo_ref)
    norm(k_ref, kg_ref[...], ko_ref)


def _foxprep(p, qg, kg):
    bsz, s, _ = p.shape
    tb = min(ROW_BLOCK, s)
    blk = lambda j: pl.BlockSpec((1, tb, WIDTH), lambda b, i: (b, i, j))
    row = pl.BlockSpec((1, HEAD_DIM), lambda b, i: (0, 0))
    shape = jax.ShapeDtypeStruct((bsz, s, WIDTH), BF16)
    return pl.pallas_call(
        _foxprep_kernel,
        grid=(bsz, s // tb),
        in_specs=[blk(0), blk(1), row, row],
        out_specs=[blk(0), blk(0)],
        out_shape=[shape, shape],
        compiler_params=_params("arbitrary", "arbitrary"),
        name="fox_qk_norm",
    )(p, p, qg, kg)


def _fox_kernel(lo_ref, q_ref, k_ref, v_ref, fx_ref, o_ref, acc_ref, st_ref, *, tk):
    b = pl.program_id(0)
    h = pl.program_id(1)
    g = pl.program_id(2)
    tq = 2 * tk
    lo = lo_ref[b * HEADS + h, 2 * g]
    lane = lax.broadcasted_iota(jnp.int32, (tq, GATE_COLS), 1)
    pick = (lane == COL_F + h) | (lane == COL_F2 + h) | (lane == COL_F3 + h)
    qa = jnp.concatenate([q_ref[0], jnp.where(pick, 1.0, 0.0).astype(BF16)], axis=1)
    acc_ref[...] = jnp.zeros_like(acc_ref)
    late = slice(tk, tq)

    def key_rows(j):
        return pl.ds(pl.multiple_of(j * tk, tk), tk)

    def scores(j, queries=slice(None)):
        ka = jnp.concatenate([k_ref[0, key_rows(j), :], fx_ref[0, key_rows(j), :]], axis=1)
        return _dot_nt(ka, qa[queries])

    def causal(st):
        r = lax.broadcasted_iota(jnp.int32, st.shape, 0)
        c = lax.broadcasted_iota(jnp.int32, st.shape, 1)
        return jnp.where(r <= c, st, -jnp.inf)

    def issue(j, kind, slot):
        if kind == "full":
            st_ref[slot] = scores(j)
        elif kind == "diag":
            st_ref[slot] = causal(scores(j))
        else:
            st_ref[slot, :, late] = causal(scores(j, late))

    def step(cur_j, cur_slot, carry, nxt=None, only_late=False):
        if nxt is not None:
            issue(nxt[0], nxt[1], 1 - cur_slot)
        cols = late if only_late else slice(None)
        m_all, l_all = carry
        m, l = m_all[:, cols], l_all[:, cols]
        st = st_ref[cur_slot, :, cols]
        m_new = jnp.maximum(m, jnp.max(st, axis=0, keepdims=True))
        alpha = jnp.exp2(m - m_new)
        p = jnp.exp2(st - m_new)
        l_new = alpha * l + jnp.sum(p, axis=0, keepdims=True)
        pv = _dot_tn(v_ref[0, key_rows(cur_j), :], p.astype(BF16))
        acc_ref[:, cols] = alpha * acc_ref[:, cols] + pv
        if only_late:
            m_new = jnp.concatenate([m_all[:, :tk], m_new], axis=1)
            l_new = jnp.concatenate([l_all[:, :tk], l_new], axis=1)
        return m_new, l_new

    d = 2 * g
    n = d - lo

    @pl.when(n == 0)
    def _():
        issue(d, "diag", 0)

    @pl.when(n > 0)
    def _():
        issue(lo, "full", 0)

    def interior_steps(t, carry):
        j = lo + FOX_UNROLL * t
        for u in range(FOX_UNROLL):
            carry = step(j + u, u % 2, carry, nxt=(j + u + 1, "full"))
        return carry

    init = (jnp.full((1, tq), -jnp.inf, F32), jnp.zeros((1, tq), F32))
    loops = jnp.maximum(n - 1, 0) // FOX_UNROLL
    carry = lax.fori_loop(0, loops, interior_steps, init)

    def finish(interior_left):
        def run(carry):
            slot = 0
            for back in range(interior_left, 0, -1):
                carry = step(d - back, slot, carry, nxt=(d - back + 1, "full" if back > 1 else "diag"))
                slot = 1 - slot
            carry = step(d, slot, carry, nxt=(d + 1, "late"))
            return step(d + 1, 1 - slot, carry, only_late=True)
        return run

    left = n - FOX_UNROLL * loops
    m, l = lax.switch(left, [finish(k) for k in range(FOX_UNROLL + 1)], carry)
    o_ref[0] = jnp.transpose(acc_ref[...] / l).astype(BF16)


def _fox(lo, qn, kn, p, fx, tk):
    bsz, s, _ = qn.shape
    tq = 2 * tk
    grid_spec = pltpu.PrefetchScalarGridSpec(
        num_scalar_prefetch=1,
        grid=(bsz, HEADS, s // tq),
        in_specs=[pl.BlockSpec((1, tq, HEAD_DIM), lambda b, h, i, lo_r: (b, i, h)),
                  pl.BlockSpec((1, s, HEAD_DIM), lambda b, h, i, lo_r: (b, 0, h)),
                  pl.BlockSpec((1, s, HEAD_DIM), lambda b, h, i, lo_r: (b, 0, 2 * HEADS + h)),
                  pl.BlockSpec((1, s, GATE_COLS), lambda b, h, i, lo_r: (b, 0, 0))],
        out_specs=pl.BlockSpec((1, tq, HEAD_DIM), lambda b, h, i, lo_r: (b, i, h)),
        scratch_shapes=[pltpu.VMEM((HEAD_DIM, tq), F32), pltpu.VMEM((2, tk, tq), F32)],
    )
    return pl.pallas_call(
        functools.partial(_fox_kernel, tk=tk),
        grid_spec=grid_spec,
        out_shape=jax.ShapeDtypeStruct((bsz, s, WIDTH), BF16),
        compiler_params=_params("arbitrary", "arbitrary", "arbitrary"),
        name="fox_attention",
    )(lo, qn, kn, p, fx)


def _fox_block_start(ft, qg, kg, tq):
    bsz, _, s = ft.shape
    f_first = ft[:, :, 0::tq]
    f_last = ft[:, :, tq - 1::tq]
    qk_bound = 1.02 * (HEAD_DIM ** 0.5) * jnp.max(jnp.abs(qg)) * jnp.max(jnp.abs(kg))
    thresh = EXP_UNDERFLOW + 2.0 * qk_bound
    skip = f_last[:, :, None, :] > f_first[:, :, :, None] + thresh
    return jnp.sum(skip, axis=-1).astype(jnp.int32).reshape(bsz * HEADS, s // tq)


def _gdnprep_kernel(q_ref, k_ref, v_ref, qh_ref, kh_ref, vh_ref, cw_ref, sm_ref, ex_ref, sel_ref,
                    qo_ref, ko_ref, kbo_ref, vbo_ref, kbeo_ref, qeo_ref, kdo_ref, eglo_ref, *, tb):
    first = pl.program_id(1) == 0

    def conv_silu(src, halo, which):
        ext = jnp.concatenate([jnp.where(first, 0.0, halo[0].astype(F32)), src[0].astype(F32)], axis=0)
        acc = jnp.zeros((tb, WIDTH), F32)
        for t in range(CONV_WIDTH):
            w_row = cw_ref[t:t + 1, which * WIDTH:(which + 1) * WIDTH]
            back = CONV_WIDTH - 1 - t
            shifted = pltpu.roll(ext, back, axis=0) if back else ext
            acc = acc + shifted[SUBLANES:, :] * w_row
        return _silu(acc)

    def spread(group):
        e = ex_ref[group].astype(BF16)
        return _dot(sm2, jnp.concatenate([e, e], axis=0))

    def l2norm(t):
        parts = []
        for h in range(HEADS):
            th = t[:, h * HEAD_DIM:(h + 1) * HEAD_DIM]
            parts.append(th * lax.rsqrt(jnp.sum(th * th, axis=-1, keepdims=True) + EPS))
        return jnp.concatenate(parts, axis=-1)

    sm = sm_ref[0]
    sm_hi = sm.astype(BF16)
    sm2 = jnp.concatenate([sm_hi, (sm - sm_hi.astype(F32)).astype(BF16)], axis=1)
    beta_x = spread(0)
    egc_x = spread(1)
    edec_x = spread(2)
    eglo_ref[0] = _dot(_dot(sel_ref[...], sm, HIGHEST), ex_ref[3], HIGHEST)

    k = l2norm(conv_silu(k_ref, kh_ref, 1))
    kb = k * beta_x
    ko_ref[0] = k.astype(BF16)
    kbo_ref[0] = kb.astype(BF16)
    kbeo_ref[0] = (kb * egc_x).astype(BF16)
    kdo_ref[0] = (k * edec_x).astype(BF16)
    q = l2norm(conv_silu(q_ref, qh_ref, 0)) * (HEAD_DIM ** -0.5)
    qo_ref[0] = q.astype(BF16)
    qeo_ref[0] = (q * egc_x).astype(BF16)
    v = conv_silu(v_ref, vh_ref, 2)
    vbo_ref[0] = (v * beta_x).astype(BF16)


def _gdnprep(p, conv_w, sm, expand, sel):
    bsz, s, _ = p.shape
    tb = min(ROW_BLOCK, s)
    nchunk = tb // CHUNK
    blk = lambda j: pl.BlockSpec((1, tb, WIDTH), lambda b, i: (b, i, j))
    halo = lambda j: pl.BlockSpec((1, SUBLANES, WIDTH),
                                  lambda b, i: (b, jnp.maximum(i * (tb // SUBLANES) - 1, 0), j))
    shape = jax.ShapeDtypeStruct((bsz, s, WIDTH), BF16)
    return pl.pallas_call(
        functools.partial(_gdnprep_kernel, tb=tb),
        grid=(bsz, s // tb),
        in_specs=[blk(4), blk(5), blk(6), halo(4), halo(5), halo(6),
                  pl.BlockSpec((CONV_WIDTH, 3 * WIDTH), lambda b, i: (0, 0)),
                  pl.BlockSpec((1, tb, GATE_COLS), lambda b, i: (b, i, 0)),
                  pl.BlockSpec((4, GATE_COLS, WIDTH), lambda b, i: (0, 0, 0)),
                  pl.BlockSpec((nchunk, tb), lambda b, i: (0, 0))],
        out_specs=[blk(0)] * 7 + [pl.BlockSpec((1, nchunk, WIDTH), lambda b, i: (b, i, 0))],
        out_shape=[shape] * 7 + [jax.ShapeDtypeStruct((bsz, s // CHUNK, WIDTH), F32)],
        compiler_params=_params("arbitrary", "arbitrary"),
        name="gdn_prep",
    )(p, p, p, p, p, p, conv_w, sm, expand, sel)


def _gdn_kernel(q_ref, k_ref, kb_ref, vb_ref, kbe_ref, ga_ref, gb_ref, qe_ref, kd_ref, egl_ref,
                o_ref, s_ref, u_ref, w_ref, attn_ref, *, nchunk, nblk):
    i = pl.program_id(1)

    @pl.when(i == 0)
    def _():
        s_ref[...] = jnp.zeros_like(s_ref)

    r = lax.broadcasted_iota(jnp.int32, (CHUNK, CHUNK), 0)
    c = lax.broadcasted_iota(jnp.int32, (CHUNK, CHUNK), 1)
    lower = r >= c
    strict = r > c
    eye = (r == c).astype(F32)
    diag8 = (r // NEUMANN_BLOCK) == (c // NEUMANN_BLOCK)
    widths = [NEUMANN_BLOCK << lvl for lvl in range((CHUNK // NEUMANN_BLOCK).bit_length() - 1)]
    levels = [((r // (2 * w)) == (c // (2 * w))) & ((r // w) != (c // w)) for w in widths]
    gcol = lax.broadcasted_iota(jnp.int32, (CHUNK, GATE_COLS), 1)
    heads = range(HEADS)
    hsl = [slice(h * HEAD_DIM, (h + 1) * HEAD_DIM) for h in heads]
    head_cols = [(gcol == COL_A + h) | (gcol == COL_A2 + h) | (gcol == COL_A3 + h) | (gcol == COL_A4 + h)
                 for h in heads]

    unroll = range(SOLVE_UNROLL)

    def chunk_rows(it, t):
        return pl.ds(pl.multiple_of((it * SOLVE_UNROLL + t) * CHUNK, CHUNK), CHUNK)

    def solve_stages(it):
        probs = [(t, h) for t in unroll for h in heads]
        n = range(len(probs))
        rows = [chunk_rows(it, t) for t in unroll]
        ga = [ga_ref[0, rows[t], :] for t in unroll]
        gb = [gb_ref[0, rows[t], :] for t in unroll]
        gbh = [jnp.where(head_cols[h], gb[t], jnp.zeros((), BF16)) for t, h in probs]
        dlog = [_dot_nt(ga[t], gbh[p]) for p, (t, h) in enumerate(probs)]
        kk = [_dot_nt(kb_ref[0, rows[t], hsl[h]], k_ref[0, rows[t], hsl[h]]) for t, h in probs]
        qk = [_dot_nt(q_ref[0, rows[t], hsl[h]], k_ref[0, rows[t], hsl[h]]) for t, h in probs]
        yield
        decay = [jnp.exp(jnp.where(lower, dlog[p], -jnp.inf)) for p in n]
        m = [jnp.where(strict, kk[p] * decay[p], 0.0) for p in n]
        attn = [(qk[p] * decay[p]).astype(BF16) for p in n]
        md = [jnp.where(diag8, m[p], 0.0).astype(BF16) for p in n]
        m2 = [_dot(md[p], md[p]).astype(BF16) for p in n]
        yield
        inv = [eye - md[p].astype(F32) for p in n]
        m4 = [_dot(m2[p], m2[p]).astype(BF16) for p in n]
        inv = [inv[p] + _dot(inv[p].astype(BF16), m2[p]) for p in n]
        yield
        inv = [inv[p] + _dot(inv[p].astype(BF16), m4[p]) for p in n]
        yield
        for lvl in levels:
            off = [jnp.where(lvl, m[p], 0.0).astype(BF16) for p in n]
            invb = [inv[p].astype(BF16) for p in n]
            x = [_dot(invb[p], off[p]).astype(BF16) for p in n]
            yield
            inv = [inv[p] - _dot(x[p], invb[p]) for p in n]
            yield
        uw = [_dot(inv[p].astype(BF16),
                   jnp.concatenate([vb_ref[0, rows[t], hsl[h]], kbe_ref[0, rows[t], hsl[h]]], axis=-1))
              for p, (t, h) in enumerate(probs)]
        yield
        for p, (t, h) in enumerate(probs):
            attn_ref[rows[t], h * HEAD_DIM:h * HEAD_DIM + CHUNK] = attn[p]
            u_ref[rows[t], hsl[h]] = uw[p][:, :HEAD_DIM]
            w_ref[rows[t], hsl[h]] = uw[p][:, HEAD_DIM:].astype(BF16)

    def scan_stages(it):
        state = [s_ref[h] for h in heads]
        for t in unroll:
            rows = chunk_rows(it, t)
            egl = egl_ref[0, pl.ds(it * SOLVE_UNROLL + t, 1), :]
            u = [u_ref[rows, hsl[h]] for h in heads]
            attn = [attn_ref[rows, h * HEAD_DIM:h * HEAD_DIM + CHUNK] for h in heads]
            ws = [_dot(jnp.concatenate([w_ref[rows, hsl[h]], qe_ref[0, rows, hsl[h]]], axis=0),
                       state[h].astype(BF16)) for h in heads]
            yield
            v_new = [(u[h] - ws[h][:CHUNK]).astype(BF16) for h in heads]
            for h in heads:
                o_ref[0, rows, hsl[h]] = (ws[h][CHUNK:] + _dot(attn[h], v_new[h])).astype(BF16)
            state = [state[h] * egl[:, hsl[h]] + _dot_tn(kd_ref[0, rows, hsl[h]], v_new[h]) for h in heads]
            yield
        for h in heads:
            s_ref[h] = state[h]

    def drain(gen):
        for _ in gen:
            pass

    def solve_only(it, carry):
        drain(solve_stages(it))
        return carry

    def scan_only(it, carry):
        drain(scan_stages(it))
        return carry

    def solve_and_scan(it, carry):
        scan = scan_stages(it)
        solve = solve_stages(it)
        solve_yields = 5 + 2 * len(levels)
        scan_yields = 2 * SOLVE_UNROLL
        for stage in range(solve_yields):
            next(solve)
            if (stage * scan_yields) // solve_yields != ((stage + 1) * scan_yields) // solve_yields:
                next(scan, None)
        drain(scan)
        drain(solve)
        return carry

    iters = nchunk // SOLVE_UNROLL

    @pl.when(i == 0)
    def _():
        lax.fori_loop(0, iters, solve_only, 0)

    @pl.when((i > 0) & (i < nblk))
    def _():
        lax.fori_loop(0, iters, solve_and_scan, 0)

    @pl.when(i == nblk)
    def _():
        lax.fori_loop(0, iters, scan_only, 0)


def _gdn(q, k, kb, vb, kbe, qe, kd, ga, gb, egl):
    bsz, s, _ = q.shape
    tb = min(ROW_BLOCK, s)
    nchunk = tb // CHUNK
    nblk = s // tb
    cur = lambda b, i: (b, jnp.minimum(i, nblk - 1), 0)
    prev = lambda b, i: (b, jnp.maximum(i - 1, 0), 0)
    return pl.pallas_call(
        functools.partial(_gdn_kernel, nchunk=nchunk, nblk=nblk),
        grid=(bsz, nblk + 1),
        in_specs=[pl.BlockSpec((1, tb, WIDTH), cur)] * 5 + [pl.BlockSpec((1, tb, GATE_COLS), cur)] * 2
                 + [pl.BlockSpec((1, tb, WIDTH), prev)] * 2 + [pl.BlockSpec((1, nchunk, WIDTH), prev)],
        out_specs=pl.BlockSpec((1, tb, WIDTH), prev),
        out_shape=jax.ShapeDtypeStruct((bsz, s, WIDTH), BF16),
        scratch_shapes=[pltpu.VMEM((HEADS, HEAD_DIM, HEAD_DIM), F32),
                        pltpu.VMEM((tb, WIDTH), F32), pltpu.VMEM((tb, WIDTH), BF16),
                        pltpu.VMEM((tb, WIDTH), BF16)],
        compiler_params=_params("arbitrary", "arbitrary"),
        name="gdn_delta_rule",
    )(q, k, kb, vb, kbe, ga, gb, qe, kd, egl)


def _out_kernel(x_ref, fo_ref, fz_ref, go_ref, gz_ref, gate_ref, gng_ref, w_ref, fg_ref, o_ref, *, final_norm):
    a = fo_ref[0].astype(F32) * _silu(fz_ref[0].astype(F32))
    parts = []
    for h in range(HEADS):
        sl = slice(h * HEAD_DIM, (h + 1) * HEAD_DIM)
        t = go_ref[0, :, sl].astype(F32)
        ms = jnp.mean(t * t, axis=-1, keepdims=True)
        parts.append(t * lax.rsqrt(ms + EPS) * gng_ref[...])
    g = jnp.concatenate(parts, axis=-1) * _silu(gz_ref[0].astype(F32))
    y = _dot(a.astype(BF16), w_ref[0:WIDTH, :]) + _dot(g.astype(BF16), w_ref[WIDTH:2 * WIDTH, :])
    xn = x_ref[0] + gate_ref[0] * y
    if final_norm:
        ms = jnp.mean(xn * xn, axis=-1, keepdims=True)
        xn = xn * lax.rsqrt(ms + EPS) * fg_ref[...]
    o_ref[0] = xn


def _out(x, fox_o, p, gdn_o, gate, gng, w_out, final_g, final_norm):
    bsz, s, d = x.shape
    tm = min(ROW_BLOCK, s)
    blk = lambda j: pl.BlockSpec((1, tm, WIDTH), lambda b, i: (b, i, j))
    return pl.pallas_call(
        functools.partial(_out_kernel, final_norm=final_norm),
        grid=(bsz, s // tm),
        in_specs=[pl.BlockSpec((1, tm, d), lambda b, i: (b, i, 0)),
                  blk(0), blk(3), blk(0), blk(7),
                  pl.BlockSpec((1, 1, d), lambda b, i: (b, 0, 0)),
                  pl.BlockSpec((1, HEAD_DIM), lambda b, i: (0, 0)),
                  pl.BlockSpec((2 * WIDTH, d), lambda b, i: (0, 0)),
                  pl.BlockSpec((1, d), lambda b, i: (0, 0))],
        out_specs=pl.BlockSpec((1, tm, d), lambda b, i: (b, i, 0)),
        out_shape=jax.ShapeDtypeStruct((bsz, s, d), F32),
        compiler_params=_params("arbitrary", "arbitrary"),
        name="gate_out_proj",
    )(x, fox_o, p, gdn_o, p, gate, gng, w_out, final_g)


def _expand_matrices():
    lane_head = jnp.arange(WIDTH) // HEAD_DIM
    rows = jnp.arange(GATE_COLS)[:, None]
    mats = [(rows == grp + lane_head[None, :]).astype(F32) for grp in (COL_B, COL_A, COL_A2, COL_A3)]
    return jnp.stack(mats)


def _chunk_last_selector(tb):
    nchunk = tb // CHUNK
    return (jnp.arange(tb)[None, :] == (jnp.arange(nchunk)[:, None] * CHUNK + CHUNK - 1)).astype(F32)


def _pack_w_kernel(wa_ref, wb_ref, o_ref, *, fox_tiles):
    t = pl.program_id(0)

    @pl.when(t < fox_tiles)
    def _():
        o_ref[...] = wa_ref[...].astype(BF16)

    @pl.when(t >= fox_tiles)
    def _():
        both = jnp.concatenate([wa_ref[...], wb_ref[...]], axis=1)
        o_ref[...] = pltpu.roll(both, both.shape[1] - HEADS, axis=1)[:, :WIDTH].astype(BF16)


def _pack_w_main(w):
    d = w.shape[0]
    fox_tiles = 4
    ntiles = 2 * fox_tiles
    rows = min(PROJ_ROWS, d)
    return pl.pallas_call(
        functools.partial(_pack_w_kernel, fox_tiles=fox_tiles),
        grid=(ntiles, d // rows),
        in_specs=[pl.BlockSpec((rows, WIDTH), lambda t, r: (r, t)),
                  pl.BlockSpec((rows, HEAD_DIM), lambda t, r: (r, (t + 1) * (WIDTH // HEAD_DIM)))],
        out_specs=pl.BlockSpec((rows, WIDTH), lambda t, r: (r, t)),
        out_shape=jax.ShapeDtypeStruct((d, ntiles * WIDTH), BF16),
        compiler_params=_params("arbitrary", "arbitrary"),
        name="pack_w_in",
    )(w, w)


def _split_w_in(w):
    fw = WIDTH
    o_ff = 4 * fw
    o_g = o_ff + HEADS
    o_ga = o_g + 4 * fw
    o_gb = o_ga + HEADS
    w_main = _pack_w_main(w)
    cols = {"f": w[:, o_ff:o_g], "a": w[:, o_ga:o_gb], "b": w[:, o_gb:o_gb + HEADS]}
    pad = jnp.zeros((w.shape[0], GATE_COLS - len(GATE_GROUPS) * HEADS), w.dtype)
    w_small = jnp.concatenate([cols[t] for t in GATE_GROUPS] + [pad], axis=1).astype(BF16)
    return w_main, w_small


def _gate_rows(b_f, dt_bias, a_log):
    z = jnp.zeros((HEADS,), F32)
    pad = jnp.zeros((GATE_COLS - len(GATE_GROUPS) * HEADS,), F32)
    add = {"f": b_f, "a": dt_bias, "b": z}
    alog = {"f": z, "a": a_log, "b": z}
    add_row = jnp.concatenate([add[t] for t in GATE_GROUPS] + [pad]).reshape(1, GATE_COLS)
    alog_row = jnp.concatenate([alog[t] for t in GATE_GROUPS] + [pad]).reshape(1, GATE_COLS)
    return add_row.astype(F32), alog_row.astype(F32)


def kernel(x, c, norm_g, w_ada, b_ada, w_in, b_fgate, fox_qn_g, fox_kn_g, gdn_conv_w, gdn_A_log,
           gdn_dt_bias, gdn_norm_g, w_out, final_g):
    bsz, s, d = x.shape
    depth = w_in.shape[0]
    expand = _expand_matrices()
    sel = _chunk_last_selector(min(ROW_BLOCK, s))
    tk = min(FOX_KEYS, s // 2)
    for l in range(depth):
        mod = _ada(c, w_ada[l], b_ada[l])
        shift, scale, gate = (mod[:, k * d:(k + 1) * d].reshape(bsz, 1, d) for k in range(3))
        w_main, w_small = _split_w_in(w_in[l])
        p, ps = _proj(x, shift, scale, norm_g[l].reshape(1, d), w_main, w_small)

        add_row, alog_row = _gate_rows(b_fgate[l], gdn_dt_bias[l], gdn_A_log[l])
        ft, fx, ga, gb, sm = _gates(ps, add_row, alog_row)

        qg = fox_qn_g[l].reshape(1, HEAD_DIM)
        kg = fox_kn_g[l].reshape(1, HEAD_DIM)
        qn, kn = _foxprep(p, qg, kg)
        lo = _fox_block_start(ft, qg, kg, tk)
        fox_o = _fox(lo, qn, kn, p, fx, tk)

        gq, gk, gkb, gvb, gkbe, gqe, gkd, egl = _gdnprep(p, gdn_conv_w[l], sm, expand, sel)
        gdn_o = _gdn(gq, gk, gkb, gvb, gkbe, gqe, gkd, ga, gb, egl)

        x = _out(x, fox_o, p, gdn_o, gate, gdn_norm_g[l].reshape(1, HEAD_DIM), w_out[l].astype(BF16),
                 final_g.reshape(1, d), final_norm=(l == depth - 1))
    return x
```

```python
import functools

import jax
import jax.numpy as jnp
from jax import lax
from jax.experimental import pallas as pl
from jax.experimental.pallas import tpu as pltpu

F32 = jnp.float32
BF16 = jnp.bfloat16
HIGHEST = lax.Precision.HIGHEST

HEADS = 8
HEAD_DIM = 128
WIDTH = HEADS * HEAD_DIM
CHUNK = 64
NEUMANN_BLOCK = 8
CONV_WIDTH = 4
EPS = 1e-6
GATE_COLS = 128
COL_F, COL_A, COL_B, COL_A2, COL_A3, COL_F2, COL_F3, COL_A4, COL_A5, COL_A6 = 0, 8, 16, 24, 32, 40, 48, 56, 64, 72
GATE_GROUPS = "fabaaffaaa"
LOG2E = 1.4426950408889634
FOX_UNROLL = 4
SOLVE_UNROLL = 4
EXP_UNDERFLOW = 104.0

SUBLANES = 8
PROJ_ROWS = 1024
PROJ_COLS = WIDTH
NORM_ROWS = 128
ADA_COLS = 512
GATE_ROWS = 256
ROW_BLOCK = 512
FOX_KEYS = 512
VMEM_LIMIT = 52 * 1024 * 1024

NT_DIMS = (((1,), (1,)), ((), ()))
TN_DIMS = (((0,), (0,)), ((), ()))


def _dot(a, b, precision=None):
    return jnp.dot(a, b, preferred_element_type=F32, precision=precision)


def _dot_nt(a, b, precision=None):
    return lax.dot_general(a, b, NT_DIMS, preferred_element_type=F32, precision=precision)


def _dot_tn(a, b):
    return lax.dot_general(a, b, TN_DIMS, preferred_element_type=F32)


def _silu(x):
    return x * jax.nn.sigmoid(x)


def _params(*sem):
    return pltpu.CompilerParams(dimension_semantics=sem, vmem_limit_bytes=VMEM_LIMIT)


def _ada_kernel(c_ref, w_ref, b_ref, o_ref):
    o_ref[...] = _dot(_silu(c_ref[...]), w_ref[...], HIGHEST) + b_ref[...]


def _ada(c, w, b):
    bsz, d = c.shape
    n = w.shape[1]
    rows = SUBLANES
    tn = ADA_COLS if n % ADA_COLS == 0 else HEAD_DIM
    cp = jnp.pad(c, ((0, rows - bsz), (0, 0)))
    out = pl.pallas_call(
        _ada_kernel,
        grid=(n // tn,),
        in_specs=[pl.BlockSpec((rows, d), lambda j: (0, 0)),
                  pl.BlockSpec((d, tn), lambda j: (0, j)),
                  pl.BlockSpec((1, tn), lambda j: (0, j))],
        out_specs=pl.BlockSpec((rows, tn), lambda j: (0, j)),
        out_shape=jax.ShapeDtypeStruct((rows, n), F32),
        compiler_params=_params("arbitrary"),
        name="ada_mod",
    )(cp, w, b.reshape(1, n))
    return out[:bsz]


def _proj_kernel(x_ref, sh_ref, sc_ref, g_ref, w_ref, ws_ref, p_ref, ps_ref, h_ref, *, tm, rows):
    @pl.when(pl.program_id(2) == 0)
    def _():
        gmul = g_ref[...] * (1.0 + sc_ref[0])
        shift = sh_ref[0]

        def body(r, carry):
            sl = pl.ds(pl.multiple_of(r * rows, rows), rows)
            xs = x_ref[0, sl, :]
            ms = jnp.mean(xs * xs, axis=-1, keepdims=True)
            h_ref[sl, :] = (xs * lax.rsqrt(ms + EPS) * gmul + shift).astype(BF16)
            return carry

        lax.fori_loop(0, tm // rows, body, 0)
        ps_ref[0] = _dot(h_ref[...], ws_ref[...])

    p_ref[0] = _dot(h_ref[...], w_ref[...]).astype(BF16)


def _proj(x, shift, scale, g, w_main, w_small):
    bsz, s, d = x.shape
    n = w_main.shape[1]
    tm = min(PROJ_ROWS, s)
    tn = PROJ_COLS
    kern = functools.partial(_proj_kernel, tm=tm, rows=min(NORM_ROWS, tm))
    return pl.pallas_call(
        kern,
        grid=(bsz, s // tm, n // tn),
        in_specs=[pl.BlockSpec((1, tm, d), lambda b, i, j: (b, i, 0)),
                  pl.BlockSpec((1, 1, d), lambda b, i, j: (b, 0, 0)),
                  pl.BlockSpec((1, 1, d), lambda b, i, j: (b, 0, 0)),
                  pl.BlockSpec((1, d), lambda b, i, j: (0, 0)),
                  pl.BlockSpec((d, tn), lambda b, i, j: (0, j)),
                  pl.BlockSpec((d, GATE_COLS), lambda b, i, j: (0, 0))],
        out_specs=[pl.BlockSpec((1, tm, tn), lambda b, i, j: (b, i, j)),
                   pl.BlockSpec((1, tm, GATE_COLS), lambda b, i, j: (b, i, 0))],
        out_shape=[jax.ShapeDtypeStruct((bsz, s, n), BF16),
                   jax.ShapeDtypeStruct((bsz, s, GATE_COLS), F32)],
        scratch_shapes=[pltpu.VMEM((tm, d), BF16)],
        compiler_params=_params("arbitrary", "arbitrary", "arbitrary"),
        name="norm_in_proj",
    )(x, shift, scale, g, w_main, w_small)


def _gates_kernel(ps_ref, add_ref, alog_ref, ft_ref, fx_ref, ga_ref, gb_ref, sm_ref, carry_ref, *, tb):
    @pl.when(pl.program_id(1) == 0)
    def _():
        carry_ref[...] = jnp.zeros_like(carry_ref)

    x = ps_ref[0] + add_ref[...]
    col = lax.broadcasted_iota(jnp.int32, (tb, GATE_COLS), 1)
    grp = lambda start: (col >= start) & (col < start + HEADS)
    is_f = grp(COL_F) | grp(COL_F2) | grp(COL_F3)
    is_a = grp(COL_A) | grp(COL_A2) | grp(COL_A3) | grp(COL_A4) | grp(COL_A5) | grp(COL_A6)
    log_f = jax.nn.log_sigmoid(x)
    g = -jnp.exp(alog_ref[...]) * jax.nn.softplus(x)
    beta = jax.nn.sigmoid(x)
    vals = jnp.where(is_f, log_f, jnp.where(is_a, g, 0.0))

    r = lax.broadcasted_iota(jnp.int32, (tb, tb), 0)
    c = lax.broadcasted_iota(jnp.int32, (tb, tb), 1)
    same_chunk = (r // CHUNK) == (c // CHUNK)
    sums = jnp.concatenate([(r >= c).astype(F32),
                            ((r >= c) & same_chunk).astype(F32),
                            same_chunk.astype(F32)], axis=0).astype(BF16)

    def split3(t):
        hi = t.astype(BF16).astype(F32)
        mid = (t - hi).astype(BF16).astype(F32)
        return hi, mid, t - hi - mid

    pieces = _dot(sums, jnp.concatenate(split3(vals), axis=1).astype(BF16))
    summed = pieces[:, :GATE_COLS] + pieces[:, GATE_COLS:2 * GATE_COLS] + pieces[:, 2 * GATE_COLS:]
    carry = carry_ref[0:1, :]
    cs_all = summed[0:tb] + carry
    cs = summed[tb:2 * tb]
    tot = summed[2 * tb:]
    carry_ref[...] = jnp.broadcast_to(carry + jnp.sum(vals, axis=0, keepdims=True), carry_ref.shape)

    ft_ref[0] = jnp.transpose(cs_all)[0:HEADS, :]
    f_hi, f_mid, f_lo = split3(cs_all * (-LOG2E))
    fx_ref[0] = jnp.where(grp(COL_F), f_hi, jnp.where(grp(COL_F2), f_mid,
                                                      jnp.where(grp(COL_F3), f_lo, 0.0))).astype(BF16)
    c_hi, c_mid, c_lo = split3(cs)
    gc_cols = grp(COL_A) | grp(COL_A2) | grp(COL_A5)
    ones = grp(COL_A3) | grp(COL_A4) | grp(COL_A6)
    ga_ref[0] = jnp.where(grp(COL_A), c_hi, jnp.where(grp(COL_A2), c_mid, jnp.where(grp(COL_A5), c_lo,
                                                                                      jnp.where(ones, 1.0, 0.0)))).astype(BF16)
    gb_ref[0] = jnp.where(gc_cols, 1.0,
                          jnp.where(grp(COL_A3), -c_hi, jnp.where(grp(COL_A4), -c_mid,
                                                                  jnp.where(grp(COL_A6), -c_lo, 0.0)))).astype(BF16)
    sm_ref[0] = jnp.where(grp(COL_A), jnp.exp(cs),
                          jnp.where(grp(COL_B), beta,
                                    jnp.where(grp(COL_A2), jnp.exp(tot - cs),
                                              jnp.where(grp(COL_A3), jnp.exp(tot), 0.0))))


def _gates(ps, add_row, alog_row):
    bsz, s, _ = ps.shape
    tb = min(GATE_ROWS, s)
    small = pl.BlockSpec((1, tb, GATE_COLS), lambda b, i: (b, i, 0))
    row = pl.BlockSpec((1, GATE_COLS), lambda b, i: (0, 0))
    small_shape = jax.ShapeDtypeStruct((bsz, s, GATE_COLS), F32)
    bf16_shape = jax.ShapeDtypeStruct((bsz, s, GATE_COLS), BF16)
    return pl.pallas_call(
        functools.partial(_gates_kernel, tb=tb),
        grid=(bsz, s // tb),
        in_specs=[small, row, row],
        out_specs=[pl.BlockSpec((1, HEADS, tb), lambda b, i: (b, 0, i)), small, small, small, small],
        out_shape=[jax.ShapeDtypeStruct((bsz, HEADS, s), F32), bf16_shape, bf16_shape, bf16_shape, small_shape],
        scratch_shapes=[pltpu.VMEM((SUBLANES, GATE_COLS), F32)],
        compiler_params=_params("arbitrary", "arbitrary"),
        name="gates",
    )(ps, add_row, alog_row)


def _foxprep_kernel(q_ref, k_ref, qg_ref, kg_ref, qo_ref, ko_ref):
    def norm(src, gain, dst):
        for h in range(HEADS):
            sl = slice(h * HEAD_DIM, (h + 1) * HEAD_DIM)
            t = src[0, :, sl].astype(F32)
            ms = jnp.mean(t * t, axis=-1, keepdims=True)
            dst[0, :, sl] = (t * lax.rsqrt(ms + EPS) * gain).astype(BF16)

    norm(q_ref, qg_ref[...] * (HEAD_DIM ** -0.5 * LOG2E), qo_ref)
    norm(k_ref, kg_ref[...], ko_ref)


def _foxprep(p, qg, kg):
    bsz, s, _ = p.shape
    tb = min(ROW_BLOCK, s)
    blk = lambda j: pl.BlockSpec((1, tb, WIDTH), lambda b, i: (b, i, j))
    row = pl.BlockSpec((1, HEAD_DIM), lambda b, i: (0, 0))
    shape = jax.ShapeDtypeStruct((bsz, s, WIDTH), BF16)
    return pl.pallas_call(
        _foxprep_kernel,
        grid=(bsz, s // tb),
        in_specs=[blk(0), blk(1), row, row],
        out_specs=[blk(0), blk(0)],
        out_shape=[shape, shape],
        compiler_params=_params("arbitrary", "arbitrary"),
        name="fox_qk_norm",
    )(p, p, qg, kg)


def _fox_kernel(lo_ref, q_ref, k_ref, v_ref, fx_ref, o_ref, acc_ref, st_ref, *, tk):
    b = pl.program_id(0)
    h = pl.program_id(1)
    g = pl.program_id(2)
    tq = 2 * tk
    lo = lo_ref[b * HEADS + h, 2 * g]
    lane = lax.broadcasted_iota(jnp.int32, (tq, GATE_COLS), 1)
    pick = (lane == COL_F + h) | (lane == COL_F2 + h) | (lane == COL_F3 + h)
    qa = jnp.concatenate([q_ref[0], jnp.where(pick, 1.0, 0.0).astype(BF16)], axis=1)
    acc_ref[...] = jnp.zeros_like(acc_ref)
    late = slice(tk, tq)

    def key_rows(j):
        return pl.ds(pl.multiple_of(j * tk, tk), tk)

    def scores(j, queries=slice(None)):
        ka = jnp.concatenate([k_ref[0, key_rows(j), :], fx_ref[0, key_rows(j), :]], axis=1)
        return _dot_nt(ka, qa[queries])

    def causal(st):
        r = lax.broadcasted_iota(jnp.int32, st.shape, 0)
        c = lax.broadcasted_iota(jnp.int32, st.shape, 1)
        return jnp.where(r <= c, st, -jnp.inf)

    def issue(j, kind, slot):
        if kind == "full":
            st_ref[slot] = scores(j)
        elif kind == "diag":
            st_ref[slot] = causal(scores(j))
        else:
            st_ref[slot, :, late] = causal(scores(j, late))

    def step(cur_j, cur_slot, carry, nxt=None, only_late=False):
        if nxt is not None:
            issue(nxt[0], nxt[1], 1 - cur_slot)
        cols = late if only_late else slice(None)
        m_all, l_all = carry
        m, l = m_all[:, cols], l_all[:, cols]
        st = st_ref[cur_slot, :, cols]
        m_new = jnp.maximum(m, jnp.max(st, axis=0, keepdims=True))
        alpha = jnp.exp2(m - m_new)
        p = jnp.exp2(st - m_new)
        l_new = alpha * l + jnp.sum(p, axis=0, keepdims=True)
        pv = _dot_tn(v_ref[0, key_rows(cur_j), :], p.astype(BF16))
        acc_ref[:, cols] = alpha * acc_ref[:, cols] + pv
        if only_late:
            m_new = jnp.concatenate([m_all[:, :tk], m_new], axis=1)
            l_new = jnp.concatenate([l_all[:, :tk], l_new], axis=1)
        return m_new, l_new

    d = 2 * g
    n = d - lo

    @pl.when(n == 0)
    def _():
        issue(d, "diag", 0)

    @pl.when(n > 0)
    def _():
        issue(lo, "full", 0)

    def interior_steps(t, carry):
        j = lo + FOX_UNROLL * t
        for u in range(FOX_UNROLL):
            carry = step(j + u, u % 2, carry, nxt=(j + u + 1, "full"))
        return carry

    init = (jnp.full((1, tq), -jnp.inf, F32), jnp.zeros((1, tq), F32))
    loops = jnp.maximum(n - 1, 0) // FOX_UNROLL
    carry = lax.fori_loop(0, loops, interior_steps, init)

    def finish(interior_left):
        def run(carry):
            slot = 0
            for back in range(interior_left, 0, -1):
                carry = step(d - back, slot, carry, nxt=(d - back + 1, "full" if back > 1 else "diag"))
                slot = 1 - slot
            carry = step(d, slot, carry, nxt=(d + 1, "late"))
            return step(d + 1, 1 - slot, carry, only_late=True)
        return run

    left = n - FOX_UNROLL * loops
    m, l = lax.switch(left, [finish(k) for k in range(FOX_UNROLL + 1)], carry)
    o_ref[0] = jnp.transpose(acc_ref[...] / l).astype(BF16)


def _fox(lo, qn, kn, p, fx, tk):
    bsz, s, _ = qn.shape
    tq = 2 * tk
    grid_spec = pltpu.PrefetchScalarGridSpec(
        num_scalar_prefetch=1,
        grid=(bsz, HEADS, s // tq),
        in_specs=[pl.BlockSpec((1, tq, HEAD_DIM), lambda b, h, i, lo_r: (b, i, h)),
                  pl.BlockSpec((1, s, HEAD_DIM), lambda b, h, i, lo_r: (b, 0, h)),
                  pl.BlockSpec((1, s, HEAD_DIM), lambda b, h, i, lo_r: (b, 0, 2 * HEADS + h)),
                  pl.BlockSpec((1, s, GATE_COLS), lambda b, h, i, lo_r: (b, 0, 0))],
        out_specs=pl.BlockSpec((1, tq, HEAD_DIM), lambda b, h, i, lo_r: (b, i, h)),
        scratch_shapes=[pltpu.VMEM((HEAD_DIM, tq), F32), pltpu.VMEM((2, tk, tq), F32)],
    )
    return pl.pallas_call(
        functools.partial(_fox_kernel, tk=tk),
        grid_spec=grid_spec,
        out_shape=jax.ShapeDtypeStruct((bsz, s, WIDTH), BF16),
        compiler_params=_params("arbitrary", "arbitrary", "arbitrary"),
        name="fox_attention",
    )(lo, qn, kn, p, fx)


def _fox_block_start(ft, qg, kg, tq):
    bsz, _, s = ft.shape
    f_first = ft[:, :, 0::tq]
    f_last = ft[:, :, tq - 1::tq]
    qk_bound = 1.02 * (HEAD_DIM ** 0.5) * jnp.max(jnp.abs(qg)) * jnp.max(jnp.abs(kg))
    thresh = EXP_UNDERFLOW + 2.0 * qk_bound
    skip = f_last[:, :, None, :] > f_first[:, :, :, None] + thresh
    return jnp.sum(skip, axis=-1).astype(jnp.int32).reshape(bsz * HEADS, s // tq)


def _gdnprep_kernel(q_ref, k_ref, v_ref, qh_ref, kh_ref, vh_ref, cw_ref, sm_ref, ex_ref, sel_ref,
                    qo_ref, ko_ref, kbo_ref, vbo_ref, kbeo_ref, qeo_ref, kdo_ref, eglo_ref, *, tb):
    first = pl.program_id(1) == 0

    def conv_silu(src, halo, which):
        ext = jnp.concatenate([jnp.where(first, 0.0, halo[0].astype(F32)), src[0].astype(F32)], axis=0)
        acc = jnp.zeros((tb, WIDTH), F32)
        for t in range(CONV_WIDTH):
            w_row = cw_ref[t:t + 1, which * WIDTH:(which + 1) * WIDTH]
            back = CONV_WIDTH - 1 - t
            shifted = pltpu.roll(ext, back, axis=0) if back else ext
            acc = acc + shifted[SUBLANES:, :] * w_row
        return _silu(acc)

    def spread(group):
        e = ex_ref[group].astype(BF16)
        return _dot(sm2, jnp.concatenate([e, e], axis=0))

    def l2norm(t):
        parts = []
        for h in range(HEADS):
            th = t[:, h * HEAD_DIM:(h + 1) * HEAD_DIM]
            parts.append(th * lax.rsqrt(jnp.sum(th * th, axis=-1, keepdims=True) + EPS))
        return jnp.concatenate(parts, axis=-1)

    sm = sm_ref[0]
    sm_hi = sm.astype(BF16)
    sm2 = jnp.concatenate([sm_hi, (sm - sm_hi.astype(F32)).astype(BF16)], axis=1)
    beta_x = spread(0)
    egc_x = spread(1)
    edec_x = spread(2)
    eglo_ref[0] = _dot(_dot(sel_ref[...], sm, HIGHEST), ex_ref[3], HIGHEST)

    k = l2norm(conv_silu(k_ref, kh_ref, 1))
    kb = k * beta_x
    ko_ref[0] = k.astype(BF16)
    kbo_ref[0] = kb.astype(BF16)
    kbeo_ref[0] = (kb * egc_x).astype(BF16)
    kdo_ref[0] = (k * edec_x).astype(BF16)
    q = l2norm(conv_silu(q_ref, qh_ref, 0)) * (HEAD_DIM ** -0.5)
    qo_ref[0] = q.astype(BF16)
    qeo_ref[0] = (q * egc_x).astype(BF16)
    v = conv_silu(v_ref, vh_ref, 2)
    vbo_ref[0] = (v * beta_x).astype(BF16)


def _gdnprep(p, conv_w, sm, expand, sel):
    bsz, s, _ = p.shape
    tb = min(ROW_BLOCK, s)
    nchunk = tb // CHUNK
    blk = lambda j: pl.BlockSpec((1, tb, WIDTH), lambda b, i: (b, i, j))
    halo = lambda j: pl.BlockSpec((1, SUBLANES, WIDTH),
                                  lambda b, i: (b, jnp.maximum(i * (tb // SUBLANES) - 1, 0), j))
    shape = jax.ShapeDtypeStruct((bsz, s, WIDTH), BF16)
    return pl.pallas_call(
        functools.partial(_gdnprep_kernel, tb=tb),
        grid=(bsz, s // tb),
        in_specs=[blk(4), blk(5), blk(6), halo(4), halo(5), halo(6),
                  pl.BlockSpec((CONV_WIDTH, 3 * WIDTH), lambda b, i: (0, 0)),
                  pl.BlockSpec((1, tb, GATE_COLS), lambda b, i: (b, i, 0)),
                  pl.BlockSpec((4, GATE_COLS, WIDTH), lambda b, i: (0, 0, 0)),
                  pl.BlockSpec((nchunk, tb), lambda b, i: (0, 0))],
        out_specs=[blk(0)] * 7 + [pl.BlockSpec((1, nchunk, WIDTH), lambda b, i: (b, i, 0))],
        out_shape=[shape] * 7 + [jax.ShapeDtypeStruct((bsz, s // CHUNK, WIDTH), F32)],
        compiler_params=_params("arbitrary", "arbitrary"),
        name="gdn_prep",
    )(p, p, p, p, p, p, conv_w, sm, expand, sel)


def _gdn_kernel(q_ref, k_ref, kb_ref, vb_ref, kbe_ref, ga_ref, gb_ref, qe_ref, kd_ref, egl_ref,
                o_ref, s_ref, u_ref, w_ref, attn_ref, *, nchunk, nblk):
    i = pl.program_id(1)

    @pl.when(i == 0)
    def _():
        s_ref[...] = jnp.zeros_like(s_ref)

    r = lax.broadcasted_iota(jnp.int32, (CHUNK, CHUNK), 0)
    c = lax.broadcasted_iota(jnp.int32, (CHUNK, CHUNK), 1)
    lower = r >= c
    strict = r > c
    eye = (r == c).astype(F32)
    diag8 = (r // NEUMANN_BLOCK) == (c // NEUMANN_BLOCK)
    widths = [NEUMANN_BLOCK << lvl for lvl in range((CHUNK // NEUMANN_BLOCK).bit_length() - 1)]
    levels = [((r // (2 * w)) == (c // (2 * w))) & ((r // w) != (c // w)) for w in widths]
    gcol = lax.broadcasted_iota(jnp.int32, (CHUNK, GATE_COLS), 1)
    heads = range(HEADS)
    hsl = [slice(h * HEAD_DIM, (h + 1) * HEAD_DIM) for h in heads]
    gc_groups = (COL_A, COL_A2, COL_A3, COL_A4, COL_A5, COL_A6)
    head_cols = [functools.reduce(lambda a, b: a | b, [gcol == grp + h for grp in gc_groups]) for h in heads]

    unroll = range(SOLVE_UNROLL)

    def chunk_rows(it, t):
        return pl.ds(pl.multiple_of((it * SOLVE_UNROLL + t) * CHUNK, CHUNK), CHUNK)

    def solve_stages(it):
        probs = [(t, h) for t in unroll for h in heads]
        n = range(len(probs))
        rows = [chunk_rows(it, t) for t in unroll]
        ga = [ga_ref[0, rows[t], :] for t in unroll]
        gb = [gb_ref[0, rows[t], :] for t in unroll]
        gbh = [jnp.where(head_cols[h], gb[t], jnp.zeros((), BF16)) for t, h in probs]
        dlog = [_dot_nt(ga[t], gbh[p]) for p, (t, h) in enumerate(probs)]
        kk = [_dot_nt(kb_ref[0, rows[t], hsl[h]], k_ref[0, rows[t], hsl[h]]) for t, h in probs]
        qk = [_dot_nt(q_ref[0, rows[t], hsl[h]], k_ref[0, rows[t], hsl[h]]) for t, h in probs]
        yield
        decay = [jnp.exp(jnp.where(lower, dlog[p], -jnp.inf)) for p in n]
        m = [jnp.where(strict, kk[p] * decay[p], 0.0) for p in n]
        attn = [(qk[p] * decay[p]).astype(BF16) for p in n]
        md = [jnp.where(diag8, m[p], 0.0).astype(BF16) for p in n]
        m2 = [_dot(md[p], md[p]).astype(BF16) for p in n]
        yield
        inv = [eye - md[p].astype(F32) for p in n]
        m4 = [_dot(m2[p], m2[p]).astype(BF16) for p in n]
        inv = [inv[p] + _dot(inv[p].astype(BF16), m2[p]) for p in n]
        yield
        inv = [inv[p] + _dot(inv[p].astype(BF16), m4[p]) for p in n]
        yield
        for lvl in levels:
            off = [jnp.where(lvl, m[p], 0.0).astype(BF16) for p in n]
            invb = [inv[p].astype(BF16) for p in n]
            x = [_dot(invb[p], off[p]).astype(BF16) for p in n]
            yield
            inv = [inv[p] - _dot(x[p], invb[p]) for p in n]
            yield
        uw = [_dot(inv[p].astype(BF16),
                   jnp.concatenate([vb_ref[0, rows[t], hsl[h]], kbe_ref[0, rows[t], hsl[h]]], axis=-1))
              for p, (t, h) in enumerate(probs)]
        yield
        for p, (t, h) in enumerate(probs):
            attn_ref[rows[t], h * HEAD_DIM:h * HEAD_DIM + CHUNK] = attn[p]
            u_ref[rows[t], hsl[h]] = uw[p][:, :HEAD_DIM]
            w_ref[rows[t], hsl[h]] = uw[p][:, HEAD_DIM:].astype(BF16)

    def scan_stages(it):
        state = [s_ref[h] for h in heads]
        for t in unroll:
            rows = chunk_rows(it, t)
            egl = egl_ref[0, pl.ds(it * SOLVE_UNROLL + t, 1), :]
            u = [u_ref[rows, hsl[h]] for h in heads]
            attn = [attn_ref[rows, h * HEAD_DIM:h * HEAD_DIM + CHUNK] for h in heads]
            ws = [_dot(jnp.concatenate([w_ref[rows, hsl[h]], qe_ref[0, rows, hsl[h]]], axis=0),
                       state[h].astype(BF16)) for h in heads]
            yield
            v_new = [(u[h] - ws[h][:CHUNK]).astype(BF16) for h in heads]
            for h in heads:
                o_ref[0, rows, hsl[h]] = (ws[h][CHUNK:] + _dot(attn[h], v_new[h])).astype(BF16)
            state = [state[h] * egl[:, hsl[h]] + _dot_tn(kd_ref[0, rows, hsl[h]], v_new[h]) for h in heads]
            yield
        for h in heads:
            s_ref[h] = state[h]

    def drain(gen):
        for _ in gen:
            pass

    def solve_only(it, carry):
        drain(solve_stages(it))
        return carry

    def scan_only(it, carry):
        drain(scan_stages(it))
        return carry

    def solve_and_scan(it, carry):
        scan = scan_stages(it)
        solve = solve_stages(it)
        solve_yields = 5 + 2 * len(levels)
        scan_yields = 2 * SOLVE_UNROLL
        for stage in range(solve_yields):
            next(solve)
            if (stage * scan_yields) // solve_yields != ((stage + 1) * scan_yields) // solve_yields:
                next(scan, None)
        drain(scan)
        drain(solve)
        return carry

    iters = nchunk // SOLVE_UNROLL

    @pl.when(i == 0)
    def _():
        lax.fori_loop(0, iters, solve_only, 0)

    @pl.when((i > 0) & (i < nblk))
    def _():
        lax.fori_loop(0, iters, solve_and_scan, 0)

    @pl.when(i == nblk)
    def _():
        lax.fori_loop(0, iters, scan_only, 0)


def _gdn(q, k, kb, vb, kbe, qe, kd, ga, gb, egl):
    bsz, s, _ = q.shape
    tb = min(ROW_BLOCK, s)
    nchunk = tb // CHUNK
    nblk = s // tb
    cur = lambda b, i: (b, jnp.minimum(i, nblk - 1), 0)
    prev = lambda b, i: (b, jnp.maximum(i - 1, 0), 0)
    return pl.pallas_call(
        functools.partial(_gdn_kernel, nchunk=nchunk, nblk=nblk),
        grid=(bsz, nblk + 1),
        in_specs=[pl.BlockSpec((1, tb, WIDTH), cur)] * 5 + [pl.BlockSpec((1, tb, GATE_COLS), cur)] * 2
                 + [pl.BlockSpec((1, tb, WIDTH), prev)] * 2 + [pl.BlockSpec((1, nchunk, WIDTH), prev)],
        out_specs=pl.BlockSpec((1, tb, WIDTH), prev),
        out_shape=jax.ShapeDtypeStruct((bsz, s, WIDTH), BF16),
        scratch_shapes=[pltpu.VMEM((HEADS, HEAD_DIM, HEAD_DIM), F32),
                        pltpu.VMEM((tb, WIDTH), F32), pltpu.VMEM((tb, WIDTH), BF16),
                        pltpu.VMEM((tb, WIDTH), BF16)],
        compiler_params=_params("arbitrary", "arbitrary"),
        name="gdn_delta_rule",
    )(q, k, kb, vb, kbe, ga, gb, qe, kd, egl)


def _out_kernel(x_ref, fo_ref, fz_ref, go_ref, gz_ref, gate_ref, gng_ref, w_ref, fg_ref, o_ref, *, final_norm):
    a = fo_ref[0].astype(F32) * _silu(fz_ref[0].astype(F32))
    parts = []
    for h in range(HEADS):
        sl = slice(h * HEAD_DIM, (h + 1) * HEAD_DIM)
        t = go_ref[0, :, sl].astype(F32)
        ms = jnp.mean(t * t, axis=-1, keepdims=True)
        parts.append(t * lax.rsqrt(ms + EPS) * gng_ref[...])
    g = jnp.concatenate(parts, axis=-1) * _silu(gz_ref[0].astype(F32))
    y = _dot(a.astype(BF16), w_ref[0:WIDTH, :]) + _dot(g.astype(BF16), w_ref[WIDTH:2 * WIDTH, :])
    xn = x_ref[0] + gate_ref[0] * y
    if final_norm:
        ms = jnp.mean(xn * xn, axis=-1, keepdims=True)
        xn = xn * lax.rsqrt(ms + EPS) * fg_ref[...]
    o_ref[0] = xn


def _out(x, fox_o, p, gdn_o, gate, gng, w_out, final_g, final_norm):
    bsz, s, d = x.shape
    tm = min(ROW_BLOCK, s)
    blk = lambda j: pl.BlockSpec((1, tm, WIDTH), lambda b, i: (b, i, j))
    return pl.pallas_call(
        functools.partial(_out_kernel, final_norm=final_norm),
        grid=(bsz, s // tm),
        in_specs=[pl.BlockSpec((1, tm, d), lambda b, i: (b, i, 0)),
                  blk(0), blk(3), blk(0), blk(7),
                  pl.BlockSpec((1, 1, d), lambda b, i: (b, 0, 0)),
                  pl.BlockSpec((1, HEAD_DIM), lambda b, i: (0, 0)),
                  pl.BlockSpec((2 * WIDTH, d), lambda b, i: (0, 0)),
                  pl.BlockSpec((1, d), lambda b, i: (0, 0))],
        out_specs=pl.BlockSpec((1, tm, d), lambda b, i: (b, i, 0)),
        out_shape=jax.ShapeDtypeStruct((bsz, s, d), F32),
        compiler_params=_params("arbitrary", "arbitrary"),
        name="gate_out_proj",
    )(x, fox_o, p, gdn_o, p, gate, gng, w_out, final_g)


def _expand_matrices():
    lane_head = jnp.arange(WIDTH) // HEAD_DIM
    rows = jnp.arange(GATE_COLS)[:, None]
    mats = [(rows == grp + lane_head[None, :]).astype(F32) for grp in (COL_B, COL_A, COL_A2, COL_A3)]
    return jnp.stack(mats)


def _chunk_last_selector(tb):
    nchunk = tb // CHUNK
    return (jnp.arange(tb)[None, :] == (jnp.arange(nchunk)[:, None] * CHUNK + CHUNK - 1)).astype(F32)


def _split_w_in(w):
    fw = WIDTH
    o_ff = 4 * fw
    o_g = o_ff + HEADS
    o_ga = o_g + 4 * fw
    o_gb = o_ga + HEADS
    w_main = jnp.concatenate([w[:, :o_ff], w[:, o_g:o_ga]], axis=1).astype(BF16)
    cols = {"f": w[:, o_ff:o_g], "a": w[:, o_ga:o_gb], "b": w[:, o_gb:o_gb + HEADS]}
    pad = jnp.zeros((w.shape[0], GATE_COLS - len(GATE_GROUPS) * HEADS), w.dtype)
    w_small = jnp.concatenate([cols[t] for t in GATE_GROUPS] + [pad], axis=1).astype(BF16)
    return w_main, w_small


def _gate_rows(b_f, dt_bias, a_log):
    z = jnp.zeros((HEADS,), F32)
    pad = jnp.zeros((GATE_COLS - len(GATE_GROUPS) * HEADS,), F32)
    add = {"f": b_f, "a": dt_bias, "b": z}
    alog = {"f": z, "a": a_log, "b": z}
    add_row = jnp.concatenate([add[t] for t in GATE_GROUPS] + [pad]).reshape(1, GATE_COLS)
    alog_row = jnp.concatenate([alog[t] for t in GATE_GROUPS] + [pad]).reshape(1, GATE_COLS)
    return add_row.astype(F32), alog_row.astype(F32)


def kernel(x, c, norm_g, w_ada, b_ada, w_in, b_fgate, fox_qn_g, fox_kn_g, gdn_conv_w, gdn_A_log,
           gdn_dt_bias, gdn_norm_g, w_out, final_g):
    bsz, s, d = x.shape
    depth = w_in.shape[0]
    expand = _expand_matrices()
    sel = _chunk_last_selector(min(ROW_BLOCK, s))
    tk = min(FOX_KEYS, s // 2)
    for l in range(depth):
        mod = _ada(c, w_ada[l], b_ada[l])
        shift, scale, gate = (mod[:, k * d:(k + 1) * d].reshape(bsz, 1, d) for k in range(3))
        w_main, w_small = _split_w_in(w_in[l])
        p, ps = _proj(x, shift, scale, norm_g[l].reshape(1, d), w_main, w_small)

        add_row, alog_row = _gate_rows(b_fgate[l], gdn_dt_bias[l], gdn_A_log[l])
        ft, fx, ga, gb, sm = _gates(ps, add_row, alog_row)

        qg = fox_qn_g[l].reshape(1, HEAD_DIM)
        kg = fox_kn_g[l].reshape(1, HEAD_DIM)
        qn, kn = _foxprep(p, qg, kg)
        lo = _fox_block_start(ft, qg, kg, tk)
        fox_o = _fox(lo, qn, kn, p, fx, tk)

        gq, gk, gkb, gvb, gkbe, gqe, gkd, egl = _gdnprep(p, gdn_conv_w[l], sm, expand, sel)
        gdn_o = _gdn(gq, gk, gkb, gvb, gkbe, gqe, gkd, ga, gb, egl)

        x = _out(x, fox_o, p, gdn_o, gate, gdn_norm_g[l].reshape(1, HEAD_DIM), w_out[l].astype(BF16),
                 final_g.reshape(1, d), final_norm=(l == depth - 1))
    return x
```

```python
import functools

import jax
import jax.numpy as jnp
from jax import lax
from jax.experimental import pallas as pl
from jax.experimental.pallas import tpu as pltpu

F32 = jnp.float32
BF16 = jnp.bfloat16
HIGHEST = lax.Precision.HIGHEST

HEADS = 8
HEAD_DIM = 128
WIDTH = HEADS * HEAD_DIM
CHUNK = 64
NEUMANN_BLOCK = 8
CONV_WIDTH = 4
EPS = 1e-6
GATE_COLS = 128
COL_F, COL_A, COL_B, COL_A2, COL_A3, COL_F2, COL_F3, COL_A4, COL_A5, COL_A6 = 0, 8, 16, 24, 32, 40, 48, 56, 64, 72
GATE_GROUPS = "fabaaffaaa"
LOG2E = 1.4426950408889634
FOX_UNROLL = 4
SOLVE_UNROLL = 4
EXP_UNDERFLOW = 104.0

SUBLANES = 8
PROJ_ROWS = 1024
PROJ_COLS = 2 * WIDTH
NORM_ROWS = 128
ADA_COLS = 512
GATE_ROWS = 256
ROW_BLOCK = 512
FOX_KEYS = 512
VMEM_LIMIT = 52 * 1024 * 1024

NT_DIMS = (((1,), (1,)), ((), ()))
TN_DIMS = (((0,), (0,)), ((), ()))


def _dot(a, b, precision=None):
    return jnp.dot(a, b, preferred_element_type=F32, precision=precision)


def _dot_nt(a, b, precision=None):
    return lax.dot_general(a, b, NT_DIMS, preferred_element_type=F32, precision=precision)


def _dot_tn(a, b):
    return lax.dot_general(a, b, TN_DIMS, preferred_element_type=F32)


def _silu(x):
    return x * jax.nn.sigmoid(x)


def _params(*sem):
    return pltpu.CompilerParams(dimension_semantics=sem, vmem_limit_bytes=VMEM_LIMIT)


def _ada_kernel(c_ref, w_ref, b_ref, o_ref):
    o_ref[...] = _dot(_silu(c_ref[...]), w_ref[...], HIGHEST) + b_ref[...]


def _ada(c, w, b):
    bsz, d = c.shape
    n = w.shape[1]
    rows = SUBLANES
    tn = ADA_COLS if n % ADA_COLS == 0 else HEAD_DIM
    cp = jnp.pad(c, ((0, rows - bsz), (0, 0)))
    out = pl.pallas_call(
        _ada_kernel,
        grid=(n // tn,),
        in_specs=[pl.BlockSpec((rows, d), lambda j: (0, 0)),
                  pl.BlockSpec((d, tn), lambda j: (0, j)),
                  pl.BlockSpec((1, tn), lambda j: (0, j))],
        out_specs=pl.BlockSpec((rows, tn), lambda j: (0, j)),
        out_shape=jax.ShapeDtypeStruct((rows, n), F32),
        compiler_params=_params("arbitrary"),
        name="ada_mod",
    )(cp, w, b.reshape(1, n))
    return out[:bsz]


def _proj_kernel(x_ref, sh_ref, sc_ref, g_ref, w_ref, ws_ref, p_ref, ps_ref, h_ref, *, tm, rows):
    @pl.when(pl.program_id(2) == 0)
    def _():
        gmul = g_ref[...] * (1.0 + sc_ref[0])
        shift = sh_ref[0]

        def body(r, carry):
            sl = pl.ds(pl.multiple_of(r * rows, rows), rows)
            xs = x_ref[0, sl, :]
            ms = jnp.mean(xs * xs, axis=-1, keepdims=True)
            h_ref[sl, :] = (xs * lax.rsqrt(ms + EPS) * gmul + shift).astype(BF16)
            return carry

        lax.fori_loop(0, tm // rows, body, 0)
        ps_ref[0] = _dot(h_ref[...], ws_ref[...])

    p_ref[0] = _dot(h_ref[...], w_ref[...]).astype(BF16)


def _proj(x, shift, scale, g, w_main, w_small):
    bsz, s, d = x.shape
    n = w_main.shape[1]
    tm = min(PROJ_ROWS, s)
    tn = PROJ_COLS
    kern = functools.partial(_proj_kernel, tm=tm, rows=min(NORM_ROWS, tm))
    return pl.pallas_call(
        kern,
        grid=(bsz, s // tm, n // tn),
        in_specs=[pl.BlockSpec((1, tm, d), lambda b, i, j: (b, i, 0)),
                  pl.BlockSpec((1, 1, d), lambda b, i, j: (b, 0, 0)),
                  pl.BlockSpec((1, 1, d), lambda b, i, j: (b, 0, 0)),
                  pl.BlockSpec((1, d), lambda b, i, j: (0, 0)),
                  pl.BlockSpec((d, tn), lambda b, i, j: (0, j)),
                  pl.BlockSpec((d, GATE_COLS), lambda b, i, j: (0, 0))],
        out_specs=[pl.BlockSpec((1, tm, tn), lambda b, i, j: (b, i, j)),
                   pl.BlockSpec((1, tm, GATE_COLS), lambda b, i, j: (b, i, 0))],
        out_shape=[jax.ShapeDtypeStruct((bsz, s, n), BF16),
                   jax.ShapeDtypeStruct((bsz, s, GATE_COLS), F32)],
        scratch_shapes=[pltpu.VMEM((tm, d), BF16)],
        compiler_params=_params("arbitrary", "arbitrary", "arbitrary"),
        name="norm_in_proj",
    )(x, shift, scale, g, w_main, w_small)


def _gates_kernel(ps_ref, add_ref, alog_ref, ft_ref, fx_ref, ga_ref, gb_ref, sm_ref, carry_ref, *, tb):
    @pl.when(pl.program_id(1) == 0)
    def _():
        carry_ref[...] = jnp.zeros_like(carry_ref)

    x = ps_ref[0] + add_ref[...]
    col = lax.broadcasted_iota(jnp.int32, (tb, GATE_COLS), 1)
    grp = lambda start: (col >= start) & (col < start + HEADS)
    is_f = grp(COL_F) | grp(COL_F2) | grp(COL_F3)
    is_a = grp(COL_A) | grp(COL_A2) | grp(COL_A3) | grp(COL_A4) | grp(COL_A5) | grp(COL_A6)
    log_f = jax.nn.log_sigmoid(x)
    g = -jnp.exp(alog_ref[...]) * jax.nn.softplus(x)
    beta = jax.nn.sigmoid(x)
    vals = jnp.where(is_f, log_f, jnp.where(is_a, g, 0.0))

    r = lax.broadcasted_iota(jnp.int32, (tb, tb), 0)
    c = lax.broadcasted_iota(jnp.int32, (tb, tb), 1)
    same_chunk = (r // CHUNK) == (c // CHUNK)
    sums = jnp.concatenate([(r >= c).astype(F32),
                            ((r >= c) & same_chunk).astype(F32),
                            same_chunk.astype(F32)], axis=0).astype(BF16)

    def split3(t):
        hi = t.astype(BF16).astype(F32)
        mid = (t - hi).astype(BF16).astype(F32)
        return hi, mid, t - hi - mid

    pieces = _dot(sums, jnp.concatenate(split3(vals), axis=1).astype(BF16))
    summed = pieces[:, :GATE_COLS] + pieces[:, GATE_COLS:2 * GATE_COLS] + pieces[:, 2 * GATE_COLS:]
    carry = carry_ref[0:1, :]
    cs_all = summed[0:tb] + carry
    cs = summed[tb:2 * tb]
    tot = summed[2 * tb:]
    carry_ref[...] = jnp.broadcast_to(carry + jnp.sum(vals, axis=0, keepdims=True), carry_ref.shape)

    ft_ref[0] = jnp.transpose(cs_all)[0:HEADS, :]
    f_hi, f_mid, f_lo = split3(cs_all * (-LOG2E))
    fx_ref[0] = jnp.where(grp(COL_F), f_hi, jnp.where(grp(COL_F2), f_mid,
                                                      jnp.where(grp(COL_F3), f_lo, 0.0))).astype(BF16)
    c_hi, c_mid, c_lo = split3(cs)
    gc_cols = grp(COL_A) | grp(COL_A2) | grp(COL_A5)
    ones = grp(COL_A3) | grp(COL_A4) | grp(COL_A6)
    ga_ref[0] = jnp.where(grp(COL_A), c_hi, jnp.where(grp(COL_A2), c_mid, jnp.where(grp(COL_A5), c_lo,
                                                                                      jnp.where(ones, 1.0, 0.0)))).astype(BF16)
    gb_ref[0] = jnp.where(gc_cols, 1.0,
                          jnp.where(grp(COL_A3), -c_hi, jnp.where(grp(COL_A4), -c_mid,
                                                                  jnp.where(grp(COL_A6), -c_lo, 0.0)))).astype(BF16)
    sm_ref[0] = jnp.where(grp(COL_A), jnp.exp(cs),
                          jnp.where(grp(COL_B), beta,
                                    jnp.where(grp(COL_A2), jnp.exp(tot - cs),
                                              jnp.where(grp(COL_A3), jnp.exp(tot), 0.0))))


def _gates(ps, add_row, alog_row):
    bsz, s, _ = ps.shape
    tb = min(GATE_ROWS, s)
    small = pl.BlockSpec((1, tb, GATE_COLS), lambda b, i: (b, i, 0))
    row = pl.BlockSpec((1, GATE_COLS), lambda b, i: (0, 0))
    small_shape = jax.ShapeDtypeStruct((bsz, s, GATE_COLS), F32)
    bf16_shape = jax.ShapeDtypeStruct((bsz, s, GATE_COLS), BF16)
    return pl.pallas_call(
        functools.partial(_gates_kernel, tb=tb),
        grid=(bsz, s // tb),
        in_specs=[small, row, row],
        out_specs=[pl.BlockSpec((1, HEADS, tb), lambda b, i: (b, 0, i)), small, small, small, small],
        out_shape=[jax.ShapeDtypeStruct((bsz, HEADS, s), F32), bf16_shape, bf16_shape, bf16_shape, small_shape],
        scratch_shapes=[pltpu.VMEM((SUBLANES, GATE_COLS), F32)],
        compiler_params=_params("arbitrary", "arbitrary"),
        name="gates",
    )(ps, add_row, alog_row)


def _foxprep_kernel(q_ref, k_ref, qg_ref, kg_ref, qo_ref, ko_ref):
    def norm(src, gain, dst):
        for h in range(HEADS):
            sl = slice(h * HEAD_DIM, (h + 1) * HEAD_DIM)
            t = src[0, :, sl].astype(F32)
            ms = jnp.mean(t * t, axis=-1, keepdims=True)
            dst[0, :, sl] = (t * lax.rsqrt(ms + EPS) * gain).astype(BF16)

    norm(q_ref, qg_ref[...] * (HEAD_DIM ** -0.5 * LOG2E), qo_ref)
    norm(k_ref, kg_ref[...], ko_ref)


def _foxprep(p, qg, kg):
    bsz, s, _ = p.shape
    tb = min(ROW_BLOCK, s)
    blk = lambda j: pl.BlockSpec((1, tb, WIDTH), lambda b, i: (b, i, j))
    row = pl.BlockSpec((1, HEAD_DIM), lambda b, i: (0, 0))
    shape = jax.ShapeDtypeStruct((bsz, s, WIDTH), BF16)
    return pl.pallas_call(
        _foxprep_kernel,
        grid=(bsz, s // tb),
        in_specs=[blk(0), blk(1), row, row],
        out_specs=[blk(0), blk(0)],
        out_shape=[shape, shape],
        compiler_params=_params("arbitrary", "arbitrary"),
        name="fox_qk_norm",
    )(p, p, qg, kg)


def _fox_kernel(lo_ref, q_ref, k_ref, v_ref, fx_ref, o_ref, acc_ref, st_ref, *, tk):
    b = pl.program_id(0)
    h = pl.program_id(1)
    g = pl.program_id(2)
    tq = 2 * tk
    lo = lo_ref[b * HEADS + h, 2 * g]
    lane = lax.broadcasted_iota(jnp.int32, (tq, GATE_COLS), 1)
    pick = (lane == COL_F + h) | (lane == COL_F2 + h) | (lane == COL_F3 + h)
    qa = jnp.concatenate([q_ref[0], jnp.where(pick, 1.0, 0.0).astype(BF16)], axis=1)
    acc_ref[...] = jnp.zeros_like(acc_ref)
    late = slice(tk, tq)

    def key_rows(j):
        return pl.ds(pl.multiple_of(j * tk, tk), tk)

    def scores(j, queries=slice(None)):
        ka = jnp.concatenate([k_ref[0, key_rows(j), :], fx_ref[0, key_rows(j), :]], axis=1)
        return _dot_nt(ka, qa[queries])

    def causal(st):
        r = lax.broadcasted_iota(jnp.int32, st.shape, 0)
        c = lax.broadcasted_iota(jnp.int32, st.shape, 1)
        return jnp.where(r <= c, st, -jnp.inf)

    def issue(j, kind, slot):
        if kind == "full":
            st_ref[slot] = scores(j)
        elif kind == "diag":
            st_ref[slot] = causal(scores(j))
        else:
            st_ref[slot, :, late] = causal(scores(j, late))

    def step(cur_j, cur_slot, carry, nxt=None, only_late=False):
        if nxt is not None:
            issue(nxt[0], nxt[1], 1 - cur_slot)
        cols = late if only_late else slice(None)
        m_all, l_all = carry
        m, l = m_all[:, cols], l_all[:, cols]
        st = st_ref[cur_slot, :, cols]
        m_new = jnp.maximum(m, jnp.max(st, axis=0, keepdims=True))
        alpha = jnp.exp2(m - m_new)
        p = jnp.exp2(st - m_new)
        l_new = alpha * l + jnp.sum(p, axis=0, keepdims=True)
        pv = _dot_tn(v_ref[0, key_rows(cur_j), :], p.astype(BF16))
        acc_ref[:, cols] = alpha * acc_ref[:, cols] + pv
        if only_late:
            m_new = jnp.concatenate([m_all[:, :tk], m_new], axis=1)
            l_new = jnp.concatenate([l_all[:, :tk], l_new], axis=1)
        return m_new, l_new

    d = 2 * g
    n = d - lo

    @pl.when(n == 0)
    def _():
        issue(d, "diag", 0)

    @pl.when(n > 0)
    def _():
        issue(lo, "full", 0)

    def interior_steps(t, carry):
        j = lo + FOX_UNROLL * t
        for u in range(FOX_UNROLL):
            carry = step(j + u, u % 2, carry, nxt=(j + u + 1, "full"))
        return carry

    init = (jnp.full((1, tq), -jnp.inf, F32), jnp.zeros((1, tq), F32))
    loops = jnp.maximum(n - 1, 0) // FOX_UNROLL
    carry = lax.fori_loop(0, loops, interior_steps, init)

    def finish(interior_left):
        def run(carry):
            slot = 0
            for back in range(interior_left, 0, -1):
                carry = step(d - back, slot, carry, nxt=(d - back + 1, "full" if back > 1 else "diag"))
                slot = 1 - slot
            carry = step(d, slot, carry, nxt=(d + 1, "late"))
            return step(d + 1, 1 - slot, carry, only_late=True)
        return run

    left = n - FOX_UNROLL * loops
    m, l = lax.switch(left, [finish(k) for k in range(FOX_UNROLL + 1)], carry)
    o_ref[0] = jnp.transpose(acc_ref[...] / l).astype(BF16)


def _fox(lo, qn, kn, p, fx, tk):
    bsz, s, _ = qn.shape
    tq = 2 * tk
    grid_spec = pltpu.PrefetchScalarGridSpec(
        num_scalar_prefetch=1,
        grid=(bsz, HEADS, s // tq),
        in_specs=[pl.BlockSpec((1, tq, HEAD_DIM), lambda b, h, i, lo_r: (b, i, h)),
                  pl.BlockSpec((1, s, HEAD_DIM), lambda b, h, i, lo_r: (b, 0, h)),
                  pl.BlockSpec((1, s, HEAD_DIM), lambda b, h, i, lo_r: (b, 0, 2 * HEADS + h)),
                  pl.BlockSpec((1, s, GATE_COLS), lambda b, h, i, lo_r: (b, 0, 0))],
        out_specs=pl.BlockSpec((1, tq, HEAD_DIM), lambda b, h, i, lo_r: (b, i, h)),
        scratch_shapes=[pltpu.VMEM((HEAD_DIM, tq), F32), pltpu.VMEM((2, tk, tq), F32)],
    )
    return pl.pallas_call(
        functools.partial(_fox_kernel, tk=tk),
        grid_spec=grid_spec,
        out_shape=jax.ShapeDtypeStruct((bsz, s, WIDTH), BF16),
        compiler_params=_params("arbitrary", "arbitrary", "arbitrary"),
        name="fox_attention",
    )(lo, qn, kn, p, fx)


def _fox_block_start(ft, qg, kg, tq):
    bsz, _, s = ft.shape
    f_first = ft[:, :, 0::tq]
    f_last = ft[:, :, tq - 1::tq]
    qk_bound = 1.02 * (HEAD_DIM ** 0.5) * jnp.max(jnp.abs(qg)) * jnp.max(jnp.abs(kg))
    thresh = EXP_UNDERFLOW + 2.0 * qk_bound
    skip = f_last[:, :, None, :] > f_first[:, :, :, None] + thresh
    return jnp.sum(skip, axis=-1).astype(jnp.int32).reshape(bsz * HEADS, s // tq)


def _gdnprep_kernel(q_ref, k_ref, v_ref, qh_ref, kh_ref, vh_ref, cw_ref, sm_ref, ex_ref, sel_ref,
                    qo_ref, ko_ref, kbo_ref, vbo_ref, kbeo_ref, qeo_ref, kdo_ref, eglo_ref, *, tb):
    first = pl.program_id(1) == 0

    def conv_silu(src, halo, which):
        ext = jnp.concatenate([jnp.where(first, 0.0, halo[0].astype(F32)), src[0].astype(F32)], axis=0)
        acc = jnp.zeros((tb, WIDTH), F32)
        for t in range(CONV_WIDTH):
            w_row = cw_ref[t:t + 1, which * WIDTH:(which + 1) * WIDTH]
            back = CONV_WIDTH - 1 - t
            shifted = pltpu.roll(ext, back, axis=0) if back else ext
            acc = acc + shifted[SUBLANES:, :] * w_row
        return _silu(acc)

    def spread(group):
        e = ex_ref[group].astype(BF16)
        return _dot(sm2, jnp.concatenate([e, e], axis=0))

    def l2norm(t):
        parts = []
        for h in range(HEADS):
            th = t[:, h * HEAD_DIM:(h + 1) * HEAD_DIM]
            parts.append(th * lax.rsqrt(jnp.sum(th * th, axis=-1, keepdims=True) + EPS))
        return jnp.concatenate(parts, axis=-1)

    sm = sm_ref[0]
    sm_hi = sm.astype(BF16)
    sm2 = jnp.concatenate([sm_hi, (sm - sm_hi.astype(F32)).astype(BF16)], axis=1)
    beta_x = spread(0)
    egc_x = spread(1)
    edec_x = spread(2)
    eglo_ref[0] = _dot(_dot(sel_ref[...], sm, HIGHEST), ex_ref[3], HIGHEST)

    k = l2norm(conv_silu(k_ref, kh_ref, 1))
    kb = k * beta_x
    ko_ref[0] = k.astype(BF16)
    kbo_ref[0] = kb.astype(BF16)
    kbeo_ref[0] = (kb * egc_x).astype(BF16)
    kdo_ref[0] = (k * edec_x).astype(BF16)
    q = l2norm(conv_silu(q_ref, qh_ref, 0)) * (HEAD_DIM ** -0.5)
    qo_ref[0] = q.astype(BF16)
    qeo_ref[0] = (q * egc_x).astype(BF16)
    v = conv_silu(v_ref, vh_ref, 2)
    vbo_ref[0] = (v * beta_x).astype(BF16)


def _gdnprep(p, conv_w, sm, expand, sel):
    bsz, s, _ = p.shape
    tb = min(ROW_BLOCK, s)
    nchunk = tb // CHUNK
    blk = lambda j: pl.BlockSpec((1, tb, WIDTH), lambda b, i: (b, i, j))
    halo = lambda j: pl.BlockSpec((1, SUBLANES, WIDTH),
                                  lambda b, i: (b, jnp.maximum(i * (tb // SUBLANES) - 1, 0), j))
    shape = jax.ShapeDtypeStruct((bsz, s, WIDTH), BF16)
    return pl.pallas_call(
        functools.partial(_gdnprep_kernel, tb=tb),
        grid=(bsz, s // tb),
        in_specs=[blk(4), blk(5), blk(6), halo(4), halo(5), halo(6),
                  pl.BlockSpec((CONV_WIDTH, 3 * WIDTH), lambda b, i: (0, 0)),
                  pl.BlockSpec((1, tb, GATE_COLS), lambda b, i: (b, i, 0)),
                  pl.BlockSpec((4, GATE_COLS, WIDTH), lambda b, i: (0, 0, 0)),
                  pl.BlockSpec((nchunk, tb), lambda b, i: (0, 0))],
        out_specs=[blk(0)] * 7 + [pl.BlockSpec((1, nchunk, WIDTH), lambda b, i: (b, i, 0))],
        out_shape=[shape] * 7 + [jax.ShapeDtypeStruct((bsz, s // CHUNK, WIDTH), F32)],
        compiler_params=_params("arbitrary", "arbitrary"),
        name="gdn_prep",
    )(p, p, p, p, p, p, conv_w, sm, expand, sel)


def _gdn_kernel(q_ref, k_ref, kb_ref, vb_ref, kbe_ref, ga_ref, gb_ref, qe_ref, kd_ref, egl_ref,
                o_ref, s_ref, u_ref, w_ref, attn_ref, *, nchunk, nblk):
    i = pl.program_id(1)

    @pl.when(i == 0)
    def _():
        s_ref[...] = jnp.zeros_like(s_ref)

    r = lax.broadcasted_iota(jnp.int32, (CHUNK, CHUNK), 0)
    c = lax.broadcasted_iota(jnp.int32, (CHUNK, CHUNK), 1)
    lower = r >= c
    strict = r > c
    eye = (r == c).astype(F32)
    diag8 = (r // NEUMANN_BLOCK) == (c // NEUMANN_BLOCK)
    widths = [NEUMANN_BLOCK << lvl for lvl in range((CHUNK // NEUMANN_BLOCK).bit_length() - 1)]
    levels = [((r // (2 * w)) == (c // (2 * w))) & ((r // w) != (c // w)) for w in widths]
    gcol = lax.broadcasted_iota(jnp.int32, (CHUNK, GATE_COLS), 1)
    heads = range(HEADS)
    hsl = [slice(h * HEAD_DIM, (h + 1) * HEAD_DIM) for h in heads]
    gc_groups = (COL_A, COL_A2, COL_A3, COL_A4, COL_A5, COL_A6)
    head_cols = [functools.reduce(lambda a, b: a | b, [gcol == grp + h for grp in gc_groups]) for h in heads]

    unroll = range(SOLVE_UNROLL)

    def chunk_rows(it, t):
        return pl.ds(pl.multiple_of((it * SOLVE_UNROLL + t) * CHUNK, CHUNK), CHUNK)

    def solve_stages(it):
        probs = [(t, h) for t in unroll for h in heads]
        n = range(len(probs))
        rows = [chunk_rows(it, t) for t in unroll]
        ga = [ga_ref[0, rows[t], :] for t in unroll]
        gb = [gb_ref[0, rows[t], :] for t in unroll]
        gbh = [jnp.where(head_cols[h], gb[t], jnp.zeros((), BF16)) for t, h in probs]
        dlog = [_dot_nt(ga[t], gbh[p]) for p, (t, h) in enumerate(probs)]
        kk = [_dot_nt(kb_ref[0, rows[t], hsl[h]], k_ref[0, rows[t], hsl[h]]) for t, h in probs]
        qk = [_dot_nt(q_ref[0, rows[t], hsl[h]], k_ref[0, rows[t], hsl[h]]) for t, h in probs]
        yield
        decay = [jnp.exp(jnp.where(lower, dlog[p], -jnp.inf)) for p in n]
        m = [jnp.where(strict, kk[p] * decay[p], 0.0) for p in n]
        attn = [(qk[p] * decay[p]).astype(BF16) for p in n]
        md = [jnp.where(diag8, m[p], 0.0).astype(BF16) for p in n]
        m2 = [_dot(md[p], md[p]).astype(BF16) for p in n]
        yield
        inv = [eye - md[p].astype(F32) for p in n]
        m4 = [_dot(m2[p], m2[p]).astype(BF16) for p in n]
        inv = [inv[p] + _dot(inv[p].astype(BF16), m2[p]) for p in n]
        yield
        inv = [inv[p] + _dot(inv[p].astype(BF16), m4[p]) for p in n]
        yield
        for lvl in levels:
            off = [jnp.where(lvl, m[p], 0.0).astype(BF16) for p in n]
            invb = [inv[p].astype(BF16) for p in n]
            x = [_dot(invb[p], off[p]).astype(BF16) for p in n]
            yield
            inv = [inv[p] - _dot(x[p], invb[p]) for p in n]
            yield
        uw = [_dot(inv[p].astype(BF16),
                   jnp.concatenate([vb_ref[0, rows[t], hsl[h]], kbe_ref[0, rows[t], hsl[h]]], axis=-1))
              for p, (t, h) in enumerate(probs)]
        yield
        for p, (t, h) in enumerate(probs):
            attn_ref[rows[t], h * HEAD_DIM:h * HEAD_DIM + CHUNK] = attn[p]
            u_ref[rows[t], hsl[h]] = uw[p][:, :HEAD_DIM]
            w_ref[rows[t], hsl[h]] = uw[p][:, HEAD_DIM:].astype(BF16)

    def scan_stages(it):
        state = [s_ref[h] for h in heads]
        for t in unroll:
            rows = chunk_rows(it, t)
            egl = egl_ref[0, pl.ds(it * SOLVE_UNROLL + t, 1), :]
            u = [u_ref[rows, hsl[h]] for h in heads]
            attn = [attn_ref[rows, h * HEAD_DIM:h * HEAD_DIM + CHUNK] for h in heads]
            ws = [_dot(jnp.concatenate([w_ref[rows, hsl[h]], qe_ref[0, rows, hsl[h]]], axis=0),
                       state[h].astype(BF16)) for h in heads]
            yield
            v_new = [(u[h] - ws[h][:CHUNK]).astype(BF16) for h in heads]
            for h in heads:
                o_ref[0, rows, hsl[h]] = (ws[h][CHUNK:] + _dot(attn[h], v_new[h])).astype(BF16)
            state = [state[h] * egl[:, hsl[h]] + _dot_tn(kd_ref[0, rows, hsl[h]], v_new[h]) for h in heads]
            yield
        for h in heads:
            s_ref[h] = state[h]

    def drain(gen):
        for _ in gen:
            pass

    def solve_only(it, carry):
        drain(solve_stages(it))
        return carry

    def scan_only(it, carry):
        drain(scan_stages(it))
        return carry

    def solve_and_scan(it, carry):
        scan = scan_stages(it)
        solve = solve_stages(it)
        solve_yields = 5 + 2 * len(levels)
        scan_yields = 2 * SOLVE_UNROLL
        for stage in range(solve_yields):
            next(solve)
            if (stage * scan_yields) // solve_yields != ((stage + 1) * scan_yields) // solve_yields:
                next(scan, None)
        drain(scan)
        drain(solve)
        return carry

    iters = nchunk // SOLVE_UNROLL

    @pl.when(i == 0)
    def _():
        lax.fori_loop(0, iters, solve_only, 0)

    @pl.when((i > 0) & (i < nblk))
    def _():
        lax.fori_loop(0, iters, solve_and_scan, 0)

    @pl.when(i == nblk)
    def _():
        lax.fori_loop(0, iters, scan_only, 0)


def _gdn(q, k, kb, vb, kbe, qe, kd, ga, gb, egl):
    bsz, s, _ = q.shape
    tb = min(ROW_BLOCK, s)
    nchunk = tb // CHUNK
    nblk = s // tb
    cur = lambda b, i: (b, jnp.minimum(i, nblk - 1), 0)
    prev = lambda b, i: (b, jnp.maximum(i - 1, 0), 0)
    return pl.pallas_call(
        functools.partial(_gdn_kernel, nchunk=nchunk, nblk=nblk),
        grid=(bsz, nblk + 1),
        in_specs=[pl.BlockSpec((1, tb, WIDTH), cur)] * 5 + [pl.BlockSpec((1, tb, GATE_COLS), cur)] * 2
                 + [pl.BlockSpec((1, tb, WIDTH), prev)] * 2 + [pl.BlockSpec((1, nchunk, WIDTH), prev)],
        out_specs=pl.BlockSpec((1, tb, WIDTH), prev),
        out_shape=jax.ShapeDtypeStruct((bsz, s, WIDTH), BF16),
        scratch_shapes=[pltpu.VMEM((HEADS, HEAD_DIM, HEAD_DIM), F32),
                        pltpu.VMEM((tb, WIDTH), F32), pltpu.VMEM((tb, WIDTH), BF16),
                        pltpu.VMEM((tb, WIDTH), BF16)],
        compiler_params=_params("arbitrary", "arbitrary"),
        name="gdn_delta_rule",
    )(q, k, kb, vb, kbe, ga, gb, qe, kd, egl)


def _out_kernel(x_ref, fo_ref, fz_ref, go_ref, gz_ref, gate_ref, gng_ref, w_ref, fg_ref, o_ref, *, final_norm):
    a = fo_ref[0].astype(F32) * _silu(fz_ref[0].astype(F32))
    parts = []
    for h in range(HEADS):
        sl = slice(h * HEAD_DIM, (h + 1) * HEAD_DIM)
        t = go_ref[0, :, sl].astype(F32)
        ms = jnp.mean(t * t, axis=-1, keepdims=True)
        parts.append(t * lax.rsqrt(ms + EPS) * gng_ref[...])
    g = jnp.concatenate(parts, axis=-1) * _silu(gz_ref[0].astype(F32))
    y = _dot(a.astype(BF16), w_ref[0:WIDTH, :]) + _dot(g.astype(BF16), w_ref[WIDTH:2 * WIDTH, :])
    xn = x_ref[0] + gate_ref[0] * y
    if final_norm:
        ms = jnp.mean(xn * xn, axis=-1, keepdims=True)
        xn = xn * lax.rsqrt(ms + EPS) * fg_ref[...]
    o_ref[0] = xn


def _out(x, fox_o, p, gdn_o, gate, gng, w_out, final_g, final_norm):
    bsz, s, d = x.shape
    tm = min(ROW_BLOCK, s)
    blk = lambda j: pl.BlockSpec((1, tm, WIDTH), lambda b, i: (b, i, j))
    return pl.pallas_call(
        functools.partial(_out_kernel, final_norm=final_norm),
        grid=(bsz, s // tm),
        in_specs=[pl.BlockSpec((1, tm, d), lambda b, i: (b, i, 0)),
                  blk(0), blk(3), blk(0), blk(7),
                  pl.BlockSpec((1, 1, d), lambda b, i: (b, 0, 0)),
                  pl.BlockSpec((1, HEAD_DIM), lambda b, i: (0, 0)),
                  pl.BlockSpec((2 * WIDTH, d), lambda b, i: (0, 0)),
                  pl.BlockSpec((1, d), lambda b, i: (0, 0))],
        out_specs=pl.BlockSpec((1, tm, d), lambda b, i: (b, i, 0)),
        out_shape=jax.ShapeDtypeStruct((bsz, s, d), F32),
        compiler_params=_params("arbitrary", "arbitrary"),
        name="gate_out_proj",
    )(x, fox_o, p, gdn_o, p, gate, gng, w_out, final_g)


def _expand_matrices():
    lane_head = jnp.arange(WIDTH) // HEAD_DIM
    rows = jnp.arange(GATE_COLS)[:, None]
    mats = [(rows == grp + lane_head[None, :]).astype(F32) for grp in (COL_B, COL_A, COL_A2, COL_A3)]
    return jnp.stack(mats)


def _chunk_last_selector(tb):
    nchunk = tb // CHUNK
    return (jnp.arange(tb)[None, :] == (jnp.arange(nchunk)[:, None] * CHUNK + CHUNK - 1)).astype(F32)


def _split_w_in(w):
    fw = WIDTH
    o_ff = 4 * fw
    o_g = o_ff + HEADS
    o_ga = o_g + 4 * fw
    o_gb = o_ga + HEADS
    w_main = jnp.concatenate([w[:, :o_ff], w[:, o_g:o_ga]], axis=1).astype(BF16)
    cols = {"f": w[:, o_ff:o_g], "a": w[:, o_ga:o_gb], "b": w[:, o_gb:o_gb + HEADS]}
    pad = jnp.zeros((w.shape[0], GATE_COLS - len(GATE_GROUPS) * HEADS), w.dtype)
    w_small = jnp.concatenate([cols[t] for t in GATE_GROUPS] + [pad], axis=1).astype(BF16)
    return w_main, w_small


def _gate_rows(b_f, dt_bias, a_log):
    z = jnp.zeros((HEADS,), F32)
    pad = jnp.zeros((GATE_COLS - len(GATE_GROUPS) * HEADS,), F32)
    add = {"f": b_f, "a": dt_bias, "b": z}
    alog = {"f": z, "a": a_log, "b": z}
    add_row = jnp.concatenate([add[t] for t in GATE_GROUPS] + [pad]).reshape(1, GATE_COLS)
    alog_row = jnp.concatenate([alog[t] for t in GATE_GROUPS] + [pad]).reshape(1, GATE_COLS)
    return add_row.astype(F32), alog_row.astype(F32)


def kernel(x, c, norm_g, w_ada, b_ada, w_in, b_fgate, fox_qn_g, fox_kn_g, gdn_conv_w, gdn_A_log,
           gdn_dt_bias, gdn_norm_g, w_out, final_g):
    bsz, s, d = x.shape
    depth = w_in.shape[0]
    expand = _expand_matrices()
    sel = _chunk_last_selector(min(ROW_BLOCK, s))
    tk = min(FOX_KEYS, s // 2)
    for l in range(depth):
        mod = _ada(c, w_ada[l], b_ada[l])
        shift, scale, gate = (mod[:, k * d:(k + 1) * d].reshape(bsz, 1, d) for k in range(3))
        w_main, w_small = _split_w_in(w_in[l])
        p, ps = _proj(x, shift, scale, norm_g[l].reshape(1, d), w_main, w_small)

        add_row, alog_row = _gate_rows(b_fgate[l], gdn_dt_bias[l], gdn_A_log[l])
        ft, fx, ga, gb, sm = _gates(ps, add_row, alog_row)

        qg = fox_qn_g[l].reshape(1, HEAD_DIM)
        kg = fox_kn_g[l].reshape(1, HEAD_DIM)
        qn, kn = _foxprep(p, qg, kg)
        lo = _fox_block_start(ft, qg, kg, tk)
        fox_o = _fox(lo, qn, kn, p, fx, tk)

        gq, gk, gkb, gvb, gkbe, gqe, gkd, egl = _gdnprep(p, gdn_conv_w[l], sm, expand, sel)
        gdn_o = _gdn(gq, gk, gkb, gvb, gkbe, gqe, gkd, ga, gb, egl)

        x = _out(x, fox_o, p, gdn_o, gate, gdn_norm_g[l].reshape(1, HEAD_DIM), w_out[l].astype(BF16),
                 final_g.reshape(1, d), final_norm=(l == depth - 1))
    return x
```

```python
import functools

import jax
import jax.numpy as jnp
from jax import lax
from jax.experimental import pallas as pl
from jax.experimental.pallas import tpu as pltpu

F32 = jnp.float32
BF16 = jnp.bfloat16
HIGHEST = lax.Precision.HIGHEST

HEADS = 8
HEAD_DIM = 128
WIDTH = HEADS * HEAD_DIM
CHUNK = 64
NEUMANN_BLOCK = 8
CONV_WIDTH = 4
EPS = 1e-6
GATE_COLS = 128
COL_F, COL_A, COL_B, COL_A2, COL_A3, COL_F2, COL_F3, COL_A4, COL_A5, COL_A6 = 0, 8, 16, 24, 32, 40, 48, 56, 64, 72
GATE_GROUPS = "fabaaffaaa"
LOG2E = 1.4426950408889634
FOX_UNROLL = 4
SOLVE_UNROLL = 4
EXP_UNDERFLOW = 104.0

SUBLANES = 8
PROJ_ROWS = 1024
PROJ_COLS = 2 * WIDTH
NORM_ROWS = 128
ADA_COLS = 512
GATE_ROWS = 256
ROW_BLOCK = 512
FOX_KEYS = 512
VMEM_LIMIT = 52 * 1024 * 1024

NT_DIMS = (((1,), (1,)), ((), ()))
TN_DIMS = (((0,), (0,)), ((), ()))


def _dot(a, b, precision=None):
    return jnp.dot(a, b, preferred_element_type=F32, precision=precision)


def _dot_nt(a, b, precision=None):
    return lax.dot_general(a, b, NT_DIMS, preferred_element_type=F32, precision=precision)


def _dot_tn(a, b):
    return lax.dot_general(a, b, TN_DIMS, preferred_element_type=F32)


def _silu(x):
    return x * jax.nn.sigmoid(x)


def _params(*sem):
    return pltpu.CompilerParams(dimension_semantics=sem, vmem_limit_bytes=VMEM_LIMIT)


def _ada_kernel(c_ref, w_ref, b_ref, o_ref):
    o_ref[...] = _dot(_silu(c_ref[...]), w_ref[...], HIGHEST) + b_ref[...]


def _ada(c, w, b):
    bsz, d = c.shape
    n = w.shape[1]
    rows = SUBLANES
    tn = ADA_COLS if n % ADA_COLS == 0 else HEAD_DIM
    cp = jnp.pad(c, ((0, rows - bsz), (0, 0)))
    out = pl.pallas_call(
        _ada_kernel,
        grid=(n // tn,),
        in_specs=[pl.BlockSpec((rows, d), lambda j: (0, 0)),
                  pl.BlockSpec((d, tn), lambda j: (0, j)),
                  pl.BlockSpec((1, tn), lambda j: (0, j))],
        out_specs=pl.BlockSpec((rows, tn), lambda j: (0, j)),
        out_shape=jax.ShapeDtypeStruct((rows, n), F32),
        compiler_params=_params("arbitrary"),
        name="ada_mod",
    )(cp, w, b.reshape(1, n))
    return out[:bsz]


def _proj_kernel(x_ref, sh_ref, sc_ref, g_ref, w_ref, ws_ref, p_ref, ps_ref, h_ref, *, tm, rows):
    @pl.when(pl.program_id(2) == 0)
    def _():
        gmul = g_ref[...] * (1.0 + sc_ref[0])
        shift = sh_ref[0]

        def body(r, carry):
            sl = pl.ds(pl.multiple_of(r * rows, rows), rows)
            xs = x_ref[0, sl, :]
            ms = jnp.mean(xs * xs, axis=-1, keepdims=True)
            h_ref[sl, :] = (xs * lax.rsqrt(ms + EPS) * gmul + shift).astype(BF16)
            return carry

        lax.fori_loop(0, tm // rows, body, 0)
        ps_ref[0] = _dot(h_ref[...], ws_ref[...])

    p_ref[0] = _dot(h_ref[...], w_ref[...]).astype(BF16)


def _proj(x, shift, scale, g, w_main, w_small):
    bsz, s, d = x.shape
    n = w_main.shape[1]
    tm = min(PROJ_ROWS, s)
    tn = PROJ_COLS
    kern = functools.partial(_proj_kernel, tm=tm, rows=min(NORM_ROWS, tm))
    return pl.pallas_call(
        kern,
        grid=(bsz, s // tm, n // tn),
        in_specs=[pl.BlockSpec((1, tm, d), lambda b, i, j: (b, i, 0)),
                  pl.BlockSpec((1, 1, d), lambda b, i, j: (b, 0, 0)),
                  pl.BlockSpec((1, 1, d), lambda b, i, j: (b, 0, 0)),
                  pl.BlockSpec((1, d), lambda b, i, j: (0, 0)),
                  pl.BlockSpec((d, tn), lambda b, i, j: (0, j)),
                  pl.BlockSpec((d, GATE_COLS), lambda b, i, j: (0, 0))],
        out_specs=[pl.BlockSpec((1, tm, tn), lambda b, i, j: (b, i, j)),
                   pl.BlockSpec((1, tm, GATE_COLS), lambda b, i, j: (b, i, 0))],
        out_shape=[jax.ShapeDtypeStruct((bsz, s, n), BF16),
                   jax.ShapeDtypeStruct((bsz, s, GATE_COLS), F32)],
        scratch_shapes=[pltpu.VMEM((tm, d), BF16)],
        compiler_params=_params("arbitrary", "arbitrary", "arbitrary"),
        name="norm_in_proj",
    )(x, shift, scale, g, w_main, w_small)


def _gates_kernel(ps_ref, add_ref, alog_ref, ft_ref, fx_ref, ga_ref, gb_ref, sm_ref, carry_ref, *, tb):
    @pl.when(pl.program_id(1) == 0)
    def _():
        carry_ref[...] = jnp.zeros_like(carry_ref)

    x = ps_ref[0] + add_ref[...]
    col = lax.broadcasted_iota(jnp.int32, (tb, GATE_COLS), 1)
    grp = lambda start: (col >= start) & (col < start + HEADS)
    is_f = grp(COL_F) | grp(COL_F2) | grp(COL_F3)
    is_a = grp(COL_A) | grp(COL_A2) | grp(COL_A3) | grp(COL_A4) | grp(COL_A5) | grp(COL_A6)
    log_f = jax.nn.log_sigmoid(x)
    g = -jnp.exp(alog_ref[...]) * jax.nn.softplus(x)
    beta = jax.nn.sigmoid(x)
    vals = jnp.where(is_f, log_f, jnp.where(is_a, g, 0.0))

    r = lax.broadcasted_iota(jnp.int32, (tb, tb), 0)
    c = lax.broadcasted_iota(jnp.int32, (tb, tb), 1)
    same_chunk = (r // CHUNK) == (c // CHUNK)
    sums = jnp.concatenate([(r >= c).astype(F32),
                            ((r >= c) & same_chunk).astype(F32),
                            same_chunk.astype(F32)], axis=0).astype(BF16)

    def split3(t):
        hi = t.astype(BF16).astype(F32)
        mid = (t - hi).astype(BF16).astype(F32)
        return hi, mid, t - hi - mid

    pieces = _dot(sums, jnp.concatenate(split3(vals), axis=1).astype(BF16))
    summed = pieces[:, :GATE_COLS] + pieces[:, GATE_COLS:2 * GATE_COLS] + pieces[:, 2 * GATE_COLS:]
    carry = carry_ref[0:1, :]
    cs_all = summed[0:tb] + carry
    cs = summed[tb:2 * tb]
    tot = summed[2 * tb:]
    carry_ref[...] = jnp.broadcast_to(carry + jnp.sum(vals, axis=0, keepdims=True), carry_ref.shape)

    ft_ref[0] = jnp.transpose(cs_all)[0:HEADS, :]
    f_hi, f_mid, f_lo = split3(cs_all * (-LOG2E))
    fx_ref[0] = jnp.where(grp(COL_F), f_hi, jnp.where(grp(COL_F2), f_mid,
                                                      jnp.where(grp(COL_F3), f_lo, 0.0))).astype(BF16)
    c_hi, c_mid, c_lo = split3(cs)
    gc_cols = grp(COL_A) | grp(COL_A2) | grp(COL_A5)
    ones = grp(COL_A3) | grp(COL_A4) | grp(COL_A6)
    ga_ref[0] = jnp.where(grp(COL_A), c_hi, jnp.where(grp(COL_A2), c_mid, jnp.where(grp(COL_A5), c_lo,
                                                                                      jnp.where(ones, 1.0, 0.0)))).astype(BF16)
    gb_ref[0] = jnp.where(gc_cols, 1.0,
                          jnp.where(grp(COL_A3), -c_hi, jnp.where(grp(COL_A4), -c_mid,
                                                                  jnp.where(grp(COL_A6), -c_lo, 0.0)))).astype(BF16)
    sm_ref[0] = jnp.where(grp(COL_A), jnp.exp(cs),
                          jnp.where(grp(COL_B), beta,
                                    jnp.where(grp(COL_A2), jnp.exp(tot - cs),
                                              jnp.where(grp(COL_A3), jnp.exp(tot), 0.0))))


def _gates(ps, add_row, alog_row):
    bsz, s, _ = ps.shape
    tb = min(GATE_ROWS, s)
    small = pl.BlockSpec((1, tb, GATE_COLS), lambda b, i: (b, i, 0))
    row = pl.BlockSpec((1, GATE_COLS), lambda b, i: (0, 0))
    small_shape = jax.ShapeDtypeStruct((bsz, s, GATE_COLS), F32)
    bf16_shape = jax.ShapeDtypeStruct((bsz, s, GATE_COLS), BF16)
    return pl.pallas_call(
        functools.partial(_gates_kernel, tb=tb),
        grid=(bsz, s // tb),
        in_specs=[small, row, row],
        out_specs=[pl.BlockSpec((1, HEADS, tb), lambda b, i: (b, 0, i)), small, small, small, small],
        out_shape=[jax.ShapeDtypeStruct((bsz, HEADS, s), F32), bf16_shape, bf16_shape, bf16_shape, small_shape],
        scratch_shapes=[pltpu.VMEM((SUBLANES, GATE_COLS), F32)],
        compiler_params=_params("arbitrary", "arbitrary"),
        name="gates",
    )(ps, add_row, alog_row)


def _foxprep_kernel(q_ref, k_ref, qg_ref, kg_ref, qo_ref, ko_ref):
    def norm(src, gain, dst):
        for h in range(HEADS):
            sl = slice(h * HEAD_DIM, (h + 1) * HEAD_DIM)
            t = src[0, :, sl].astype(F32)
            ms = jnp.mean(t * t, axis=-1, keepdims=True)
            dst[0, :, sl] = (t * lax.rsqrt(ms + EPS) * gain).astype(BF16)

    norm(q_ref, qg_ref[...] * (HEAD_DIM ** -0.5 * LOG2E), qo_ref)
    norm(k_ref, kg_ref[...], ko_ref)


def _foxprep(p, qg, kg):
    bsz, s, _ = p.shape
    tb = min(2 * ROW_BLOCK, s)
    blk = lambda j: pl.BlockSpec((1, tb, WIDTH), lambda b, i: (b, i, j))
    row = pl.BlockSpec((1, HEAD_DIM), lambda b, i: (0, 0))
    shape = jax.ShapeDtypeStruct((bsz, s, WIDTH), BF16)
    return pl.pallas_call(
        _foxprep_kernel,
        grid=(bsz, s // tb),
        in_specs=[blk(0), blk(1), row, row],
        out_specs=[blk(0), blk(0)],
        out_shape=[shape, shape],
        compiler_params=_params("arbitrary", "arbitrary"),
        name="fox_qk_norm",
    )(p, p, qg, kg)


def _fox_kernel(lo_ref, q_ref, k_ref, v_ref, fx_ref, o_ref, acc_ref, st_ref, *, tk, npairs):
    b = pl.program_id(0)
    h = pl.program_id(1)
    tq = 2 * tk
    lane = lax.broadcasted_iota(jnp.int32, (tq, GATE_COLS), 1)
    pick = (lane == COL_F + h) | (lane == COL_F2 + h) | (lane == COL_F3 + h)
    picks = jnp.where(pick, 1.0, 0.0).astype(BF16)
    late = slice(tk, tq)

    def key_rows(j):
        return pl.ds(pl.multiple_of(j * tk, tk), tk)

    def causal(st):
        r = lax.broadcasted_iota(jnp.int32, st.shape, 0)
        c = lax.broadcasted_iota(jnp.int32, st.shape, 1)
        return jnp.where(r <= c, st, -jnp.inf)

    def query_pair(g, unused):
        lo = lo_ref[b * HEADS + h, 2 * g]
        q_rows = pl.ds(pl.multiple_of(g * tq, tq), tq)
        qa = jnp.concatenate([q_ref[0, q_rows, :], picks], axis=1)
        acc_ref[...] = jnp.zeros_like(acc_ref)

        def scores(j, queries=slice(None)):
            ka = jnp.concatenate([k_ref[0, key_rows(j), :], fx_ref[0, key_rows(j), :]], axis=1)
            return _dot_nt(ka, qa[queries])

        def issue(j, kind, slot):
            if kind == "full":
                st_ref[slot] = scores(j)
            elif kind == "diag":
                st_ref[slot] = causal(scores(j))
            else:
                st_ref[slot, :, late] = causal(scores(j, late))

        def step(cur_j, cur_slot, carry, nxt=None, only_late=False):
            if nxt is not None:
                issue(nxt[0], nxt[1], 1 - cur_slot)
            cols = late if only_late else slice(None)
            m_all, l_all = carry
            m, l = m_all[:, cols], l_all[:, cols]
            st = st_ref[cur_slot, :, cols]
            m_new = jnp.maximum(m, jnp.max(st, axis=0, keepdims=True))
            alpha = jnp.exp2(m - m_new)
            p = jnp.exp2(st - m_new)
            l_new = alpha * l + jnp.sum(p, axis=0, keepdims=True)
            pv = _dot_tn(v_ref[0, key_rows(cur_j), :], p.astype(BF16))
            acc_ref[:, cols] = alpha * acc_ref[:, cols] + pv
            if only_late:
                m_new = jnp.concatenate([m_all[:, :tk], m_new], axis=1)
                l_new = jnp.concatenate([l_all[:, :tk], l_new], axis=1)
            return m_new, l_new

        d = 2 * g
        n = d - lo

        @pl.when(n == 0)
        def _():
            issue(d, "diag", 0)

        @pl.when(n > 0)
        def _():
            issue(lo, "full", 0)

        def interior_steps(t, carry):
            j = lo + FOX_UNROLL * t
            for u in range(FOX_UNROLL):
                carry = step(j + u, u % 2, carry, nxt=(j + u + 1, "full"))
            return carry

        init = (jnp.full((1, tq), -jnp.inf, F32), jnp.zeros((1, tq), F32))
        loops = jnp.maximum(n - 1, 0) // FOX_UNROLL
        carry = lax.fori_loop(0, loops, interior_steps, init)

        def finish(interior_left):
            def run(carry):
                slot = 0
                for back in range(interior_left, 0, -1):
                    carry = step(d - back, slot, carry, nxt=(d - back + 1, "full" if back > 1 else "diag"))
                    slot = 1 - slot
                carry = step(d, slot, carry, nxt=(d + 1, "late"))
                return step(d + 1, 1 - slot, carry, only_late=True)
            return run

        left = n - FOX_UNROLL * loops
        m, l = lax.switch(left, [finish(k) for k in range(FOX_UNROLL + 1)], carry)
        o_ref[0, q_rows, :] = jnp.transpose(acc_ref[...] / l).astype(BF16)
        return unused

    lax.fori_loop(0, npairs, query_pair, 0)


def _fox(lo, qn, kn, p, fx, tk):
    bsz, s, _ = qn.shape
    tq = 2 * tk
    head = lambda offset: pl.BlockSpec((1, s, HEAD_DIM), lambda b, h, lo_r: (b, 0, offset + h))
    grid_spec = pltpu.PrefetchScalarGridSpec(
        num_scalar_prefetch=1,
        grid=(bsz, HEADS),
        in_specs=[head(0), head(0), head(2 * HEADS),
                  pl.BlockSpec((1, s, GATE_COLS), lambda b, h, lo_r: (b, 0, 0))],
        out_specs=head(0),
        scratch_shapes=[pltpu.VMEM((HEAD_DIM, tq), F32), pltpu.VMEM((2, tk, tq), F32)],
    )
    return pl.pallas_call(
        functools.partial(_fox_kernel, tk=tk, npairs=s // tq),
        grid_spec=grid_spec,
        out_shape=jax.ShapeDtypeStruct((bsz, s, WIDTH), BF16),
        compiler_params=_params("arbitrary", "arbitrary"),
        name="fox_attention",
    )(lo, qn, kn, p, fx)


def _fox_block_start(ft, qg, kg, tq):
    bsz, _, s = ft.shape
    f_first = ft[:, :, 0::tq]
    f_last = ft[:, :, tq - 1::tq]
    qk_bound = 1.02 * (HEAD_DIM ** 0.5) * jnp.max(jnp.abs(qg)) * jnp.max(jnp.abs(kg))
    thresh = EXP_UNDERFLOW + 2.0 * qk_bound
    skip = f_last[:, :, None, :] > f_first[:, :, :, None] + thresh
    return jnp.sum(skip, axis=-1).astype(jnp.int32).reshape(bsz * HEADS, s // tq)


def _gdnprep_kernel(q_ref, k_ref, v_ref, qh_ref, kh_ref, vh_ref, cw_ref, sm_ref, ex_ref, sel_ref,
                    qo_ref, ko_ref, kbo_ref, vbo_ref, kbeo_ref, qeo_ref, kdo_ref, eglo_ref, *, tb):
    first = pl.program_id(1) == 0

    def conv_silu(src, halo, which):
        ext = jnp.concatenate([jnp.where(first, 0.0, halo[0].astype(F32)), src[0].astype(F32)], axis=0)
        acc = jnp.zeros((tb, WIDTH), F32)
        for t in range(CONV_WIDTH):
            w_row = cw_ref[t:t + 1, which * WIDTH:(which + 1) * WIDTH]
            back = CONV_WIDTH - 1 - t
            shifted = pltpu.roll(ext, back, axis=0) if back else ext
            acc = acc + shifted[SUBLANES:, :] * w_row
        return _silu(acc)

    def spread(group):
        e = ex_ref[group].astype(BF16)
        return _dot(sm2, jnp.concatenate([e, e], axis=0))

    def l2norm(t):
        parts = []
        for h in range(HEADS):
            th = t[:, h * HEAD_DIM:(h + 1) * HEAD_DIM]
            parts.append(th * lax.rsqrt(jnp.sum(th * th, axis=-1, keepdims=True) + EPS))
        return jnp.concatenate(parts, axis=-1)

    sm = sm_ref[0]
    sm_hi = sm.astype(BF16)
    sm2 = jnp.concatenate([sm_hi, (sm - sm_hi.astype(F32)).astype(BF16)], axis=1)
    beta_x = spread(0)
    egc_x = spread(1)
    edec_x = spread(2)
    eglo_ref[0] = _dot(_dot(sel_ref[...], sm, HIGHEST), ex_ref[3], HIGHEST)

    k = l2norm(conv_silu(k_ref, kh_ref, 1))
    kb = k * beta_x
    ko_ref[0] = k.astype(BF16)
    kbo_ref[0] = kb.astype(BF16)
    kbeo_ref[0] = (kb * egc_x).astype(BF16)
    kdo_ref[0] = (k * edec_x).astype(BF16)
    q = l2norm(conv_silu(q_ref, qh_ref, 0)) * (HEAD_DIM ** -0.5)
    qo_ref[0] = q.astype(BF16)
    qeo_ref[0] = (q * egc_x).astype(BF16)
    v = conv_silu(v_ref, vh_ref, 2)
    vbo_ref[0] = (v * beta_x).astype(BF16)


def _gdnprep(p, conv_w, sm, expand, sel):
    bsz, s, _ = p.shape
    tb = min(ROW_BLOCK, s)
    nchunk = tb // CHUNK
    blk = lambda j: pl.BlockSpec((1, tb, WIDTH), lambda b, i: (b, i, j))
    halo = lambda j: pl.BlockSpec((1, SUBLANES, WIDTH),
                                  lambda b, i: (b, jnp.maximum(i * (tb // SUBLANES) - 1, 0), j))
    shape = jax.ShapeDtypeStruct((bsz, s, WIDTH), BF16)
    return pl.pallas_call(
        functools.partial(_gdnprep_kernel, tb=tb),
        grid=(bsz, s // tb),
        in_specs=[blk(4), blk(5), blk(6), halo(4), halo(5), halo(6),
                  pl.BlockSpec((CONV_WIDTH, 3 * WIDTH), lambda b, i: (0, 0)),
                  pl.BlockSpec((1, tb, GATE_COLS), lambda b, i: (b, i, 0)),
                  pl.BlockSpec((4, GATE_COLS, WIDTH), lambda b, i: (0, 0, 0)),
                  pl.BlockSpec((nchunk, tb), lambda b, i: (0, 0))],
        out_specs=[blk(0)] * 7 + [pl.BlockSpec((1, nchunk, WIDTH), lambda b, i: (b, i, 0))],
        out_shape=[shape] * 7 + [jax.ShapeDtypeStruct((bsz, s // CHUNK, WIDTH), F32)],
        compiler_params=_params("arbitrary", "arbitrary"),
        name="gdn_prep",
    )(p, p, p, p, p, p, conv_w, sm, expand, sel)


def _gdn_kernel(q_ref, k_ref, kb_ref, vb_ref, kbe_ref, ga_ref, gb_ref, qe_ref, kd_ref, egl_ref,
                o_ref, s_ref, u_ref, w_ref, attn_ref, *, nchunk, nblk):
    i = pl.program_id(1)

    @pl.when(i == 0)
    def _():
        s_ref[...] = jnp.zeros_like(s_ref)

    r = lax.broadcasted_iota(jnp.int32, (CHUNK, CHUNK), 0)
    c = lax.broadcasted_iota(jnp.int32, (CHUNK, CHUNK), 1)
    lower = r >= c
    strict = r > c
    eye = (r == c).astype(F32)
    diag8 = (r // NEUMANN_BLOCK) == (c // NEUMANN_BLOCK)
    widths = [NEUMANN_BLOCK << lvl for lvl in range((CHUNK // NEUMANN_BLOCK).bit_length() - 1)]
    levels = [((r // (2 * w)) == (c // (2 * w))) & ((r // w) != (c // w)) for w in widths]
    gcol = lax.broadcasted_iota(jnp.int32, (CHUNK, GATE_COLS), 1)
    heads = range(HEADS)
    hsl = [slice(h * HEAD_DIM, (h + 1) * HEAD_DIM) for h in heads]
    gc_groups = (COL_A, COL_A2, COL_A3, COL_A4, COL_A5, COL_A6)
    head_cols = [functools.reduce(lambda a, b: a | b, [gcol == grp + h for grp in gc_groups]) for h in heads]

    unroll = range(SOLVE_UNROLL)

    def chunk_rows(it, t):
        return pl.ds(pl.multiple_of((it * SOLVE_UNROLL + t) * CHUNK, CHUNK), CHUNK)

    def solve_stages(it):
        probs = [(t, h) for t in unroll for h in heads]
        n = range(len(probs))
        rows = [chunk_rows(it, t) for t in unroll]
        ga = [ga_ref[0, rows[t], :] for t in unroll]
        gb = [gb_ref[0, rows[t], :] for t in unroll]
        gbh = [jnp.where(head_cols[h], gb[t], jnp.zeros((), BF16)) for t, h in probs]
        dlog = [_dot_nt(ga[t], gbh[p]) for p, (t, h) in enumerate(probs)]
        kk = [_dot_nt(kb_ref[0, rows[t], hsl[h]], k_ref[0, rows[t], hsl[h]]) for t, h in probs]
        qk = [_dot_nt(q_ref[0, rows[t], hsl[h]], k_ref[0, rows[t], hsl[h]]) for t, h in probs]
        yield
        decay = [jnp.exp(jnp.where(lower, dlog[p], -jnp.inf)) for p in n]
        m = [jnp.where(strict, kk[p] * decay[p], 0.0) for p in n]
        attn = [(qk[p] * decay[p]).astype(BF16) for p in n]
        md = [jnp.where(diag8, m[p], 0.0).astype(BF16) for p in n]
        m2 = [_dot(md[p], md[p]).astype(BF16) for p in n]
        yield
        inv = [eye - md[p].astype(F32) for p in n]
        m4 = [_dot(m2[p], m2[p]).astype(BF16) for p in n]
        inv = [inv[p] + _dot(inv[p].astype(BF16), m2[p]) for p in n]
        yield
        inv = [inv[p] + _dot(inv[p].astype(BF16), m4[p]) for p in n]
        yield
        for lvl in levels:
            off = [jnp.where(lvl, m[p], 0.0).astype(BF16) for p in n]
            invb = [inv[p].astype(BF16) for p in n]
            x = [_dot(invb[p], off[p]).astype(BF16) for p in n]
            yield
            inv = [inv[p] - _dot(x[p], invb[p]) for p in n]
            yield
        uw = [_dot(inv[p].astype(BF16),
                   jnp.concatenate([vb_ref[0, rows[t], hsl[h]], kbe_ref[0, rows[t], hsl[h]]], axis=-1))
              for p, (t, h) in enumerate(probs)]
        yield
        for p, (t, h) in enumerate(probs):
            attn_ref[rows[t], h * HEAD_DIM:h * HEAD_DIM + CHUNK] = attn[p]
            u_ref[rows[t], hsl[h]] = uw[p][:, :HEAD_DIM]
            w_ref[rows[t], hsl[h]] = uw[p][:, HEAD_DIM:].astype(BF16)

    def scan_stages(it):
        state = [s_ref[h] for h in heads]
        for t in unroll:
            rows = chunk_rows(it, t)
            egl = egl_ref[0, pl.ds(it * SOLVE_UNROLL + t, 1), :]
            u = [u_ref[rows, hsl[h]] for h in heads]
            attn = [attn_ref[rows, h * HEAD_DIM:h * HEAD_DIM + CHUNK] for h in heads]
            ws = [_dot(jnp.concatenate([w_ref[rows, hsl[h]], qe_ref[0, rows, hsl[h]]], axis=0),
                       state[h].astype(BF16)) for h in heads]
            yield
            v_new = [(u[h] - ws[h][:CHUNK]).astype(BF16) for h in heads]
            for h in heads:
                o_ref[0, rows, hsl[h]] = (ws[h][CHUNK:] + _dot(attn[h], v_new[h])).astype(BF16)
            state = [state[h] * egl[:, hsl[h]] + _dot_tn(kd_ref[0, rows, hsl[h]], v_new[h]) for h in heads]
            yield
        for h in heads:
            s_ref[h] = state[h]

    def drain(gen):
        for _ in gen:
            pass

    def solve_only(it, carry):
        drain(solve_stages(it))
        return carry

    def scan_only(it, carry):
        drain(scan_stages(it))
        return carry

    def solve_and_scan(it, carry):
        scan = scan_stages(it)
        solve = solve_stages(it)
        solve_yields = 5 + 2 * len(levels)
        scan_yields = 2 * SOLVE_UNROLL
        for stage in range(solve_yields):
            next(solve)
            if (stage * scan_yields) // solve_yields != ((stage + 1) * scan_yields) // solve_yields:
                next(scan, None)
        drain(scan)
        drain(solve)
        return carry

    iters = nchunk // SOLVE_UNROLL

    @pl.when(i == 0)
    def _():
        lax.fori_loop(0, iters, solve_only, 0)

    @pl.when((i > 0) & (i < nblk))
    def _():
        lax.fori_loop(0, iters, solve_and_scan, 0)

    @pl.when(i == nblk)
    def _():
        lax.fori_loop(0, iters, scan_only, 0)


def _gdn(q, k, kb, vb, kbe, qe, kd, ga, gb, egl):
    bsz, s, _ = q.shape
    tb = min(ROW_BLOCK, s)
    nchunk = tb // CHUNK
    nblk = s // tb
    cur = lambda b, i: (b, jnp.minimum(i, nblk - 1), 0)
    prev = lambda b, i: (b, jnp.maximum(i - 1, 0), 0)
    return pl.pallas_call(
        functools.partial(_gdn_kernel, nchunk=nchunk, nblk=nblk),
        grid=(bsz, nblk + 1),
        in_specs=[pl.BlockSpec((1, tb, WIDTH), cur)] * 5 + [pl.BlockSpec((1, tb, GATE_COLS), cur)] * 2
                 + [pl.BlockSpec((1, tb, WIDTH), prev)] * 2 + [pl.BlockSpec((1, nchunk, WIDTH), prev)],
        out_specs=pl.BlockSpec((1, tb, WIDTH), prev),
        out_shape=jax.ShapeDtypeStruct((bsz, s, WIDTH), BF16),
        scratch_shapes=[pltpu.VMEM((HEADS, HEAD_DIM, HEAD_DIM), F32),
                        pltpu.VMEM((tb, WIDTH), F32), pltpu.VMEM((tb, WIDTH), BF16),
                        pltpu.VMEM((tb, WIDTH), BF16)],
        compiler_params=_params("arbitrary", "arbitrary"),
        name="gdn_delta_rule",
    )(q, k, kb, vb, kbe, ga, gb, qe, kd, egl)


def _out_kernel(x_ref, fo_ref, fz_ref, go_ref, gz_ref, gate_ref, gng_ref, w_ref, fg_ref, o_ref, *, final_norm):
    a = fo_ref[0].astype(F32) * _silu(fz_ref[0].astype(F32))
    parts = []
    for h in range(HEADS):
        sl = slice(h * HEAD_DIM, (h + 1) * HEAD_DIM)
        t = go_ref[0, :, sl].astype(F32)
        ms = jnp.mean(t * t, axis=-1, keepdims=True)
        parts.append(t * lax.rsqrt(ms + EPS) * gng_ref[...])
    g = jnp.concatenate(parts, axis=-1) * _silu(gz_ref[0].astype(F32))
    y = _dot(a.astype(BF16), w_ref[0:WIDTH, :]) + _dot(g.astype(BF16), w_ref[WIDTH:2 * WIDTH, :])
    xn = x_ref[0] + gate_ref[0] * y
    if final_norm:
        ms = jnp.mean(xn * xn, axis=-1, keepdims=True)
        xn = xn * lax.rsqrt(ms + EPS) * fg_ref[...]
    o_ref[0] = xn


def _out(x, fox_o, p, gdn_o, gate, gng, w_out, final_g, final_norm):
    bsz, s, d = x.shape
    tm = min(ROW_BLOCK, s)
    blk = lambda j: pl.BlockSpec((1, tm, WIDTH), lambda b, i: (b, i, j))
    return pl.pallas_call(
        functools.partial(_out_kernel, final_norm=final_norm),
        grid=(bsz, s // tm),
        in_specs=[pl.BlockSpec((1, tm, d), lambda b, i: (b, i, 0)),
                  blk(0), blk(3), blk(0), blk(7),
                  pl.BlockSpec((1, 1, d), lambda b, i: (b, 0, 0)),
                  pl.BlockSpec((1, HEAD_DIM), lambda b, i: (0, 0)),
                  pl.BlockSpec((2 * WIDTH, d), lambda b, i: (0, 0)),
                  pl.BlockSpec((1, d), lambda b, i: (0, 0))],
        out_specs=pl.BlockSpec((1, tm, d), lambda b, i: (b, i, 0)),
        out_shape=jax.ShapeDtypeStruct((bsz, s, d), F32),
        compiler_params=_params("arbitrary", "arbitrary"),
        name="gate_out_proj",
    )(x, fox_o, p, gdn_o, p, gate, gng, w_out, final_g)


def _expand_matrices():
    lane_head = jnp.arange(WIDTH) // HEAD_DIM
    rows = jnp.arange(GATE_COLS)[:, None]
    mats = [(rows == grp + lane_head[None, :]).astype(F32) for grp in (COL_B, COL_A, COL_A2, COL_A3)]
    return jnp.stack(mats)


def _chunk_last_selector(tb):
    nchunk = tb // CHUNK
    return (jnp.arange(tb)[None, :] == (jnp.arange(nchunk)[:, None] * CHUNK + CHUNK - 1)).astype(F32)


def _split_w_in(w):
    fw = WIDTH
    o_ff = 4 * fw
    o_g = o_ff + HEADS
    o_ga = o_g + 4 * fw
    o_gb = o_ga + HEADS
    w_main = jnp.concatenate([w[:, :o_ff], w[:, o_g:o_ga]], axis=1).astype(BF16)
    cols = {"f": w[:, o_ff:o_g], "a": w[:, o_ga:o_gb], "b": w[:, o_gb:o_gb + HEADS]}
    pad = jnp.zeros((w.shape[0], GATE_COLS - len(GATE_GROUPS) * HEADS), w.dtype)
    w_small = jnp.concatenate([cols[t] for t in GATE_GROUPS] + [pad], axis=1).astype(BF16)
    return w_main, w_small


def _gate_rows(b_f, dt_bias, a_log):
    z = jnp.zeros((HEADS,), F32)
    pad = jnp.zeros((GATE_COLS - len(GATE_GROUPS) * HEADS,), F32)
    add = {"f": b_f, "a": dt_bias, "b": z}
    alog = {"f": z, "a": a_log, "b": z}
    add_row = jnp.concatenate([add[t] for t in GATE_GROUPS] + [pad]).reshape(1, GATE_COLS)
    alog_row = jnp.concatenate([alog[t] for t in GATE_GROUPS] + [pad]).reshape(1, GATE_COLS)
    return add_row.astype(F32), alog_row.astype(F32)


def kernel(x, c, norm_g, w_ada, b_ada, w_in, b_fgate, fox_qn_g, fox_kn_g, gdn_conv_w, gdn_A_log,
           gdn_dt_bias, gdn_norm_g, w_out, final_g):
    bsz, s, d = x.shape
    depth = w_in.shape[0]
    expand = _expand_matrices()
    sel = _chunk_last_selector(min(ROW_BLOCK, s))
    tk = min(FOX_KEYS, s // 2)
    for l in range(depth):
        mod = _ada(c, w_ada[l], b_ada[l])
        shift, scale, gate = (mod[:, k * d:(k + 1) * d].reshape(bsz, 1, d) for k in range(3))
        w_main, w_small = _split_w_in(w_in[l])
        p, ps = _proj(x, shift, scale, norm_g[l].reshape(1, d), w_main, w_small)

        add_row, alog_row = _gate_rows(b_fgate[l], gdn_dt_bias[l], gdn_A_log[l])
        ft, fx, ga, gb, sm = _gates(ps, add_row, alog_row)

        qg = fox_qn_g[l].reshape(1, HEAD_DIM)
        kg = fox_kn_g[l].reshape(1, HEAD_DIM)
        qn, kn = _foxprep(p, qg, kg)
        lo = _fox_block_start(ft, qg, kg, tk)
        fox_o = _fox(lo, qn, kn, p, fx, tk)

        gq, gk, gkb, gvb, gkbe, gqe, gkd, egl = _gdnprep(p, gdn_conv_w[l], sm, expand, sel)
        gdn_o = _gdn(gq, gk, gkb, gvb, gkbe, gqe, gkd, ga, gb, egl)

        x = _out(x, fox_o, p, gdn_o, gate, gdn_norm_g[l].reshape(1, HEAD_DIM), w_out[l].astype(BF16),
                 final_g.reshape(1, d), final_norm=(l == depth - 1))
    return x
```
